```python
import math
import jax, jax.numpy as jnp
from jax import lax
import numpy as np


D_MODEL = 1024
BATCH = 4
SEQ = 8192
DEPTH = 2

PLE_DIM = 256
DA_HEADS = 4
DA_QK_DIM = 64
DA_V_DIM = 128
DA_WIDTH = DA_HEADS * DA_V_DIM
RET_HEADS = 4
RET_HEAD_DIM = 64
RET_WIDTH = RET_HEADS * RET_HEAD_DIM
RET_CHUNK = 128
S5_WIDTH = 256
S5_GROUP = 16
S5_GROUPS = S5_WIDTH // S5_GROUP
S5_STATE = 64
MIX_WIDTH = DA_WIDTH + RET_WIDTH + S5_WIDTH
D_FF = 2816
CONV_WIDTH = 3
ROPE_THETA = 10000.0
Q_BLOCK = 128
LN_EPS = 1e-5
ALPHA = (2 * DEPTH) ** 0.25
BETA = (8 * DEPTH) ** -0.25

COL_DA_Q = 0
COL_DA_K = COL_DA_Q + DA_HEADS * 2 * DA_QK_DIM
COL_DA_V = COL_DA_K + DA_HEADS * 2 * DA_QK_DIM
COL_RET_Q = COL_DA_V + DA_WIDTH
COL_RET_K = COL_RET_Q + RET_WIDTH
COL_RET_V = COL_RET_K + RET_WIDTH
COL_RET_G = COL_RET_V + RET_WIDTH
COL_S5_U = COL_RET_G + RET_WIDTH
IN_COLS = COL_S5_U + S5_WIDTH

kernel_name = 'hybrid_diffattn_s5_retention_encoder'

F32 = jnp.float32


def layer_norm(x, g, b):
    xf = x.astype(F32)
    mu = jnp.mean(xf, axis=-1, keepdims=True)
    var = jnp.mean(jnp.square(xf - mu), axis=-1, keepdims=True)
    y = (xf - mu) * lax.rsqrt(var + LN_EPS)
    return (y * g.astype(F32) + b.astype(F32)).astype(x.dtype)


def rms_norm(x, g, eps=1e-6):
    xf = x.astype(F32)
    y = xf * lax.rsqrt(jnp.mean(jnp.square(xf), axis=-1, keepdims=True) + eps)
    return (y * g.astype(F32)).astype(x.dtype)


def rope_tables(positions, dim):
    inv_freq = ROPE_THETA ** (-jnp.arange(0, dim, 2, dtype=F32) / dim)
    ang = positions.astype(F32)[..., None] * inv_freq
    return jnp.cos(ang), jnp.sin(ang)


def apply_rope(x, cos, sin):
    extra = x.ndim - 3
    shp = cos.shape[:2] + (1,) * extra + cos.shape[-1:]
    c = cos.reshape(shp).astype(x.dtype)
    s = sin.reshape(shp).astype(x.dtype)
    x1, x2 = jnp.split(x, 2, axis=-1)
    return jnp.concatenate([x1 * c - x2 * s, x2 * c + x1 * s], axis=-1)


def diff_attention(q, k, v, cos, sin, lam, subln_g, lambda_init):
    Bsz, L, H, _, d = q.shape
    dv = v.shape[-1]
    nb = L // Q_BLOCK
    q = apply_rope(q, cos, sin) * (d ** -0.5)
    k = apply_rope(k, cos, sin)
    q_blocks = q.reshape(Bsz, nb, Q_BLOCK, H, 2, d).transpose(1, 0, 3, 4, 2, 5)
    k_t = k.transpose(0, 2, 3, 1, 4)
    v_t = v.transpose(0, 2, 1, 3)

    def attend(qb):
        s = jnp.einsum('bhmqd,bhmkd->bhmqk', qb, k_t).astype(F32)
        a = jax.nn.softmax(s, axis=-1)
        w = (a[:, :, 0] - lam * a[:, :, 1]).astype(v.dtype)
        return jnp.einsum('bhqk,bhkv->bhqv', w, v_t)

    o = lax.map(attend, q_blocks)
    o = o.transpose(1, 0, 3, 2, 4).reshape(Bsz, L, H, dv)
    o = rms_norm(o, subln_g) * (1.0 - lambda_init)
    return o.reshape(Bsz, L, H * dv)


def _retention_causal(q, k, v, log_gamma, include_diag):
    Bsz, L, H, d = q.shape
    dv = v.shape[-1]
    C = RET_CHUNK
    nc = L // C

    def chunks(t):
        return t.reshape(Bsz, nc, C, H, t.shape[-1]).transpose(1, 0, 3, 2, 4)

    qc, kc, vc = chunks(q), chunks(k), chunks(v)
    idx = jnp.arange(C, dtype=F32)
    diff = idx[:, None] - idx[None, :]
    mask = (diff >= 0) if include_diag else (diff > 0)
    lg = log_gamma[:, None, None]
    decay_in = jnp.where(mask[None], jnp.exp(lg * jnp.maximum(diff, 0.0)[None]), 0.0)
    s = jnp.einsum('nbhqd,nbhkd->nbhqk', qc, kc) * decay_in.astype(q.dtype)
    inner = jnp.einsum('nbhqk,nbhkv->nbhqv', s, vc)
    k_decay = jnp.exp(log_gamma[:, None] * (C - 1 - idx)[None]).astype(q.dtype)
    q_decay = jnp.exp(log_gamma[:, None] * (idx + 1)[None]).astype(q.dtype)
    chunk_decay = jnp.exp(log_gamma * C).astype(q.dtype)
    kv = jnp.einsum('nbhkd,nbhkv->nbhdv', kc * k_decay[:, :, None], vc)

    def step(state, kv_c):
        return chunk_decay[:, None, None] * state + kv_c, state

    _, state_before = lax.scan(step, jnp.zeros_like(kv[0]), kv)
    cross = jnp.einsum('nbhqd,nbhdv->nbhqv', qc * q_decay[:, :, None], state_before)
    return (inner + cross).transpose(1, 0, 3, 2, 4).reshape(Bsz, L, H, dv)


def retention_block(q, k, v, g, cos, sin, log_gamma, gn_g, gn_b):
    Bsz, L, H, d = q.shape
    q = apply_rope(q, cos, sin)
    k = apply_rope(k, cos, sin) * (d ** -0.5)
    fwd = _retention_causal(q, k, v, log_gamma, True)
    bwd = jnp.flip(_retention_causal(jnp.flip(q, 1), jnp.flip(k, 1), jnp.flip(v, 1), log_gamma, False), 1)
    o = layer_norm(fwd + bwd, gn_g, gn_b)
    return jax.nn.silu(g) * o.reshape(Bsz, L, H * v.shape[-1])


def _complex_linear_combine(left, right):
    ar1, ai1, br1, bi1 = left
    ar2, ai2, br2, bi2 = right
    ar = ar2 * ar1 - ai2 * ai1
    ai = ar2 * ai1 + ai2 * ar1
    br = ar2 * br1 - ai2 * bi1 + br2
    bi = ar2 * bi1 + ai2 * br1 + bi2
    return ar, ai, br, bi


def s5_block(u, A_re, A_im, log_dt, B_re, B_im, C_re, C_im, D, glu_w, glu_b):
    Bsz, L, _ = u.shape
    dt_ = u.dtype
    ug = u.reshape(Bsz, L, S5_GROUPS, S5_GROUP)
    y = D.reshape(S5_GROUPS, S5_GROUP).astype(dt_) * ug
    for direction in range(2):
        a_re = A_re[direction].astype(F32)
        a_im = A_im[direction].astype(F32)
        step = jnp.exp(log_dt[direction].astype(F32))[:, None]
        e = jnp.exp(step * a_re)
        abar_re = e * jnp.cos(step * a_im)
        abar_im = e * jnp.sin(step * a_im)
        den = a_re * a_re + a_im * a_im
        nr = abar_re - 1.0
        ni = abar_im
        coef_re = (nr * a_re + ni * a_im) / den
        coef_im = (ni * a_re - nr * a_im) / den
        b_re = B_re[direction].astype(F32)
        b_im = B_im[direction].astype(F32)
        bb_re = (coef_re[..., None] * b_re - coef_im[..., None] * b_im).astype(dt_)
        bb_im = (coef_re[..., None] * b_im + coef_im[..., None] * b_re).astype(dt_)
        bu_re = jnp.einsum('blgc,gpc->blgp', ug, bb_re)
        bu_im = jnp.einsum('blgc,gpc->blgp', ug, bb_im)
        a_seq_re = jnp.broadcast_to(abar_re.astype(dt_)[None, None], (1, L, S5_GROUPS, S5_STATE))
        a_seq_im = jnp.broadcast_to(abar_im.astype(dt_)[None, None], (1, L, S5_GROUPS, S5_STATE))
        _, _, x_re, x_im = lax.associative_scan(
            _complex_linear_combine, (a_seq_re, a_seq_im, bu_re, bu_im),
            reverse=(direction == 1), axis=1)
        y = y + jnp.einsum('blgp,gcp->blgc', x_re, C_re[direction].astype(dt_)) \
              - jnp.einsum('blgp,gcp->blgc', x_im, C_im[direction].astype(dt_))
    y = jax.nn.gelu(y.reshape(Bsz, L, S5_WIDTH))
    return y * jax.nn.sigmoid(y @ glu_w + glu_b)


def conv_ffn(x, w_up, conv_w, conv_b, w_down):
    h = x @ w_up
    gate, val = jnp.split(h, 2, axis=-1)
    pad = CONV_WIDTH // 2
    gate = lax.conv_general_dilated(
        gate, conv_w[:, None, :].astype(gate.dtype), window_strides=(1,),
        padding=((pad, pad),), dimension_numbers=('NWC', 'WIO', 'NWC'),
        feature_group_count=D_FF) + conv_b
    return (jax.nn.gelu(gate) * val) @ w_down


def setup_inputs(seed: int = 0) -> dict:
    key = jax.random.key(seed)
    ks = jax.random.split(key, 40)
    nrm = lambda k, shape, s: jax.random.normal(k, shape, F32) * s
    x = jax.random.normal(ks[0], (BATCH, SEQ, D_MODEL), F32)
    p = jax.random.normal(ks[1], (DEPTH, BATCH, SEQ, PLE_DIM), F32)
    positions = jnp.broadcast_to(jnp.arange(SEQ, dtype=jnp.int32)[None], (BATCH, SEQ))
    col_scale = jnp.ones((IN_COLS,), F32)
    col_scale = col_scale.at[COL_DA_V:COL_RET_Q].set(BETA).at[COL_RET_V:COL_RET_G].set(BETA)
    w_in = nrm(ks[2], (DEPTH, D_MODEL, IN_COLS), D_MODEL ** -0.5) * col_scale
    da_lambda_q1 = nrm(ks[3], (DEPTH, DA_QK_DIM), 0.1)
    da_lambda_k1 = nrm(ks[4], (DEPTH, DA_QK_DIM), 0.1)
    da_lambda_q2 = nrm(ks[5], (DEPTH, DA_QK_DIM), 0.1)
    da_lambda_k2 = nrm(ks[6], (DEPTH, DA_QK_DIM), 0.1)
    da_subln_g = 1.0 + nrm(ks[7], (DEPTH, DA_V_DIM), 0.02)
    ret_gn_g = 1.0 + nrm(ks[8], (DEPTH, RET_HEAD_DIM), 0.02)
    ret_gn_b = nrm(ks[9], (DEPTH, RET_HEAD_DIM), 0.02)
    n_idx = jnp.arange(S5_STATE, dtype=F32)
    s5_A_re = -0.5 + nrm(ks[10], (DEPTH, 2, S5_GROUPS, S5_STATE), 0.01)
    s5_A_im = math.pi * n_idx + nrm(ks[11], (DEPTH, 2, S5_GROUPS, S5_STATE), 0.01)
    s5_log_dt = jax.random.uniform(ks[12], (DEPTH, 2, S5_GROUPS), F32, math.log(0.001), math.log(0.1))
    s5_B_re = nrm(ks[13], (DEPTH, 2, S5_GROUPS, S5_STATE, S5_GROUP), (2 * S5_GROUP) ** -0.5)
    s5_B_im = nrm(ks[14], (DEPTH, 2, S5_GROUPS, S5_STATE, S5_GROUP), (2 * S5_GROUP) ** -0.5)
    s5_C_re = nrm(ks[15], (DEPTH, 2, S5_GROUPS, S5_GROUP, S5_STATE), (2 * S5_STATE) ** -0.5)
    s5_C_im = nrm(ks[16], (DEPTH, 2, S5_GROUPS, S5_GROUP, S5_STATE), (2 * S5_STATE) ** -0.5)
    s5_D = nrm(ks[17], (DEPTH, S5_WIDTH), 1.0)
    s5_glu_w = nrm(ks[18], (DEPTH, S5_WIDTH, S5_WIDTH), S5_WIDTH ** -0.5)
    s5_glu_b = nrm(ks[19], (DEPTH, S5_WIDTH), 0.02)
    w_out = nrm(ks[20], (DEPTH, MIX_WIDTH, D_MODEL), MIX_WIDTH ** -0.5 * BETA)
    ln1_g = 1.0 + nrm(ks[21], (DEPTH, D_MODEL), 0.02)
    ln1_b = nrm(ks[22], (DEPTH, D_MODEL), 0.02)
    ffn_w_up = nrm(ks[23], (DEPTH, D_MODEL, 2 * D_FF), D_MODEL ** -0.5 * BETA)
    ffn_conv_w = nrm(ks[24], (DEPTH, CONV_WIDTH, D_FF), CONV_WIDTH ** -0.5)
    ffn_conv_b = nrm(ks[25], (DEPTH, D_FF), 0.02)
    ffn_w_down = nrm(ks[26], (DEPTH, D_FF, D_MODEL), D_FF ** -0.5 * BETA)
    ple_w = nrm(ks[27], (DEPTH, PLE_DIM, D_MODEL), PLE_DIM ** -0.5 * BETA)
    ple_gate_w = nrm(ks[28], (DEPTH, D_MODEL, D_MODEL), D_MODEL ** -0.5)
    ln2_g = 1.0 + nrm(ks[29], (DEPTH, D_MODEL), 0.02)
    ln2_b = nrm(ks[30], (DEPTH, D_MODEL), 0.02)
    return {'x': x, 'p': p, 'positions': positions, 'w_in': w_in,
            'da_lambda_q1': da_lambda_q1, 'da_lambda_k1': da_lambda_k1,
            'da_lambda_q2': da_lambda_q2, 'da_lambda_k2': da_lambda_k2,
            'da_subln_g': da_subln_g, 'ret_gn_g': ret_gn_g, 'ret_gn_b': ret_gn_b,
            's5_A_re': s5_A_re, 's5_A_im': s5_A_im, 's5_log_dt': s5_log_dt,
            's5_B_re': s5_B_re, 's5_B_im': s5_B_im, 's5_C_re': s5_C_re, 's5_C_im': s5_C_im,
            's5_D': s5_D, 's5_glu_w': s5_glu_w, 's5_glu_b': s5_glu_b,
            'w_out': w_out, 'ln1_g': ln1_g, 'ln1_b': ln1_b,
            'ffn_w_up': ffn_w_up, 'ffn_conv_w': ffn_conv_w, 'ffn_conv_b': ffn_conv_b,
            'ffn_w_down': ffn_w_down, 'ple_w': ple_w, 'ple_gate_w': ple_gate_w,
            'ln2_g': ln2_g, 'ln2_b': ln2_b}


def reference(x, p, positions, w_in, da_lambda_q1, da_lambda_k1, da_lambda_q2, da_lambda_k2,
              da_subln_g, ret_gn_g, ret_gn_b, s5_A_re, s5_A_im, s5_log_dt, s5_B_re, s5_B_im,
              s5_C_re, s5_C_im, s5_D, s5_glu_w, s5_glu_b, w_out, ln1_g, ln1_b,
              ffn_w_up, ffn_conv_w, ffn_conv_b, ffn_w_down, ple_w, ple_gate_w, ln2_g, ln2_b):
    Bsz, L, _ = x.shape
    cos, sin = rope_tables(positions, DA_QK_DIM)
    log_gamma = jnp.log(1.0 - 2.0 ** (-5.0 - jnp.arange(RET_HEADS, dtype=F32)))
    for i in range(DEPTH):
        lambda_init = 0.8 - 0.6 * math.exp(-0.3 * i)
        z = x @ w_in[i]
        dq = z[..., COL_DA_Q:COL_DA_K].reshape(Bsz, L, DA_HEADS, 2, DA_QK_DIM)
        dk = z[..., COL_DA_K:COL_DA_V].reshape(Bsz, L, DA_HEADS, 2, DA_QK_DIM)
        dv = z[..., COL_DA_V:COL_RET_Q].reshape(Bsz, L, DA_HEADS, DA_V_DIM)
        lam = (jnp.exp(jnp.sum(da_lambda_q1[i].astype(F32) * da_lambda_k1[i].astype(F32)))
               - jnp.exp(jnp.sum(da_lambda_q2[i].astype(F32) * da_lambda_k2[i].astype(F32)))
               + lambda_init)
        y_da = diff_attention(dq, dk, dv, cos, sin, lam, da_subln_g[i], lambda_init)
        rq = z[..., COL_RET_Q:COL_RET_K].reshape(Bsz, L, RET_HEADS, RET_HEAD_DIM)
        rk = z[..., COL_RET_K:COL_RET_V].reshape(Bsz, L, RET_HEADS, RET_HEAD_DIM)
        rv = z[..., COL_RET_V:COL_RET_G].reshape(Bsz, L, RET_HEADS, RET_HEAD_DIM)
        rg = z[..., COL_RET_G:COL_S5_U]
        y_ret = retention_block(rq, rk, rv, rg, cos, sin, log_gamma, ret_gn_g[i], ret_gn_b[i])
        u = z[..., COL_S5_U:IN_COLS]
        y_s5 = s5_block(u, s5_A_re[i], s5_A_im[i], s5_log_dt[i], s5_B_re[i], s5_B_im[i],
                        s5_C_re[i], s5_C_im[i], s5_D[i], s5_glu_w[i], s5_glu_b[i])
        mix = jnp.concatenate([y_da, y_ret, y_s5], axis=-1) @ w_out[i]
        x = layer_norm(ALPHA * x + mix, ln1_g[i], ln1_b[i])
        f = conv_ffn(x, ffn_w_up[i], ffn_conv_w[i], ffn_conv_b[i], ffn_w_down[i])
        ple = (p[i] @ ple_w[i]) * jax.nn.sigmoid(x @ ple_gate_w[i])
        x = layer_norm(ALPHA * x + f + ple, ln2_g[i], ln2_b[i])
    return x
```

```python
import functools
import math

import jax
import jax.numpy as jnp
from jax import lax
from jax.experimental import pallas as pl
from jax.experimental.pallas import tpu as pltpu

F32 = jnp.float32
BF16 = jnp.bfloat16

D_MODEL = 1024
PLE_DIM = 256
DA_HEADS = 4
DA_QK_DIM = 64
DA_V_DIM = 128
DA_WIDTH = DA_HEADS * DA_V_DIM
RET_HEADS = 4
RET_HEAD_DIM = 64
RET_WIDTH = RET_HEADS * RET_HEAD_DIM
S5_WIDTH = 256
S5_GROUP = 16
S5_GROUPS = S5_WIDTH // S5_GROUP
S5_STATE = 64
D_FF = 2816
ROPE_THETA = 10000.0
LN_EPS = 1e-5
RMS_EPS = 1e-6

COL_DA_Q = 0
COL_DA_K = COL_DA_Q + DA_HEADS * 2 * DA_QK_DIM
COL_DA_V = COL_DA_K + DA_HEADS * 2 * DA_QK_DIM
COL_RET_Q = COL_DA_V + DA_WIDTH
COL_RET_K = COL_RET_Q + RET_WIDTH
COL_RET_V = COL_RET_K + RET_WIDTH
COL_RET_G = COL_RET_V + RET_WIDTH
COL_S5_U = COL_RET_G + RET_WIDTH
IN_COLS = COL_S5_U + S5_WIDTH

LANES = 128
MXU_WIDTH = 256
S5_CHUNK = 8
S5_HALF = 128
RET_CHUNK = 256
FF_CHUNK = 256
LOG2E = 1.4426950408889634
NEG_BIG = -1e30
VMEM_LIMIT = 56 * 1024 * 1024


def _params(sem, vmem=VMEM_LIMIT):
    return pltpu.CompilerParams(dimension_semantics=sem, vmem_limit_bytes=vmem)


def _const_spec(shape):
    nd = len(shape)
    return pl.BlockSpec(shape, lambda *_: (0,) * nd, pipeline_mode=pl.Buffered(1))


def _layer_norm(x, g, b):
    mu = jnp.mean(x, axis=-1, keepdims=True)
    d = x - mu
    var = jnp.mean(d * d, axis=-1, keepdims=True)
    return d * lax.rsqrt(var + LN_EPS) * g + b


def _gelu_tanh(x):
    return 0.5 * x * (1.0 + jnp.tanh(math.sqrt(2.0 / math.pi) * (x + 0.044715 * (x * x * x))))


def _sigmoid(x):
    return 1.0 / (1.0 + jnp.exp(-x))


def _dot(a, b):
    return jnp.dot(a, b, preferred_element_type=F32)


def _dot_nt(a, b):
    return lax.dot_general(a, b, (((1,), (1,)), ((), ())), preferred_element_type=F32)


def _dot_tn(a, b):
    return lax.dot_general(a, b, (((0,), (0,)), ((), ())), preferred_element_type=F32)


def _dot_split(x, w):
    hi = x.astype(BF16)
    lo = (x - hi.astype(F32)).astype(BF16)
    return _dot(hi, w) + _dot(lo, w)


def _rope_table_kernel(pos_ref, freq_ref, sign_ref, cos_ref, sin_ref):
    ang = pos_ref[...].astype(F32) * freq_ref[...]
    cos_ref[...] = jnp.cos(ang)
    sin_ref[...] = jnp.sin(ang) * sign_ref[...]


def rope_tables(positions, tm=1024):
    n = positions.size
    half = DA_QK_DIM // 2
    inv_freq = ROPE_THETA ** (-jnp.arange(0, DA_QK_DIM, 2, dtype=F32) / DA_QK_DIM)
    freq_row = jnp.tile(inv_freq, LANES // half).reshape(1, LANES)
    lane = jnp.arange(LANES)
    sign_row = jnp.where(lane % DA_QK_DIM < half, -1.0, 1.0).astype(F32).reshape(1, LANES)
    pos = positions.reshape(n, 1)
    tm = min(tm, n)
    return pl.pallas_call(
        _rope_table_kernel,
        grid=(n // tm,),
        in_specs=[pl.BlockSpec((tm, 1), lambda i: (i, 0)),
                  _const_spec((1, LANES)), _const_spec((1, LANES))],
        out_specs=[pl.BlockSpec((tm, LANES), lambda i: (i, 0)),
                   pl.BlockSpec((tm, LANES), lambda i: (i, 0))],
        out_shape=[jax.ShapeDtypeStruct((n, LANES), F32)] * 2,
        compiler_params=_params(("parallel",)),
        name="rope_tables",
    )(pos, freq_row, sign_row)


def _rope(x, cos, sin, first_half):
    swapped = jnp.where(first_half, pltpu.roll(x, LANES - DA_QK_DIM // 2, 1),
                        pltpu.roll(x, DA_QK_DIM // 2, 1))
    return x * cos + swapped * sin


def _in_proj_kernel(x_ref, w_ref, cos_ref, sin_ref, da_ref, ret_ref, g_ref, u_ref):
    xb = x_ref[...].astype(BF16)
    cos = cos_ref[...]
    sin = sin_ref[...]
    lane = lax.broadcasted_iota(jnp.int32, cos.shape, 1)
    first_half = (lane % DA_QK_DIM) < (DA_QK_DIM // 2)
    q_scale = DA_QK_DIM ** -0.5 * LOG2E
    k_scale = RET_HEAD_DIM ** -0.5

    def proj(col):
        return _dot(xb, w_ref[:, col:col + MXU_WIDTH])

    def roped(z, scale):
        parts = [_rope(z[:, a:a + LANES], cos, sin, first_half) for a in (0, LANES)]
        out = jnp.concatenate(parts, axis=-1)
        return out if scale is None else out * scale

    for c in range(0, COL_DA_K, MXU_WIDTH):
        da_ref[:, c:c + MXU_WIDTH] = roped(proj(c), q_scale).astype(BF16)
    for c in range(COL_DA_K, COL_DA_V, MXU_WIDTH):
        da_ref[:, c:c + MXU_WIDTH] = roped(proj(c), None).astype(BF16)
    for c in range(COL_DA_V, COL_RET_Q, MXU_WIDTH):
        da_ref[:, c:c + MXU_WIDTH] = proj(c).astype(BF16)
    ret_ref[:, 0:RET_WIDTH] = roped(proj(COL_RET_Q), None).astype(BF16)
    ret_ref[:, RET_WIDTH:2 * RET_WIDTH] = roped(proj(COL_RET_K), k_scale).astype(BF16)
    ret_ref[:, 2 * RET_WIDTH:3 * RET_WIDTH] = proj(COL_RET_V).astype(BF16)
    g_ref[...] = proj(COL_RET_G)
    u = proj(COL_S5_U).astype(BF16)
    u_ref[0] = u[:, :S5_HALF]
    u_ref[1] = u[:, S5_HALF:]


def in_proj(x, w_bf16, cos, sin, tm=512):
    n = x.shape[0]
    tm = min(tm, n)
    row = lambda i: (i, 0)
    return pl.pallas_call(
        _in_proj_kernel,
        grid=(n // tm,),
        in_specs=[pl.BlockSpec((tm, D_MODEL), row), _const_spec((D_MODEL, IN_COLS)),
                  pl.BlockSpec((tm, LANES), row), pl.BlockSpec((tm, LANES), row)],
        out_specs=[pl.BlockSpec((tm, COL_RET_Q), row), pl.BlockSpec((tm, 3 * RET_WIDTH), row),
                   pl.BlockSpec((tm, RET_WIDTH), row),
                   pl.BlockSpec((2, tm, S5_HALF), lambda i: (0, i, 0))],
        out_shape=[jax.ShapeDtypeStruct((n, COL_RET_Q), BF16),
                   jax.ShapeDtypeStruct((n, 3 * RET_WIDTH), BF16),
                   jax.ShapeDtypeStruct((n, RET_WIDTH), F32),
                   jax.ShapeDtypeStruct((2, n, S5_HALF), BF16)],
        compiler_params=_params(("parallel",)),
        name="in_proj",
    )(x, w_bf16, cos, sin)


def _diff_attn_kernel(q_ref, k_ref, v_ref, lam_ref, g_ref, o_ref,
                      vext_ref, qm_ref, m_ref, acc_ref, *, tk, out_scale):
    seq = k_ref.shape[0]
    tq = q_ref.shape[0]

    @pl.when(pl.program_id(2) == 0)
    def _():
        vext_ref[:, :DA_V_DIM] = v_ref[...]
        vext_ref[:, DA_V_DIM:] = jnp.ones((seq, MXU_WIDTH - DA_V_DIM), BF16)

    q = q_ref[...]
    lane = lax.broadcasted_iota(jnp.int32, q.shape, 1)
    zero = jnp.zeros_like(q)
    qm_ref[0] = jnp.where(lane < DA_QK_DIM, q, zero)
    qm_ref[1] = jnp.where(lane >= DA_QK_DIM, q, zero)
    m_ref[...] = jnp.full(m_ref.shape, NEG_BIG, F32)
    acc_ref[...] = jnp.zeros(acc_ref.shape, F32)

    def kv_step(i, carry):
        start = pl.multiple_of(i * tk, tk)
        k_t = k_ref[pl.ds(start, tk), :]
        v_t = vext_ref[pl.ds(start, tk), :]
        for mi in range(2):
            s = _dot_nt(qm_ref[mi], k_t)
            m_old = m_ref[mi]
            m_new = jnp.maximum(m_old, jnp.max(s, axis=-1, keepdims=True))
            alpha = jnp.exp2(m_old - m_new)
            p = jnp.exp2(s - m_new).astype(BF16)
            acc_ref[mi] = acc_ref[mi] * alpha + _dot(p, v_t)
            m_ref[mi] = m_new
        return carry

    lax.fori_loop(0, seq // tk, kv_step, 0)

    a0 = acc_ref[0]
    a1 = acc_ref[1]
    o = a0[:, :DA_V_DIM] / a0[:, DA_V_DIM:] - lam_ref[...] * (a1[:, :DA_V_DIM] / a1[:, DA_V_DIM:])
    ms = jnp.mean(o * o, axis=-1, keepdims=True)
    o_ref[...] = (o * lax.rsqrt(ms + RMS_EPS) * g_ref[...] * out_scale).astype(o_ref.dtype)


def diff_attention(da, lam_row, subln_row, out_scale, bsz, seq, tq=512, tk=512):
    tq = min(tq, seq)
    tk = min(tk, seq)
    kern = functools.partial(_diff_attn_kernel, tk=tk, out_scale=out_scale)
    return pl.pallas_call(
        kern,
        grid=(bsz, DA_HEADS, seq // tq),
        in_specs=[pl.BlockSpec((None, tq, LANES), lambda b, h, i: (b, i, h)),
                  pl.BlockSpec((None, seq, LANES), lambda b, h, i: (b, 0, DA_HEADS + h)),
                  pl.BlockSpec((None, seq, LANES), lambda b, h, i: (b, 0, 2 * DA_HEADS + h)),
                  _const_spec((1, DA_V_DIM)), _const_spec((1, DA_V_DIM))],
        out_specs=pl.BlockSpec((None, tq, DA_V_DIM), lambda b, h, i: (b, i, h)),
        out_shape=jax.ShapeDtypeStruct((bsz, seq, DA_WIDTH), BF16),
        scratch_shapes=[pltpu.VMEM((seq, MXU_WIDTH), BF16),
                        pltpu.VMEM((2, tq, LANES), BF16),
                        pltpu.VMEM((2, tq, 1), F32),
                        pltpu.VMEM((2, tq, MXU_WIDTH), F32)],
        compiler_params=_params(("parallel", "parallel", "arbitrary")),
        name="diff_attention",
    )(da, da, da, lam_row, subln_row)


def _ret_state_kernel(kf_ref, vf_ref, kb_ref, vb_ref, dkf_ref, dkb_ref, gc_ref,
                      sf_out, sb_out, sf_ref, sb_ref):
    @pl.when(pl.program_id(1) == 0)
    def _():
        sf_ref[...] = jnp.zeros(sf_ref.shape, F32)
        sb_ref[...] = jnp.zeros(sb_ref.shape, F32)

    sf_out[...] = sf_ref[...]
    sb_out[...] = sb_ref[...]

    def update(s_ref, k_ref, v_ref, dk_ref):
        kd = (k_ref[...].astype(F32) * dk_ref[...]).astype(BF16)
        v = v_ref[...]
        for pr in range(RET_WIDTH // LANES):
            sl = slice(pr * LANES, (pr + 1) * LANES)
            kv = _dot_tn(kd[:, sl], v[:, sl])
            keep = gc_ref[sl, :]
            s_ref[sl, :] = keep * s_ref[sl, :] + jnp.where(keep > 0.0, kv, 0.0)

    update(sf_ref, kf_ref, vf_ref, dkf_ref)
    update(sb_ref, kb_ref, vb_ref, dkb_ref)


def _ret_out_kernel(q_ref, k_ref, v_ref, g_ref, sf_ref, sb_ref, dec_ref, dqf_ref, dqb_ref,
                    avg_ref, gng_ref, gnb_ref, o_ref):
    q = q_ref[...]
    k = k_ref[...]
    v = v_ref[...]
    qf = q.astype(F32)
    lane = lax.broadcasted_iota(jnp.int32, (q.shape[0], LANES), 1)
    zero = jnp.zeros((q.shape[0], LANES), BF16)
    parts = []
    for pr in range(RET_WIDTH // LANES):
        sl = slice(pr * LANES, (pr + 1) * LANES)
        qp, kp, vp = q[:, sl], k[:, sl], v[:, sl]
        acc = _dot((qf[:, sl] * dqf_ref[:, sl]).astype(BF16), sf_ref[sl, :].astype(BF16))
        acc += _dot((qf[:, sl] * dqb_ref[:, sl]).astype(BF16), sb_ref[sl, :].astype(BF16))
        for hh in range(LANES // RET_HEAD_DIM):
            mine = (lane >= hh * RET_HEAD_DIM) & (lane < (hh + 1) * RET_HEAD_DIM)
            s = _dot_nt(jnp.where(mine, qp, zero), kp) * dec_ref[pr * 2 + hh]
            acc += _dot(s.astype(BF16), jnp.where(mine, vp, zero))
        parts.append(acc)
    o = jnp.concatenate(parts, axis=-1)
    avg = avg_ref[...]
    mu = _dot_split(o, avg)
    d = o - mu
    var = _dot_split(d * d, avg)
    y = d * lax.rsqrt(var + LN_EPS) * gng_ref[...] + gnb_ref[...]
    g = g_ref[...]
    o_ref[...] = (g * _sigmoid(g) * y).astype(o_ref.dtype)


def retention(ret, g, gn_g, gn_b, bsz, seq):
    c = min(RET_CHUNK, seq)
    nc = seq // c
    heads = jnp.arange(RET_HEADS, dtype=F32)
    log_gamma = jnp.log(1.0 - 2.0 ** (-5.0 - heads))
    lg_cols = jnp.repeat(log_gamma, RET_HEAD_DIM)[None, :]
    idx = jnp.arange(c, dtype=F32)[:, None]
    dk_f = jnp.exp(lg_cols * (c - 1 - idx))
    dk_b = jnp.exp(lg_cols * idx)
    dq_f = jnp.exp(lg_cols * (idx + 1))
    dq_b = jnp.exp(lg_cols * (c - idx))
    dist = jnp.abs(idx - idx.T)
    decay = jnp.exp(log_gamma[:, None, None] * dist[None])
    row_head = jnp.arange(RET_WIDTH)[:, None] // RET_HEAD_DIM
    col_head = (jnp.arange(LANES)[None, :] // RET_HEAD_DIM) + 2 * (jnp.arange(RET_WIDTH)[:, None] // LANES)
    gc = jnp.where(row_head == col_head, jnp.exp(lg_cols.T * c), 0.0).astype(F32)
    seg = jnp.arange(RET_WIDTH) // RET_HEAD_DIM
    avg = jnp.where(seg[:, None] == seg[None, :], 1.0 / RET_HEAD_DIM, 0.0).astype(BF16)
    gng = jnp.tile(gn_g, RET_HEADS)[None, :]
    gnb = jnp.tile(gn_b, RET_HEADS)[None, :]

    blk = lambda col: pl.BlockSpec((None, c, RET_WIDTH), lambda b, j: (b, j, col))
    blk_rev = lambda col: pl.BlockSpec((None, c, RET_WIDTH), lambda b, j: (b, nc - 1 - j, col))
    st_shape = jax.ShapeDtypeStruct((bsz, nc, RET_WIDTH, LANES), F32)
    sf, sb = pl.pallas_call(
        _ret_state_kernel,
        grid=(bsz, nc),
        in_specs=[blk(1), blk(2), blk_rev(1), blk_rev(2),
                  _const_spec((c, RET_WIDTH)), _const_spec((c, RET_WIDTH)),
                  _const_spec((RET_WIDTH, LANES))],
        out_specs=[pl.BlockSpec((None, None, RET_WIDTH, LANES), lambda b, j: (b, j, 0, 0)),
                   pl.BlockSpec((None, None, RET_WIDTH, LANES), lambda b, j: (b, nc - 1 - j, 0, 0))],
        out_shape=[st_shape, st_shape],
        scratch_shapes=[pltpu.VMEM((RET_WIDTH, LANES), F32), pltpu.VMEM((RET_WIDTH, LANES), F32)],
        compiler_params=_params(("parallel", "arbitrary")),
        name="retention_state",
    )(ret, ret, ret, ret, dk_f, dk_b, gc)

    st_spec = pl.BlockSpec((None, None, RET_WIDTH, LANES), lambda b, j: (b, j, 0, 0))
    return pl.pallas_call(
        _ret_out_kernel,
        grid=(bsz, nc),
        in_specs=[blk(0), blk(1), blk(2), blk(0), st_spec, st_spec,
                  _const_spec((RET_HEADS, c, c)), _const_spec((c, RET_WIDTH)),
                  _const_spec((c, RET_WIDTH)), _const_spec((RET_WIDTH, RET_WIDTH)),
                  _const_spec((1, RET_WIDTH)), _const_spec((1, RET_WIDTH))],
        out_specs=blk(0),
        out_shape=jax.ShapeDtypeStruct((bsz, seq, RET_WIDTH), BF16),
        compiler_params=_params(("parallel", "parallel")),
        name="retention_out",
    )(ret, ret, ret, g, sf, sb, decay, dq_f, dq_b, avg, gng, gnb)


def _cmul(ar, ai, br, bi):
    return ar * br - ai * bi, ar * bi + ai * br


def s5_matrices(A_re, A_im, log_dt, B_re, B_im, C_re, C_im, D):
    T, G, P, Cn = S5_CHUNK, S5_GROUPS, S5_STATE, S5_GROUP
    gh = S5_HALF // Cn
    step = jnp.exp(log_dt.astype(F32))[..., None]
    a_re = A_re.astype(F32)
    a_im = A_im.astype(F32)
    e = jnp.exp(step * a_re)
    abar_re = e * jnp.cos(step * a_im)
    abar_im = e * jnp.sin(step * a_im)
    den = a_re * a_re + a_im * a_im
    nr = abar_re - 1.0
    ni = abar_im
    coef_re = (nr * a_re + ni * a_im) / den
    coef_im = (ni * a_re - nr * a_im) / den
    b_re = B_re.astype(F32)
    b_im = B_im.astype(F32)
    bb_re = coef_re[..., None] * b_re - coef_im[..., None] * b_im
    bb_im = coef_re[..., None] * b_im + coef_im[..., None] * b_re
    pows = [(jnp.ones_like(abar_re), jnp.zeros_like(abar_re))]
    for _ in range(T):
        pows.append(_cmul(pows[-1][0], pows[-1][1], abar_re, abar_im))
    pw_re = jnp.stack([p[0] for p in pows])
    pw_im = jnp.stack([p[1] for p in pows])
    c_re = C_re.astype(F32)
    c_im = C_im.astype(F32)
    eye = jnp.eye(gh, dtype=F32)
    t_idx = jnp.arange(T)

    def state_in(direction, order):
        pr = pw_re[order, direction]
        pi = pw_im[order, direction]
        vr, vi = _cmul(pr[:, :, None, :], pi[:, :, None, :],
                       jnp.swapaxes(bb_re[direction], 1, 2)[None],
                       jnp.swapaxes(bb_im[direction], 1, 2)[None])
        return vr, vi

    def read_out(direction, order):
        pr = pw_re[order, direction][:, :, None, :]
        pi = pw_im[order, direction][:, :, None, :]
        er, ei = _cmul(c_re[direction][None], c_im[direction][None], pr, pi)
        return er, ei

    def lag_kernel(direction):
        pr = pw_re[:T, direction][:, :, :, None]
        pi = pw_im[:T, direction][:, :, :, None]
        wr, wi = _cmul(pr, pi, bb_re[direction][None], bb_im[direction][None])
        return (jnp.einsum('gop,dgpi->dgoi', c_re[direction], wr)
                - jnp.einsum('gop,dgpi->dgoi', c_im[direction], wi))

    vfr, vfi = state_in(0, T - 1 - t_idx)
    vbr, vbi = state_in(1, t_idx)
    efr, efi = read_out(0, t_idx + 1)
    ebr, ebi = read_out(1, T - t_idx)
    kf = lag_kernel(0)
    kb = lag_kernel(1)
    lag = t_idx[None, :] - t_idx[:, None]
    toe = (jnp.where((lag >= 0)[:, :, None, None, None], kf[jnp.clip(lag, 0, T - 1)], 0.0)
           + jnp.where((lag <= 0)[:, :, None, None, None], kb[jnp.clip(-lag, 0, T - 1)], 0.0))
    d_diag = D.astype(F32).reshape(G, Cn)[:, :, None] * jnp.eye(Cn, dtype=F32)[None]
    toe = toe + jnp.where((lag == 0)[:, :, None, None, None], d_diag[None, None], 0.0)

    mb, tp, mc, a8 = [], [], [], []
    for h in range(2):
        gs = slice(h * gh, (h + 1) * gh)
        sin_blk = lambda v: jnp.einsum('sgcp,gk->sgckp', v[:, gs], eye).reshape(T * gh * Cn, gh * P)
        mb.append(jnp.concatenate([sin_blk(vfr), sin_blk(vfi), sin_blk(vbr), sin_blk(vbi)], axis=1))
        tp.append(jnp.einsum('stgoi,gk->sgitko', toe[:, :, gs], eye).reshape(T * gh * Cn, T * gh * Cn))
        out_blk = lambda v: jnp.einsum('tgop,gk->gptko', v[:, gs], eye).reshape(gh * P, T * gh * Cn)
        mc.append(jnp.concatenate([out_blk(efr), -out_blk(efi), out_blk(ebr), -out_blk(ebi)], axis=0))
        rows = [pw_re[T, 0, gs].reshape(-1), pw_im[T, 0, gs].reshape(-1),
                pw_re[T, 1, gs].reshape(-1), pw_im[T, 1, gs].reshape(-1)]
        a8.append(jnp.stack(rows + rows))
    return (jnp.stack(mb).astype(BF16), jnp.stack(tp).astype(BF16),
            jnp.stack(mc).astype(BF16), jnp.stack(a8))


def _s5_kernel(u_ref, mb_ref, tp_ref, mc_ref, a8_ref, y_ref, w_ref, *, sub):
    rows = u_ref.shape[0]
    ns = a8_ref.shape[1]
    for r in range(0, rows, sub):
        w_ref[r:r + sub, :] = _dot(u_ref[r:r + sub, :], mb_ref[...])

    afr, afi = a8_ref[0:1, :], a8_ref[1:2, :]
    abr, abi = a8_ref[2:3, :], a8_ref[3:4, :]

    def scan_step(j, carry):
        xfr, xfi, xbr, xbi = carry
        jb = rows - 1 - j
        wf_r = w_ref[pl.ds(j, 1), 0:ns]
        wf_i = w_ref[pl.ds(j, 1), ns:2 * ns]
        wb_r = w_ref[pl.ds(jb, 1), 2 * ns:3 * ns]
        wb_i = w_ref[pl.ds(jb, 1), 3 * ns:4 * ns]
        w_ref[pl.ds(j, 1), 0:ns] = xfr
        w_ref[pl.ds(j, 1), ns:2 * ns] = xfi
        w_ref[pl.ds(jb, 1), 2 * ns:3 * ns] = xbr
        w_ref[pl.ds(jb, 1), 3 * ns:4 * ns] = xbi
        nfr = afr * xfr - afi * xfi + wf_r
        nfi = afr * xfi + afi * xfr + wf_i
        nbr = abr * xbr - abi * xbi + wb_r
        nbi = abr * xbi + abi * xbr + wb_i
        return nfr, nfi, nbr, nbi

    z = jnp.zeros((1, ns), F32)
    lax.fori_loop(0, rows, scan_step, (z, z, z, z))

    for r in range(0, rows, sub):
        y_ref[r:r + sub, :] = (_dot(u_ref[r:r + sub, :], tp_ref[...])
                               + _dot(w_ref[r:r + sub, :].astype(BF16), mc_ref[...]))


def s5_mixer(u, mats, bsz, seq):
    mb, tp, mc, a8 = mats
    n = bsz * seq
    rows = seq // S5_CHUNK
    width = S5_CHUNK * S5_HALF
    ns = (S5_HALF // S5_GROUP) * S5_STATE
    u8 = u.reshape(2, n // S5_CHUNK, width)
    kern = functools.partial(_s5_kernel, sub=min(256, rows))
    wspec = lambda a: pl.BlockSpec((None,) + a.shape[1:], lambda h, b: (h, 0, 0),
                                   pipeline_mode=pl.Buffered(1))
    y8 = pl.pallas_call(
        kern,
        grid=(2, bsz),
        in_specs=[pl.BlockSpec((None, rows, width), lambda h, b: (h, b, 0)),
                  wspec(mb), wspec(tp), wspec(mc), wspec(a8)],
        out_specs=pl.BlockSpec((None, rows, width), lambda h, b: (h, b, 0)),
        out_shape=jax.ShapeDtypeStruct((2, n // S5_CHUNK, width), F32),
        scratch_shapes=[pltpu.VMEM((rows, 4 * ns), F32)],
        compiler_params=_params(("arbitrary", "arbitrary")),
        name="s5_mixer",
    )(u8, mb, tp, mc, a8)
    return y8.reshape(2, n, S5_HALF)


def _out_proj_kernel(x_ref, da_ref, ret_ref, y5_ref, gluw_ref, glub_ref, wout_ref,
                     g_ref, b_ref, o_ref, *, alpha):
    y = jnp.concatenate([y5_ref[0], y5_ref[1]], axis=-1)
    ya = _gelu_tanh(y)
    gate = _sigmoid(_dot(ya.astype(BF16), gluw_ref[...]) + glub_ref[...])
    ys5 = (ya * gate).astype(BF16)
    c1 = DA_WIDTH
    c2 = DA_WIDTH + RET_WIDTH
    mix = (_dot(da_ref[...], wout_ref[0:c1, :]) + _dot(ret_ref[...], wout_ref[c1:c2, :])
           + _dot(ys5, wout_ref[c2:, :]))
    o_ref[...] = _layer_norm(alpha * x_ref[...] + mix, g_ref[...], b_ref[...])


def out_proj(x, y_da, y_ret, y5, glu_w, glu_b, w_out, ln_g, ln_b, alpha, tm=512):
    n = x.shape[0]
    tm = min(tm, n)
    row = lambda i: (i, 0)
    return pl.pallas_call(
        functools.partial(_out_proj_kernel, alpha=alpha),
        grid=(n // tm,),
        in_specs=[pl.BlockSpec((tm, D_MODEL), row), pl.BlockSpec((tm, DA_WIDTH), row),
                  pl.BlockSpec((tm, RET_WIDTH), row),
                  pl.BlockSpec((2, tm, S5_HALF), lambda i: (0, i, 0)),
                  _const_spec((S5_WIDTH, S5_WIDTH)), _const_spec((1, S5_WIDTH)),
                  _const_spec((D_MODEL, D_MODEL)),
                  _const_spec((1, D_MODEL)), _const_spec((1, D_MODEL))],
        out_specs=pl.BlockSpec((tm, D_MODEL), row),
        out_shape=jax.ShapeDtypeStruct((n, D_MODEL), F32),
        compiler_params=_params(("parallel",)),
        name="out_proj",
    )(x, y_da, y_ret, y5, glu_w, glu_b, w_out, ln_g, ln_b)


def _ffn_kernel(x_ref, xp_ref, xn_ref, p_ref, wup_ref, cw_ref, cb_ref, wdn_ref,
                plew_ref, gatew_ref, g_ref, b_ref, o_ref, acc_ref, *, alpha, tiles_per_seq):
    tm = x_ref.shape[0]
    i = pl.program_id(0)
    has_prev = ((i % tiles_per_seq) != 0).astype(F32)
    has_next = ((i % tiles_per_seq) != tiles_per_seq - 1).astype(F32)
    x = x_ref[...]
    xb = x.astype(BF16)
    xpb = xp_ref[...].astype(BF16)
    xnb = xn_ref[...].astype(BF16)
    row = lax.broadcasted_iota(jnp.int32, (tm, FF_CHUNK), 0)
    halo = xp_ref.shape[0]

    for c in range(0, D_FF, FF_CHUNK):
        wg = wup_ref[:, c:c + FF_CHUNK]
        gate = _dot(xb, wg)
        val = _dot(xb, wup_ref[:, D_FF + c:D_FF + c + FF_CHUNK])
        before = _dot(xpb, wg)[halo - 1:halo, :] * has_prev
        after = _dot(xnb, wg)[0:1, :] * has_next
        left = jnp.where(row == 0, before, pltpu.roll(gate, 1, 0))
        right = jnp.where(row == tm - 1, after, pltpu.roll(gate, tm - 1, 0))
        conv = (cw_ref[0:1, c:c + FF_CHUNK] * left + cw_ref[1:2, c:c + FF_CHUNK] * gate
                + cw_ref[2:3, c:c + FF_CHUNK] * right + cb_ref[:, c:c + FF_CHUNK])
        act = (_gelu_tanh(conv) * val).astype(BF16)
        contrib = _dot(act, wdn_ref[c:c + FF_CHUNK, :])
        if c == 0:
            acc_ref[...] = contrib
        else:
            acc_ref[...] += contrib

    ple = _dot(p_ref[...].astype(BF16), plew_ref[...]) * _sigmoid(_dot(xb, gatew_ref[...]))
    o_ref[...] = _layer_norm(alpha * x + acc_ref[...] + ple, g_ref[...], b_ref[...])


def conv_ffn_ple(x, p, w_up, conv_w, conv_b, w_down, ple_w, gate_w, ln_g, ln_b, alpha, seq, tm=512):
    n = x.shape[0]
    tm = min(tm, seq)
    halo = 8
    tiles_per_seq = seq // tm
    per = tm // halo
    last = n // halo - 1
    row = lambda i: (i, 0)
    kern = functools.partial(_ffn_kernel, alpha=alpha, tiles_per_seq=tiles_per_seq)
    return pl.pallas_call(
        kern,
        grid=(n // tm,),
        in_specs=[pl.BlockSpec((tm, D_MODEL), row),
                  pl.BlockSpec((halo, D_MODEL), lambda i: (jnp.maximum(i * per - 1, 0), 0)),
                  pl.BlockSpec((halo, D_MODEL), lambda i: (jnp.minimum((i + 1) * per, last), 0)),
                  pl.BlockSpec((tm, PLE_DIM), row),
                  _const_spec((D_MODEL, 2 * D_FF)), _const_spec((3, D_FF)), _const_spec((1, D_FF)),
                  _const_spec((D_FF, D_MODEL)), _const_spec((PLE_DIM, D_MODEL)),
                  _const_spec((D_MODEL, D_MODEL)),
                  _const_spec((1, D_MODEL)), _const_spec((1, D_MODEL))],
        out_specs=pl.BlockSpec((tm, D_MODEL), row),
        out_shape=jax.ShapeDtypeStruct((n, D_MODEL), F32),
        scratch_shapes=[pltpu.VMEM((tm, D_MODEL), F32)],
        compiler_params=_params(("parallel",)),
        name="conv_ffn_ple",
    )(x, x, x, p, w_up, conv_w, conv_b, w_down, ple_w, gate_w, ln_g, ln_b)


def kernel(x, p, positions, w_in, da_lambda_q1, da_lambda_k1, da_lambda_q2, da_lambda_k2,
           da_subln_g, ret_gn_g, ret_gn_b, s5_A_re, s5_A_im, s5_log_dt, s5_B_re, s5_B_im,
           s5_C_re, s5_C_im, s5_D, s5_glu_w, s5_glu_b, w_out, ln1_g, ln1_b,
           ffn_w_up, ffn_conv_w, ffn_conv_b, ffn_w_down, ple_w, ple_gate_w, ln2_g, ln2_b):
    bsz, seq, _ = x.shape
    depth = w_in.shape[0]
    n = bsz * seq
    alpha = (2 * depth) ** 0.25
    cos, sin = rope_tables(positions)
    xf = x.reshape(n, D_MODEL)
    row = lambda v: v.reshape(1, -1).astype(F32)
    for i in range(depth):
        lambda_init = 0.8 - 0.6 * math.exp(-0.3 * i)
        lam = (jnp.exp(jnp.sum(da_lambda_q1[i].astype(F32) * da_lambda_k1[i].astype(F32)))
               - jnp.exp(jnp.sum(da_lambda_q2[i].astype(F32) * da_lambda_k2[i].astype(F32)))
               + lambda_init)
        lam_row = jnp.full((1, DA_V_DIM), lam, F32)
        da, ret, g, u = in_proj(xf, w_in[i].astype(BF16), cos, sin)
        y_da = diff_attention(da.reshape(bsz, seq, -1), lam_row, row(da_subln_g[i]),
                              1.0 - lambda_init, bsz, seq)
        y_ret = retention(ret.reshape(bsz, seq, -1), g.reshape(bsz, seq, -1),
                          ret_gn_g[i].astype(F32), ret_gn_b[i].astype(F32), bsz, seq)
        mats = s5_matrices(s5_A_re[i], s5_A_im[i], s5_log_dt[i], s5_B_re[i], s5_B_im[i],
                           s5_C_re[i], s5_C_im[i], s5_D[i])
        y5 = s5_mixer(u, mats, bsz, seq)
        x1 = out_proj(xf, y_da.reshape(n, -1), y_ret.reshape(n, -1), y5,
                      s5_glu_w[i].astype(BF16), row(s5_glu_b[i]), w_out[i].astype(BF16),
                      row(ln1_g[i]), row(ln1_b[i]), alpha)
        xf = conv_ffn_ple(x1, p[i].reshape(n, PLE_DIM), ffn_w_up[i].astype(BF16),
                          ffn_conv_w[i].astype(F32), row(ffn_conv_b[i]),
                          ffn_w_down[i].astype(BF16), ple_w[i].astype(BF16),
                          ple_gate_w[i].astype(BF16), row(ln2_g[i]), row(ln2_b[i]), alpha, seq)
    return xf.reshape(bsz, seq, D_MODEL)
```

```python
import functools
import math

import jax
import jax.numpy as jnp
from jax import lax
from jax.experimental import pallas as pl
from jax.experimental.pallas import tpu as pltpu

F32 = jnp.float32
BF16 = jnp.bfloat16

D_MODEL = 1024
PLE_DIM = 256
DA_HEADS = 4
DA_QK_DIM = 64
DA_V_DIM = 128
DA_WIDTH = DA_HEADS * DA_V_DIM
RET_HEADS = 4
RET_HEAD_DIM = 64
RET_WIDTH = RET_HEADS * RET_HEAD_DIM
S5_WIDTH = 256
S5_GROUP = 16
S5_GROUPS = S5_WIDTH // S5_GROUP
S5_STATE = 64
D_FF = 2816
ROPE_THETA = 10000.0
LN_EPS = 1e-5
RMS_EPS = 1e-6

COL_DA_Q = 0
COL_DA_K = COL_DA_Q + DA_HEADS * 2 * DA_QK_DIM
COL_DA_V = COL_DA_K + DA_HEADS * 2 * DA_QK_DIM
COL_RET_Q = COL_DA_V + DA_WIDTH
COL_RET_K = COL_RET_Q + RET_WIDTH
COL_RET_V = COL_RET_K + RET_WIDTH
COL_RET_G = COL_RET_V + RET_WIDTH
COL_S5_U = COL_RET_G + RET_WIDTH
IN_COLS = COL_S5_U + S5_WIDTH

LANES = 128
MXU_WIDTH = 256
S5_CHUNK = 8
S5_HALF = 128
RET_CHUNK = 256
FF_CHUNK = 256
LOG2E = 1.4426950408889634
NEG_BIG = -1e30
VMEM_LIMIT = 56 * 1024 * 1024


def _params(sem, vmem=VMEM_LIMIT):
    return pltpu.CompilerParams(dimension_semantics=sem, vmem_limit_bytes=vmem)


def _const_spec(shape):
    nd = len(shape)
    return pl.BlockSpec(shape, lambda *_: (0,) * nd, pipeline_mode=pl.Buffered(1))


def _layer_norm(x, g, b):
    mu = jnp.mean(x, axis=-1, keepdims=True)
    d = x - mu
    var = jnp.mean(d * d, axis=-1, keepdims=True)
    return d * lax.rsqrt(var + LN_EPS) * g + b


def _gelu_tanh(x):
    return 0.5 * x * (1.0 + jnp.tanh(math.sqrt(2.0 / math.pi) * (x + 0.044715 * (x * x * x))))


def _sigmoid(x):
    return 1.0 / (1.0 + jnp.exp(-x))


def _dot(a, b):
    return jnp.dot(a, b, preferred_element_type=F32)


def _dot_nt(a, b):
    return lax.dot_general(a, b, (((1,), (1,)), ((), ())), preferred_element_type=F32)


def _dot_tn(a, b):
    return lax.dot_general(a, b, (((0,), (0,)), ((), ())), preferred_element_type=F32)


def _dot_split(x, w):
    hi = x.astype(BF16)
    lo = (x - hi.astype(F32)).astype(BF16)
    return _dot(hi, w) + _dot(lo, w)


def _rope_table_kernel(pos_ref, freq_ref, sign_ref, cos_ref, sin_ref):
    ang = pos_ref[...].astype(F32) * freq_ref[...]
    cos_ref[...] = jnp.cos(ang)
    sin_ref[...] = jnp.sin(ang) * sign_ref[...]


def rope_tables(positions, tm=1024):
    n = positions.size
    half = DA_QK_DIM // 2
    inv_freq = ROPE_THETA ** (-jnp.arange(0, DA_QK_DIM, 2, dtype=F32) / DA_QK_DIM)
    freq_row = jnp.tile(inv_freq, LANES // half).reshape(1, LANES)
    lane = jnp.arange(LANES)
    sign_row = jnp.where(lane % DA_QK_DIM < half, -1.0, 1.0).astype(F32).reshape(1, LANES)
    pos = positions.reshape(n, 1)
    tm = min(tm, n)
    return pl.pallas_call(
        _rope_table_kernel,
        grid=(n // tm,),
        in_specs=[pl.BlockSpec((tm, 1), lambda i: (i, 0)),
                  _const_spec((1, LANES)), _const_spec((1, LANES))],
        out_specs=[pl.BlockSpec((tm, LANES), lambda i: (i, 0)),
                   pl.BlockSpec((tm, LANES), lambda i: (i, 0))],
        out_shape=[jax.ShapeDtypeStruct((n, LANES), F32)] * 2,
        compiler_params=_params(("parallel",)),
        name="rope_tables",
    )(pos, freq_row, sign_row)


def _rope(x, cos, sin, first_half):
    swapped = jnp.where(first_half, pltpu.roll(x, LANES - DA_QK_DIM // 2, 1),
                        pltpu.roll(x, DA_QK_DIM // 2, 1))
    return x * cos + swapped * sin


def _in_proj_kernel(x_ref, w_ref, cos_ref, sin_ref, da_ref, ret_ref, g_ref, u_ref):
    xb = x_ref[...].astype(BF16)
    cos = cos_ref[...]
    sin = sin_ref[...]
    lane = lax.broadcasted_iota(jnp.int32, cos.shape, 1)
    first_half = (lane % DA_QK_DIM) < (DA_QK_DIM // 2)
    q_scale = DA_QK_DIM ** -0.5 * LOG2E
    k_scale = RET_HEAD_DIM ** -0.5

    def proj(col):
        return _dot(xb, w_ref[:, col:col + MXU_WIDTH])

    def roped(z, scale):
        parts = [_rope(z[:, a:a + LANES], cos, sin, first_half) for a in (0, LANES)]
        out = jnp.concatenate(parts, axis=-1)
        return out if scale is None else out * scale

    for c in range(0, COL_DA_K, MXU_WIDTH):
        da_ref[:, c:c + MXU_WIDTH] = roped(proj(c), q_scale).astype(BF16)
    for c in range(COL_DA_K, COL_DA_V, MXU_WIDTH):
        da_ref[:, c:c + MXU_WIDTH] = roped(proj(c), None).astype(BF16)
    for c in range(COL_DA_V, COL_RET_Q, MXU_WIDTH):
        da_ref[:, c:c + MXU_WIDTH] = proj(c).astype(BF16)
    ret_ref[:, 0:RET_WIDTH] = roped(proj(COL_RET_Q), None).astype(BF16)
    ret_ref[:, RET_WIDTH:2 * RET_WIDTH] = roped(proj(COL_RET_K), k_scale).astype(BF16)
    ret_ref[:, 2 * RET_WIDTH:3 * RET_WIDTH] = proj(COL_RET_V).astype(BF16)
    g_ref[...] = proj(COL_RET_G)
    u = proj(COL_S5_U).astype(BF16)
    u_ref[0] = u[:, :S5_HALF]
    u_ref[1] = u[:, S5_HALF:]


def in_proj(x, w_bf16, cos, sin, tm=512):
    n = x.shape[0]
    tm = min(tm, n)
    row = lambda i: (i, 0)
    return pl.pallas_call(
        _in_proj_kernel,
        grid=(n // tm,),
        in_specs=[pl.BlockSpec((tm, D_MODEL), row), _const_spec((D_MODEL, IN_COLS)),
                  pl.BlockSpec((tm, LANES), row), pl.BlockSpec((tm, LANES), row)],
        out_specs=[pl.BlockSpec((tm, COL_RET_Q), row), pl.BlockSpec((tm, 3 * RET_WIDTH), row),
                   pl.BlockSpec((tm, RET_WIDTH), row),
                   pl.BlockSpec((2, tm, S5_HALF), lambda i: (0, i, 0))],
        out_shape=[jax.ShapeDtypeStruct((n, COL_RET_Q), BF16),
                   jax.ShapeDtypeStruct((n, 3 * RET_WIDTH), BF16),
                   jax.ShapeDtypeStruct((n, RET_WIDTH), F32),
                   jax.ShapeDtypeStruct((2, n, S5_HALF), BF16)],
        compiler_params=_params(("parallel",)),
        name="in_proj",
    )(x, w_bf16, cos, sin)


def _diff_attn_kernel(q_ref, k_ref, v_ref, lam_ref, g_ref, o_ref,
                      vext_ref, qm_ref, m_ref, acc_ref, s_ref, *, tk, out_scale):
    seq = k_ref.shape[0]
    n_kv = seq // tk

    @pl.when(pl.program_id(2) == 0)
    def _():
        vext_ref[:, :DA_V_DIM] = v_ref[...]
        vext_ref[:, DA_V_DIM:] = jnp.ones((seq, MXU_WIDTH - DA_V_DIM), BF16)

    q = q_ref[...]
    lane = lax.broadcasted_iota(jnp.int32, q.shape, 1)
    zero = jnp.zeros_like(q)
    qm_ref[0] = jnp.where(lane < DA_QK_DIM, q, zero)
    qm_ref[1] = jnp.where(lane >= DA_QK_DIM, q, zero)
    m_ref[...] = jnp.full(m_ref.shape, NEG_BIG, F32)
    acc_ref[...] = jnp.zeros(acc_ref.shape, F32)

    def scores(i, slot):
        start = pl.multiple_of(i * tk, tk)
        k_t = k_ref[pl.ds(start, tk), :]
        for mi in range(2):
            s_ref[slot, mi] = _dot_nt(qm_ref[mi], k_t)

    def consume(i, slot):
        start = pl.multiple_of(i * tk, tk)
        v_t = vext_ref[pl.ds(start, tk), :]
        for mi in range(2):
            s = s_ref[slot, mi]
            m_old = m_ref[mi]
            m_new = jnp.maximum(m_old, jnp.max(s, axis=-1, keepdims=True))
            alpha = jnp.exp2(m_old - m_new)
            p = jnp.exp2(s - m_new).astype(BF16)
            acc_ref[mi] = acc_ref[mi] * alpha + _dot(p, v_t)
            m_ref[mi] = m_new

    scores(0, 0)

    def kv_pair(j, carry):
        i = 2 * j
        scores(i + 1, 1)
        consume(i, 0)
        scores(jnp.minimum(i + 2, n_kv - 1), 0)
        consume(i + 1, 1)
        return carry

    lax.fori_loop(0, n_kv // 2, kv_pair, 0)

    a0 = acc_ref[0]
    a1 = acc_ref[1]
    o = a0[:, :DA_V_DIM] / a0[:, DA_V_DIM:] - lam_ref[...] * (a1[:, :DA_V_DIM] / a1[:, DA_V_DIM:])
    ms = jnp.mean(o * o, axis=-1, keepdims=True)
    o_ref[...] = (o * lax.rsqrt(ms + RMS_EPS) * g_ref[...] * out_scale).astype(o_ref.dtype)


def diff_attention(da, lam_row, subln_row, out_scale, bsz, seq, tq=512, tk=512):
    tq = min(tq, seq)
    tk = min(tk, seq // 2)
    assert seq % (2 * tk) == 0 and seq % tq == 0
    kern = functools.partial(_diff_attn_kernel, tk=tk, out_scale=out_scale)
    return pl.pallas_call(
        kern,
        grid=(bsz, DA_HEADS, seq // tq),
        in_specs=[pl.BlockSpec((None, tq, LANES), lambda b, h, i: (b, i, h)),
                  pl.BlockSpec((None, seq, LANES), lambda b, h, i: (b, 0, DA_HEADS + h)),
                  pl.BlockSpec((None, seq, LANES), lambda b, h, i: (b, 0, 2 * DA_HEADS + h)),
                  _const_spec((1, DA_V_DIM)), _const_spec((1, DA_V_DIM))],
        out_specs=pl.BlockSpec((None, tq, DA_V_DIM), lambda b, h, i: (b, i, h)),
        out_shape=jax.ShapeDtypeStruct((bsz, seq, DA_WIDTH), BF16),
        scratch_shapes=[pltpu.VMEM((seq, MXU_WIDTH), BF16),
                        pltpu.VMEM((2, tq, LANES), BF16),
                        pltpu.VMEM((2, tq, 1), F32),
                        pltpu.VMEM((2, tq, MXU_WIDTH), F32),
                        pltpu.VMEM((2, 2, tq, tk), F32)],
        compiler_params=_params(("parallel", "parallel", "arbitrary")),
        name="diff_attention",
    )(da, da, da, lam_row, subln_row)


def _ret_state_kernel(kf_ref, vf_ref, kb_ref, vb_ref, dkf_ref, dkb_ref, gc_ref,
                      sf_out, sb_out, sf_ref, sb_ref):
    @pl.when(pl.program_id(1) == 0)
    def _():
        sf_ref[...] = jnp.zeros(sf_ref.shape, F32)
        sb_ref[...] = jnp.zeros(sb_ref.shape, F32)

    sf_out[...] = sf_ref[...]
    sb_out[...] = sb_ref[...]

    def update(s_ref, k_ref, v_ref, dk_ref):
        kd = (k_ref[...].astype(F32) * dk_ref[...]).astype(BF16)
        v = v_ref[...]
        for pr in range(RET_WIDTH // LANES):
            sl = slice(pr * LANES, (pr + 1) * LANES)
            kv = _dot_tn(kd[:, sl], v[:, sl])
            keep = gc_ref[sl, :]
            s_ref[sl, :] = keep * s_ref[sl, :] + jnp.where(keep > 0.0, kv, 0.0)

    update(sf_ref, kf_ref, vf_ref, dkf_ref)
    update(sb_ref, kb_ref, vb_ref, dkb_ref)


def _ret_out_kernel(q_ref, k_ref, v_ref, g_ref, sf_ref, sb_ref, dec_ref, dqf_ref, dqb_ref,
                    avg_ref, gng_ref, gnb_ref, o_ref):
    q = q_ref[...]
    k = k_ref[...]
    v = v_ref[...]
    qf = q.astype(F32)
    lane = lax.broadcasted_iota(jnp.int32, (q.shape[0], LANES), 1)
    zero = jnp.zeros((q.shape[0], LANES), BF16)
    parts = []
    for pr in range(RET_WIDTH // LANES):
        sl = slice(pr * LANES, (pr + 1) * LANES)
        qp, kp, vp = q[:, sl], k[:, sl], v[:, sl]
        acc = _dot((qf[:, sl] * dqf_ref[:, sl]).astype(BF16), sf_ref[sl, :].astype(BF16))
        acc += _dot((qf[:, sl] * dqb_ref[:, sl]).astype(BF16), sb_ref[sl, :].astype(BF16))
        for hh in range(LANES // RET_HEAD_DIM):
            mine = (lane >= hh * RET_HEAD_DIM) & (lane < (hh + 1) * RET_HEAD_DIM)
            s = _dot_nt(jnp.where(mine, qp, zero), kp) * dec_ref[pr * 2 + hh]
            acc += _dot(s.astype(BF16), jnp.where(mine, vp, zero))
        parts.append(acc)
    o = jnp.concatenate(parts, axis=-1)
    avg = avg_ref[...]
    mu = _dot_split(o, avg)
    d = o - mu
    var = _dot_split(d * d, avg)
    y = d * lax.rsqrt(var + LN_EPS) * gng_ref[...] + gnb_ref[...]
    g = g_ref[...]
    o_ref[...] = (g * _sigmoid(g) * y).astype(o_ref.dtype)


def retention(ret, g, gn_g, gn_b, bsz, seq):
    c = min(RET_CHUNK, seq)
    nc = seq // c
    heads = jnp.arange(RET_HEADS, dtype=F32)
    log_gamma = jnp.log(1.0 - 2.0 ** (-5.0 - heads))
    lg_cols = jnp.repeat(log_gamma, RET_HEAD_DIM)[None, :]
    idx = jnp.arange(c, dtype=F32)[:, None]
    dk_f = jnp.exp(lg_cols * (c - 1 - idx))
    dk_b = jnp.exp(lg_cols * idx)
    dq_f = jnp.exp(lg_cols * (idx + 1))
    dq_b = jnp.exp(lg_cols * (c - idx))
    dist = jnp.abs(idx - idx.T)
    decay = jnp.exp(log_gamma[:, None, None] * dist[None])
    row_head = jnp.arange(RET_WIDTH)[:, None] // RET_HEAD_DIM
    col_head = (jnp.arange(LANES)[None, :] // RET_HEAD_DIM) + 2 * (jnp.arange(RET_WIDTH)[:, None] // LANES)
    gc = jnp.where(row_head == col_head, jnp.exp(lg_cols.T * c), 0.0).astype(F32)
    seg = jnp.arange(RET_WIDTH) // RET_HEAD_DIM
    avg = jnp.where(seg[:, None] == seg[None, :], 1.0 / RET_HEAD_DIM, 0.0).astype(BF16)
    gng = jnp.tile(gn_g, RET_HEADS)[None, :]
    gnb = jnp.tile(gn_b, RET_HEADS)[None, :]

    blk = lambda col: pl.BlockSpec((None, c, RET_WIDTH), lambda b, j: (b, j, col))
    blk_rev = lambda col: pl.BlockSpec((None, c, RET_WIDTH), lambda b, j: (b, nc - 1 - j, col))
    st_shape = jax.ShapeDtypeStruct((bsz, nc, RET_WIDTH, LANES), F32)
    sf, sb = pl.pallas_call(
        _ret_state_kernel,
        grid=(bsz, nc),
        in_specs=[blk(1), blk(2), blk_rev(1), blk_rev(2),
                  _const_spec((c, RET_WIDTH)), _const_spec((c, RET_WIDTH)),
                  _const_spec((RET_WIDTH, LANES))],
        out_specs=[pl.BlockSpec((None, None, RET_WIDTH, LANES), lambda b, j: (b, j, 0, 0)),
                   pl.BlockSpec((None, None, RET_WIDTH, LANES), lambda b, j: (b, nc - 1 - j, 0, 0))],
        out_shape=[st_shape, st_shape],
        scratch_shapes=[pltpu.VMEM((RET_WIDTH, LANES), F32), pltpu.VMEM((RET_WIDTH, LANES), F32)],
        compiler_params=_params(("parallel", "arbitrary")),
        name="retention_state",
    )(ret, ret, ret, ret, dk_f, dk_b, gc)

    st_spec = pl.BlockSpec((None, None, RET_WIDTH, LANES), lambda b, j: (b, j, 0, 0))
    return pl.pallas_call(
        _ret_out_kernel,
        grid=(bsz, nc),
        in_specs=[blk(0), blk(1), blk(2), blk(0), st_spec, st_spec,
                  _const_spec((RET_HEADS, c, c)), _const_spec((c, RET_WIDTH)),
                  _const_spec((c, RET_WIDTH)), _const_spec((RET_WIDTH, RET_WIDTH)),
                  _const_spec((1, RET_WIDTH)), _const_spec((1, RET_WIDTH))],
        out_specs=blk(0),
        out_shape=jax.ShapeDtypeStruct((bsz, seq, RET_WIDTH), BF16),
        compiler_params=_params(("parallel", "parallel")),
        name="retention_out",
    )(ret, ret, ret, g, sf, sb, decay, dq_f, dq_b, avg, gng, gnb)


def _cmul(ar, ai, br, bi):
    return ar * br - ai * bi, ar * bi + ai * br


def s5_matrices(A_re, A_im, log_dt, B_re, B_im, C_re, C_im, D):
    T, G, P, Cn = S5_CHUNK, S5_GROUPS, S5_STATE, S5_GROUP
    gh = S5_HALF // Cn
    step = jnp.exp(log_dt.astype(F32))[..., None]
    a_re = A_re.astype(F32)
    a_im = A_im.astype(F32)
    e = jnp.exp(step * a_re)
    abar_re = e * jnp.cos(step * a_im)
    abar_im = e * jnp.sin(step * a_im)
    den = a_re * a_re + a_im * a_im
    nr = abar_re - 1.0
    ni = abar_im
    coef_re = (nr * a_re + ni * a_im) / den
    coef_im = (ni * a_re - nr * a_im) / den
    b_re = B_re.astype(F32)
    b_im = B_im.astype(F32)
    bb_re = coef_re[..., None] * b_re - coef_im[..., None] * b_im
    bb_im = coef_re[..., None] * b_im + coef_im[..., None] * b_re
    pows = [(jnp.ones_like(abar_re), jnp.zeros_like(abar_re))]
    for _ in range(T):
        pows.append(_cmul(pows[-1][0], pows[-1][1], abar_re, abar_im))
    pw_re = jnp.stack([p[0] for p in pows])
    pw_im = jnp.stack([p[1] for p in pows])
    c_re = C_re.astype(F32)
    c_im = C_im.astype(F32)
    eye = jnp.eye(gh, dtype=F32)
    t_idx = jnp.arange(T)

    def state_in(direction, order):
        pr = pw_re[order, direction]
        pi = pw_im[order, direction]
        vr, vi = _cmul(pr[:, :, None, :], pi[:, :, None, :],
                       jnp.swapaxes(bb_re[direction], 1, 2)[None],
                       jnp.swapaxes(bb_im[direction], 1, 2)[None])
        return vr, vi

    def read_out(direction, order):
        pr = pw_re[order, direction][:, :, None, :]
        pi = pw_im[order, direction][:, :, None, :]
        er, ei = _cmul(c_re[direction][None], c_im[direction][None], pr, pi)
        return er, ei

    def lag_kernel(direction):
        pr = pw_re[:T, direction][:, :, :, None]
        pi = pw_im[:T, direction][:, :, :, None]
        wr, wi = _cmul(pr, pi, bb_re[direction][None], bb_im[direction][None])
        return (jnp.einsum('gop,dgpi->dgoi', c_re[direction], wr)
                - jnp.einsum('gop,dgpi->dgoi', c_im[direction], wi))

    vfr, vfi = state_in(0, T - 1 - t_idx)
    vbr, vbi = state_in(1, t_idx)
    efr, efi = read_out(0, t_idx + 1)
    ebr, ebi = read_out(1, T - t_idx)
    kf = lag_kernel(0)
    kb = lag_kernel(1)
    lag = t_idx[None, :] - t_idx[:, None]
    toe = (jnp.where((lag >= 0)[:, :, None, None, None], kf[jnp.clip(lag, 0, T - 1)], 0.0)
           + jnp.where((lag <= 0)[:, :, None, None, None], kb[jnp.clip(-lag, 0, T - 1)], 0.0))
    d_diag = D.astype(F32).reshape(G, Cn)[:, :, None] * jnp.eye(Cn, dtype=F32)[None]
    toe = toe + jnp.where((lag == 0)[:, :, None, None, None], d_diag[None, None], 0.0)

    mb, tp, mc, a8 = [], [], [], []
    for h in range(2):
        gs = slice(h * gh, (h + 1) * gh)
        sin_blk = lambda v: jnp.einsum('sgcp,gk->sgckp', v[:, gs], eye).reshape(T * gh * Cn, gh * P)
        mb.append(jnp.concatenate([sin_blk(vfr), sin_blk(vfi), sin_blk(vbr), sin_blk(vbi)], axis=1))
        tp.append(jnp.einsum('stgoi,gk->sgitko', toe[:, :, gs], eye).reshape(T * gh * Cn, T * gh * Cn))
        out_blk = lambda v: jnp.einsum('tgop,gk->gptko', v[:, gs], eye).reshape(gh * P, T * gh * Cn)
        mc.append(jnp.concatenate([out_blk(efr), -out_blk(efi), out_blk(ebr), -out_blk(ebi)], axis=0))
        rows = [pw_re[T, 0, gs].reshape(-1), pw_im[T, 0, gs].reshape(-1),
                pw_re[T, 1, gs].reshape(-1), pw_im[T, 1, gs].reshape(-1)]
        a8.append(jnp.stack(rows + rows))
    return (jnp.stack(mb).astype(BF16), jnp.stack(tp).astype(BF16),
            jnp.stack(mc).astype(BF16), jnp.stack(a8))


def _s5_kernel(u_ref, mb_ref, tp_ref, mc_ref, a8_ref, y_ref, w_ref, *, sub):
    rows = u_ref.shape[0]
    ns = a8_ref.shape[1]
    for r in range(0, rows, sub):
        w_ref[r:r + sub, :] = _dot(u_ref[r:r + sub, :], mb_ref[...])

    afr, afi = a8_ref[0:1, :], a8_ref[1:2, :]
    abr, abi = a8_ref[2:3, :], a8_ref[3:4, :]

    def scan_step(j, carry):
        xfr, xfi, xbr, xbi = carry
        jb = rows - 1 - j
        wf_r = w_ref[pl.ds(j, 1), 0:ns]
        wf_i = w_ref[pl.ds(j, 1), ns:2 * ns]
        wb_r = w_ref[pl.ds(jb, 1), 2 * ns:3 * ns]
        wb_i = w_ref[pl.ds(jb, 1), 3 * ns:4 * ns]
        w_ref[pl.ds(j, 1), 0:ns] = xfr
        w_ref[pl.ds(j, 1), ns:2 * ns] = xfi
        w_ref[pl.ds(jb, 1), 2 * ns:3 * ns] = xbr
        w_ref[pl.ds(jb, 1), 3 * ns:4 * ns] = xbi
        nfr = afr * xfr - afi * xfi + wf_r
        nfi = afr * xfi + afi * xfr + wf_i
        nbr = abr * xbr - abi * xbi + wb_r
        nbi = abr * xbi + abi * xbr + wb_i
        return nfr, nfi, nbr, nbi

    z = jnp.zeros((1, ns), F32)
    lax.fori_loop(0, rows, scan_step, (z, z, z, z))

    for r in range(0, rows, sub):
        y_ref[r:r + sub, :] = (_dot(u_ref[r:r + sub, :], tp_ref[...])
                               + _dot(w_ref[r:r + sub, :].astype(BF16), mc_ref[...]))


def s5_mixer(u, mats, bsz, seq):
    mb, tp, mc, a8 = mats
    n = bsz * seq
    rows = seq // S5_CHUNK
    width = S5_CHUNK * S5_HALF
    ns = (S5_HALF // S5_GROUP) * S5_STATE
    u8 = u.reshape(2, n // S5_CHUNK, width)
    kern = functools.partial(_s5_kernel, sub=min(256, rows))
    wspec = lambda a: pl.BlockSpec((None,) + a.shape[1:], lambda h, b: (h, 0, 0),
                                   pipeline_mode=pl.Buffered(1))
    y8 = pl.pallas_call(
        kern,
        grid=(2, bsz),
        in_specs=[pl.BlockSpec((None, rows, width), lambda h, b: (h, b, 0)),
                  wspec(mb), wspec(tp), wspec(mc), wspec(a8)],
        out_specs=pl.BlockSpec((None, rows, width), lambda h, b: (h, b, 0)),
        out_shape=jax.ShapeDtypeStruct((2, n // S5_CHUNK, width), F32),
        scratch_shapes=[pltpu.VMEM((rows, 4 * ns), F32)],
        compiler_params=_params(("arbitrary", "arbitrary")),
        name="s5_mixer",
    )(u8, mb, tp, mc, a8)
    return y8.reshape(2, n, S5_HALF)


def _out_proj_kernel(x_ref, da_ref, ret_ref, y5_ref, gluw_ref, glub_ref, wout_ref,
                     g_ref, b_ref, o_ref, *, alpha):
    y = jnp.concatenate([y5_ref[0], y5_ref[1]], axis=-1)
    ya = _gelu_tanh(y)
    gate = _sigmoid(_dot(ya.astype(BF16), gluw_ref[...]) + glub_ref[...])
    ys5 = (ya * gate).astype(BF16)
    c1 = DA_WIDTH
    c2 = DA_WIDTH + RET_WIDTH
    mix = (_dot(da_ref[...], wout_ref[0:c1, :]) + _dot(ret_ref[...], wout_ref[c1:c2, :])
           + _dot(ys5, wout_ref[c2:, :]))
    o_ref[...] = _layer_norm(alpha * x_ref[...] + mix, g_ref[...], b_ref[...])


def out_proj(x, y_da, y_ret, y5, glu_w, glu_b, w_out, ln_g, ln_b, alpha, tm=512):
    n = x.shape[0]
    tm = min(tm, n)
    row = lambda i: (i, 0)
    return pl.pallas_call(
        functools.partial(_out_proj_kernel, alpha=alpha),
        grid=(n // tm,),
        in_specs=[pl.BlockSpec((tm, D_MODEL), row), pl.BlockSpec((tm, DA_WIDTH), row),
                  pl.BlockSpec((tm, RET_WIDTH), row),
                  pl.BlockSpec((2, tm, S5_HALF), lambda i: (0, i, 0)),
                  _const_spec((S5_WIDTH, S5_WIDTH)), _const_spec((1, S5_WIDTH)),
                  _const_spec((D_MODEL, D_MODEL)),
                  _const_spec((1, D_MODEL)), _const_spec((1, D_MODEL))],
        out_specs=pl.BlockSpec((tm, D_MODEL), row),
        out_shape=jax.ShapeDtypeStruct((n, D_MODEL), F32),
        compiler_params=_params(("parallel",)),
        name="out_proj",
    )(x, y_da, y_ret, y5, glu_w, glu_b, w_out, ln_g, ln_b)


def _ffn_kernel(x_ref, xp_ref, xn_ref, p_ref, wup_ref, cw_ref, cb_ref, wdn_ref,
                plew_ref, gatew_ref, g_ref, b_ref, o_ref, acc_ref, *, alpha, tiles_per_seq):
    tm = x_ref.shape[0]
    i = pl.program_id(0)
    has_prev = ((i % tiles_per_seq) != 0).astype(F32)
    has_next = ((i % tiles_per_seq) != tiles_per_seq - 1).astype(F32)
    x = x_ref[...]
    xb = x.astype(BF16)
    xpb = xp_ref[...].astype(BF16)
    xnb = xn_ref[...].astype(BF16)
    row = lax.broadcasted_iota(jnp.int32, (tm, FF_CHUNK), 0)
    halo = xp_ref.shape[0]

    for c in range(0, D_FF, FF_CHUNK):
        wg = wup_ref[:, c:c + FF_CHUNK]
        gate = _dot(xb, wg)
        val = _dot(xb, wup_ref[:, D_FF + c:D_FF + c + FF_CHUNK])
        before = _dot(xpb, wg)[halo - 1:halo, :] * has_prev
        after = _dot(xnb, wg)[0:1, :] * has_next
        left = jnp.where(row == 0, before, pltpu.roll(gate, 1, 0))
        right = jnp.where(row == tm - 1, after, pltpu.roll(gate, tm - 1, 0))
        conv = (cw_ref[0:1, c:c + FF_CHUNK] * left + cw_ref[1:2, c:c + FF_CHUNK] * gate
                + cw_ref[2:3, c:c + FF_CHUNK] * right + cb_ref[:, c:c + FF_CHUNK])
        act = (_gelu_tanh(conv) * val).astype(BF16)
        contrib = _dot(act, wdn_ref[c:c + FF_CHUNK, :])
        if c == 0:
            acc_ref[...] = contrib
        else:
            acc_ref[...] += contrib

    ple = _dot(p_ref[...].astype(BF16), plew_ref[...]) * _sigmoid(_dot(xb, gatew_ref[...]))
    o_ref[...] = _layer_norm(alpha * x + acc_ref[...] + ple, g_ref[...], b_ref[...])


def conv_ffn_ple(x, p, w_up, conv_w, conv_b, w_down, ple_w, gate_w, ln_g, ln_b, alpha, seq, tm=512):
    n = x.shape[0]
    tm = min(tm, seq)
    halo = 8
    tiles_per_seq = seq // tm
    per = tm // halo
    last = n // halo - 1
    row = lambda i: (i, 0)
    kern = functools.partial(_ffn_kernel, alpha=alpha, tiles_per_seq=tiles_per_seq)
    return pl.pallas_call(
        kern,
        grid=(n // tm,),
        in_specs=[pl.BlockSpec((tm, D_MODEL), row),
                  pl.BlockSpec((halo, D_MODEL), lambda i: (jnp.maximum(i * per - 1, 0), 0)),
                  pl.BlockSpec((halo, D_MODEL), lambda i: (jnp.minimum((i + 1) * per, last), 0)),
                  pl.BlockSpec((tm, PLE_DIM), row),
                  _const_spec((D_MODEL, 2 * D_FF)), _const_spec((3, D_FF)), _const_spec((1, D_FF)),
                  _const_spec((D_FF, D_MODEL)), _const_spec((PLE_DIM, D_MODEL)),
                  _const_spec((D_MODEL, D_MODEL)),
                  _const_spec((1, D_MODEL)), _const_spec((1, D_MODEL))],
        out_specs=pl.BlockSpec((tm, D_MODEL), row),
        out_shape=jax.ShapeDtypeStruct((n, D_MODEL), F32),
        scratch_shapes=[pltpu.VMEM((tm, D_MODEL), F32)],
        compiler_params=_params(("parallel",)),
        name="conv_ffn_ple",
    )(x, x, x, p, w_up, conv_w, conv_b, w_down, ple_w, gate_w, ln_g, ln_b)


def kernel(x, p, positions, w_in, da_lambda_q1, da_lambda_k1, da_lambda_q2, da_lambda_k2,
           da_subln_g, ret_gn_g, ret_gn_b, s5_A_re, s5_A_im, s5_log_dt, s5_B_re, s5_B_im,
           s5_C_re, s5_C_im, s5_D, s5_glu_w, s5_glu_b, w_out, ln1_g, ln1_b,
           ffn_w_up, ffn_conv_w, ffn_conv_b, ffn_w_down, ple_w, ple_gate_w, ln2_g, ln2_b):
    bsz, seq, _ = x.shape
    depth = w_in.shape[0]
    n = bsz * seq
    alpha = (2 * depth) ** 0.25
    cos, sin = rope_tables(positions)
    xf = x.reshape(n, D_MODEL)
    row = lambda v: v.reshape(1, -1).astype(F32)
    for i in range(depth):
        lambda_init = 0.8 - 0.6 * math.exp(-0.3 * i)
        lam = (jnp.exp(jnp.sum(da_lambda_q1[i].astype(F32) * da_lambda_k1[i].astype(F32)))
               - jnp.exp(jnp.sum(da_lambda_q2[i].astype(F32) * da_lambda_k2[i].astype(F32)))
               + lambda_init)
        lam_row = jnp.full((1, DA_V_DIM), lam, F32)
        da, ret, g, u = in_proj(xf, w_in[i].astype(BF16), cos, sin)
        y_da = diff_attention(da.reshape(bsz, seq, -1), lam_row, row(da_subln_g[i]),
                              1.0 - lambda_init, bsz, seq)
        y_ret = retention(ret.reshape(bsz, seq, -1), g.reshape(bsz, seq, -1),
                          ret_gn_g[i].astype(F32), ret_gn_b[i].astype(F32), bsz, seq)
        mats = s5_matrices(s5_A_re[i], s5_A_im[i], s5_log_dt[i], s5_B_re[i], s5_B_im[i],
                           s5_C_re[i], s5_C_im[i], s5_D[i])
        y5 = s5_mixer(u, mats, bsz, seq)
        x1 = out_proj(xf, y_da.reshape(n, -1), y_ret.reshape(n, -1), y5,
                      s5_glu_w[i].astype(BF16), row(s5_glu_b[i]), w_out[i].astype(BF16),
                      row(ln1_g[i]), row(ln1_b[i]), alpha)
        xf = conv_ffn_ple(x1, p[i].reshape(n, PLE_DIM), ffn_w_up[i].astype(BF16),
                          ffn_conv_w[i].astype(F32), row(ffn_conv_b[i]),
                          ffn_w_down[i].astype(BF16), ple_w[i].astype(BF16),
                          ple_gate_w[i].astype(BF16), row(ln2_g[i]), row(ln2_b[i]), alpha, seq)
    return xf.reshape(bsz, seq, D_MODEL)
```

```python
import functools
import math

import jax
import jax.numpy as jnp
from jax import lax
from jax.experimental import pallas as pl
from jax.experimental.pallas import tpu as pltpu

F32 = jnp.float32
BF16 = jnp.bfloat16

D_MODEL = 1024
PLE_DIM = 256
DA_HEADS = 4
DA_QK_DIM = 64
DA_V_DIM = 128
DA_WIDTH = DA_HEADS * DA_V_DIM
RET_HEADS = 4
RET_HEAD_DIM = 64
RET_WIDTH = RET_HEADS * RET_HEAD_DIM
S5_WIDTH = 256
S5_GROUP = 16
S5_GROUPS = S5_WIDTH // S5_GROUP
S5_STATE = 64
D_FF = 2816
ROPE_THETA = 10000.0
LN_EPS = 1e-5
RMS_EPS = 1e-6

COL_DA_Q = 0
COL_DA_K = COL_DA_Q + DA_HEADS * 2 * DA_QK_DIM
COL_DA_V = COL_DA_K + DA_HEADS * 2 * DA_QK_DIM
COL_RET_Q = COL_DA_V + DA_WIDTH
COL_RET_K = COL_RET_Q + RET_WIDTH
COL_RET_V = COL_RET_K + RET_WIDTH
COL_RET_G = COL_RET_V + RET_WIDTH
COL_S5_U = COL_RET_G + RET_WIDTH
IN_COLS = COL_S5_U + S5_WIDTH

LANES = 128
MXU_WIDTH = 256
S5_CHUNK = 8
S5_HALF = 128
RET_CHUNK = 256
FF_CHUNK = 256
LOG2E = 1.4426950408889634
NEG_BIG = -1e30
VMEM_LIMIT = 56 * 1024 * 1024


def _params(sem, vmem=VMEM_LIMIT):
    return pltpu.CompilerParams(dimension_semantics=sem, vmem_limit_bytes=vmem)


def _const_spec(shape):
    nd = len(shape)
    return pl.BlockSpec(shape, lambda *_: (0,) * nd, pipeline_mode=pl.Buffered(1))


def _layer_norm(x, g, b):
    mu = jnp.mean(x, axis=-1, keepdims=True)
    d = x - mu
    var = jnp.mean(d * d, axis=-1, keepdims=True)
    return d * lax.rsqrt(var + LN_EPS) * g + b


def _gelu_tanh(x):
    return 0.5 * x * (1.0 + jnp.tanh(math.sqrt(2.0 / math.pi) * (x + 0.044715 * (x * x * x))))


def _sigmoid(x):
    return 1.0 / (1.0 + jnp.exp(-x))


def _dot(a, b):
    return jnp.dot(a, b, preferred_element_type=F32)


def _dot_nt(a, b):
    return lax.dot_general(a, b, (((1,), (1,)), ((), ())), preferred_element_type=F32)


def _dot_tn(a, b):
    return lax.dot_general(a, b, (((0,), (0,)), ((), ())), preferred_element_type=F32)


def _dot_split(x, w):
    hi = x.astype(BF16)
    lo = (x - hi.astype(F32)).astype(BF16)
    return _dot(hi, w) + _dot(lo, w)


def _rope_table_kernel(pos_ref, freq_ref, sign_ref, cos_ref, sin_ref):
    ang = pos_ref[...].astype(F32) * freq_ref[...]
    cos_ref[...] = jnp.cos(ang)
    sin_ref[...] = jnp.sin(ang) * sign_ref[...]


def rope_tables(positions, tm=1024):
    n = positions.size
    half = DA_QK_DIM // 2
    inv_freq = ROPE_THETA ** (-jnp.arange(0, DA_QK_DIM, 2, dtype=F32) / DA_QK_DIM)
    freq_row = jnp.tile(inv_freq, LANES // half).reshape(1, LANES)
    lane = jnp.arange(LANES)
    sign_row = jnp.where(lane % DA_QK_DIM < half, -1.0, 1.0).astype(F32).reshape(1, LANES)
    pos = positions.reshape(n, 1)
    tm = min(tm, n)
    return pl.pallas_call(
        _rope_table_kernel,
        grid=(n // tm,),
        in_specs=[pl.BlockSpec((tm, 1), lambda i: (i, 0)),
                  _const_spec((1, LANES)), _const_spec((1, LANES))],
        out_specs=[pl.BlockSpec((tm, LANES), lambda i: (i, 0)),
                   pl.BlockSpec((tm, LANES), lambda i: (i, 0))],
        out_shape=[jax.ShapeDtypeStruct((n, LANES), F32)] * 2,
        compiler_params=_params(("parallel",)),
        name="rope_tables",
    )(pos, freq_row, sign_row)


def _rope(x, cos, sin, first_half):
    swapped = jnp.where(first_half, pltpu.roll(x, LANES - DA_QK_DIM // 2, 1),
                        pltpu.roll(x, DA_QK_DIM // 2, 1))
    return x * cos + swapped * sin


def _in_proj_kernel(x_ref, w_ref, cos_ref, sin_ref, da_ref, ret_ref, g_ref, u_ref):
    xb = x_ref[...].astype(BF16)
    cos = cos_ref[...]
    sin = sin_ref[...]
    lane = lax.broadcasted_iota(jnp.int32, cos.shape, 1)
    first_half = (lane % DA_QK_DIM) < (DA_QK_DIM // 2)
    q_scale = DA_QK_DIM ** -0.5 * LOG2E
    k_scale = RET_HEAD_DIM ** -0.5

    def proj(col):
        return _dot(xb, w_ref[:, col:col + MXU_WIDTH])

    def roped(z, scale):
        parts = [_rope(z[:, a:a + LANES], cos, sin, first_half) for a in (0, LANES)]
        out = jnp.concatenate(parts, axis=-1)
        return out if scale is None else out * scale

    for c in range(0, COL_DA_K, MXU_WIDTH):
        da_ref[:, c:c + MXU_WIDTH] = roped(proj(c), q_scale).astype(BF16)
    for c in range(COL_DA_K, COL_DA_V, MXU_WIDTH):
        da_ref[:, c:c + MXU_WIDTH] = roped(proj(c), None).astype(BF16)
    for c in range(COL_DA_V, COL_RET_Q, MXU_WIDTH):
        da_ref[:, c:c + MXU_WIDTH] = proj(c).astype(BF16)
    ret_ref[:, 0:RET_WIDTH] = roped(proj(COL_RET_Q), None).astype(BF16)
    ret_ref[:, RET_WIDTH:2 * RET_WIDTH] = roped(proj(COL_RET_K), k_scale).astype(BF16)
    ret_ref[:, 2 * RET_WIDTH:3 * RET_WIDTH] = proj(COL_RET_V).astype(BF16)
    g_ref[...] = proj(COL_RET_G)
    u = proj(COL_S5_U)
    u_ref[0] = u[:, :S5_HALF]
    u_ref[1] = u[:, S5_HALF:]


def in_proj(x, w_bf16, cos, sin, tm=512):
    n = x.shape[0]
    tm = min(tm, n)
    row = lambda i: (i, 0)
    return pl.pallas_call(
        _in_proj_kernel,
        grid=(n // tm,),
        in_specs=[pl.BlockSpec((tm, D_MODEL), row), _const_spec((D_MODEL, IN_COLS)),
                  pl.BlockSpec((tm, LANES), row), pl.BlockSpec((tm, LANES), row)],
        out_specs=[pl.BlockSpec((tm, COL_RET_Q), row), pl.BlockSpec((tm, 3 * RET_WIDTH), row),
                   pl.BlockSpec((tm, RET_WIDTH), row),
                   pl.BlockSpec((2, tm, S5_HALF), lambda i: (0, i, 0))],
        out_shape=[jax.ShapeDtypeStruct((n, COL_RET_Q), BF16),
                   jax.ShapeDtypeStruct((n, 3 * RET_WIDTH), BF16),
                   jax.ShapeDtypeStruct((n, RET_WIDTH), F32),
                   jax.ShapeDtypeStruct((2, n, S5_HALF), F32)],
        compiler_params=_params(("parallel",)),
        name="in_proj",
    )(x, w_bf16, cos, sin)


def _diff_attn_kernel(q_ref, k_ref, v_ref, lam_ref, g_ref, o_ref,
                      vext_ref, m_ref, acc_ref, s_ref, *, tq, tk, out_scale):
    seq = k_ref.shape[0]
    n_kv = seq // tk
    n_q = seq // tq
    vext_ref[:, :DA_V_DIM] = v_ref[...]
    vext_ref[:, DA_V_DIM:] = jnp.ones((seq, MXU_WIDTH - DA_V_DIM), BF16)
    lane = lax.broadcasted_iota(jnp.int32, (tq, LANES), 1)
    zero = jnp.zeros((tq, LANES), BF16)

    def scores(qi, ki, slot):
        q = q_ref[pl.ds(pl.multiple_of(qi * tq, tq), tq), :]
        k_t = k_ref[pl.ds(pl.multiple_of(ki * tk, tk), tk), :]
        s_ref[slot, 0] = _dot_nt(jnp.where(lane < DA_QK_DIM, q, zero), k_t)
        s_ref[slot, 1] = _dot_nt(jnp.where(lane >= DA_QK_DIM, q, zero), k_t)

    def consume(ki, slot):
        v_t = vext_ref[pl.ds(pl.multiple_of(ki * tk, tk), tk), :]
        for mi in range(2):
            s = s_ref[slot, mi]
            m_old = m_ref[mi]
            m_new = jnp.maximum(m_old, jnp.max(s, axis=-1, keepdims=True))
            alpha = jnp.exp2(m_old - m_new)
            p = jnp.exp2(s - m_new).astype(BF16)
            acc_ref[mi] = acc_ref[mi] * alpha + _dot(p, v_t)
            m_ref[mi] = m_new

    scores(0, 0, 0)

    def q_tile(qi, carry):
        m_ref[...] = jnp.full(m_ref.shape, NEG_BIG, F32)
        acc_ref[...] = jnp.zeros(acc_ref.shape, F32)

        def kv_pair(j, c):
            for u in range(2):
                ki = 2 * j + u
                wrap = ki + 1 >= n_kv
                scores(jnp.where(wrap, jnp.minimum(qi + 1, n_q - 1), qi),
                       jnp.where(wrap, 0, ki + 1), 1 - u)
                consume(ki, u)
            return c

        lax.fori_loop(0, n_kv // 2, kv_pair, 0)
        a0 = acc_ref[0]
        a1 = acc_ref[1]
        o = (a0[:, :DA_V_DIM] / a0[:, DA_V_DIM:]
             - lam_ref[...] * (a1[:, :DA_V_DIM] / a1[:, DA_V_DIM:]))
        ms = jnp.mean(o * o, axis=-1, keepdims=True)
        rows = pl.ds(pl.multiple_of(qi * tq, tq), tq)
        o_ref[rows, :] = (o * lax.rsqrt(ms + RMS_EPS) * g_ref[...] * out_scale).astype(o_ref.dtype)
        return carry

    lax.fori_loop(0, n_q, q_tile, 0)


def diff_attention(da, lam_row, subln_row, out_scale, bsz, seq, tq=512, tk=512):
    tq = min(tq, seq)
    tk = min(tk, seq // 2)
    assert seq % (2 * tk) == 0 and seq % tq == 0
    kern = functools.partial(_diff_attn_kernel, tq=tq, tk=tk, out_scale=out_scale)
    head_block = lambda part: pl.BlockSpec((None, seq, LANES), lambda b, h: (b, 0, part * DA_HEADS + h))
    return pl.pallas_call(
        kern,
        grid=(bsz, DA_HEADS),
        in_specs=[head_block(0), head_block(1), head_block(2),
                  _const_spec((1, DA_V_DIM)), _const_spec((1, DA_V_DIM))],
        out_specs=pl.BlockSpec((None, seq, DA_V_DIM), lambda b, h: (b, 0, h)),
        out_shape=jax.ShapeDtypeStruct((bsz, seq, DA_WIDTH), BF16),
        scratch_shapes=[pltpu.VMEM((seq, MXU_WIDTH), BF16),
                        pltpu.VMEM((2, tq, 1), F32),
                        pltpu.VMEM((2, tq, MXU_WIDTH), F32),
                        pltpu.VMEM((2, 2, tq, tk), F32)],
        compiler_params=_params(("parallel", "parallel")),
        name="diff_attention",
    )(da, da, da, lam_row, subln_row)


def _ret_state_kernel(kf_ref, vf_ref, kb_ref, vb_ref, dkf_ref, dkb_ref, gc_ref,
                      sf_out, sb_out, sf_ref, sb_ref):
    @pl.when(pl.program_id(1) == 0)
    def _():
        sf_ref[...] = jnp.zeros(sf_ref.shape, F32)
        sb_ref[...] = jnp.zeros(sb_ref.shape, F32)

    sf_out[...] = sf_ref[...]
    sb_out[...] = sb_ref[...]

    def update(s_ref, k_ref, v_ref, dk_ref):
        kd = (k_ref[...].astype(F32) * dk_ref[...]).astype(BF16)
        v = v_ref[...]
        for pr in range(RET_WIDTH // LANES):
            sl = slice(pr * LANES, (pr + 1) * LANES)
            kv = _dot_tn(kd[:, sl], v[:, sl])
            keep = gc_ref[sl, :]
            s_ref[sl, :] = keep * s_ref[sl, :] + jnp.where(keep > 0.0, kv, 0.0)

    update(sf_ref, kf_ref, vf_ref, dkf_ref)
    update(sb_ref, kb_ref, vb_ref, dkb_ref)


def _ret_out_kernel(q_ref, k_ref, v_ref, g_ref, sf_ref, sb_ref, dec_ref, dqf_ref, dqb_ref,
                    avg_ref, gng_ref, gnb_ref, o_ref):
    q = q_ref[...]
    k = k_ref[...]
    v = v_ref[...]
    qf = q.astype(F32)
    lane = lax.broadcasted_iota(jnp.int32, (q.shape[0], LANES), 1)
    zero = jnp.zeros((q.shape[0], LANES), BF16)
    parts = []
    for pr in range(RET_WIDTH // LANES):
        sl = slice(pr * LANES, (pr + 1) * LANES)
        qp, kp, vp = q[:, sl], k[:, sl], v[:, sl]
        acc = _dot((qf[:, sl] * dqf_ref[:, sl]).astype(BF16), sf_ref[sl, :].astype(BF16))
        acc += _dot((qf[:, sl] * dqb_ref[:, sl]).astype(BF16), sb_ref[sl, :].astype(BF16))
        for hh in range(LANES // RET_HEAD_DIM):
            mine = (lane >= hh * RET_HEAD_DIM) & (lane < (hh + 1) * RET_HEAD_DIM)
            s = _dot_nt(jnp.where(mine, qp, zero), kp) * dec_ref[pr * 2 + hh]
            acc += _dot(s.astype(BF16), jnp.where(mine, vp, zero))
        parts.append(acc)
    o = jnp.concatenate(parts, axis=-1)
    avg = avg_ref[...]
    mu = _dot_split(o, avg)
    d = o - mu
    var = _dot_split(d * d, avg)
    y = d * lax.rsqrt(var + LN_EPS) * gng_ref[...] + gnb_ref[...]
    g = g_ref[...]
    o_ref[...] = (g * _sigmoid(g) * y).astype(o_ref.dtype)


def retention(ret, g, gn_g, gn_b, bsz, seq):
    c = min(RET_CHUNK, seq)
    nc = seq // c
    heads = jnp.arange(RET_HEADS, dtype=F32)
    log_gamma = jnp.log(1.0 - 2.0 ** (-5.0 - heads))
    lg_cols = jnp.repeat(log_gamma, RET_HEAD_DIM)[None, :]
    idx = jnp.arange(c, dtype=F32)[:, None]
    dk_f = jnp.exp(lg_cols * (c - 1 - idx))
    dk_b = jnp.exp(lg_cols * idx)
    dq_f = jnp.exp(lg_cols * (idx + 1))
    dq_b = jnp.exp(lg_cols * (c - idx))
    dist = jnp.abs(idx - idx.T)
    decay = jnp.exp(log_gamma[:, None, None] * dist[None])
    row_head = jnp.arange(RET_WIDTH)[:, None] // RET_HEAD_DIM
    col_head = (jnp.arange(LANES)[None, :] // RET_HEAD_DIM) + 2 * (jnp.arange(RET_WIDTH)[:, None] // LANES)
    gc = jnp.where(row_head == col_head, jnp.exp(lg_cols.T * c), 0.0).astype(F32)
    seg = jnp.arange(RET_WIDTH) // RET_HEAD_DIM
    avg = jnp.where(seg[:, None] == seg[None, :], 1.0 / RET_HEAD_DIM, 0.0).astype(BF16)
    gng = jnp.tile(gn_g, RET_HEADS)[None, :]
    gnb = jnp.tile(gn_b, RET_HEADS)[None, :]

    blk = lambda col: pl.BlockSpec((None, c, RET_WIDTH), lambda b, j: (b, j, col))
    blk_rev = lambda col: pl.BlockSpec((None, c, RET_WIDTH), lambda b, j: (b, nc - 1 - j, col))
    st_shape = jax.ShapeDtypeStruct((bsz, nc, RET_WIDTH, LANES), F32)
    sf, sb = pl.pallas_call(
        _ret_state_kernel,
        grid=(bsz, nc),
        in_specs=[blk(1), blk(2), blk_rev(1), blk_rev(2),
                  _const_spec((c, RET_WIDTH)), _const_spec((c, RET_WIDTH)),
                  _const_spec((RET_WIDTH, LANES))],
        out_specs=[pl.BlockSpec((None, None, RET_WIDTH, LANES), lambda b, j: (b, j, 0, 0)),
                   pl.BlockSpec((None, None, RET_WIDTH, LANES), lambda b, j: (b, nc - 1 - j, 0, 0))],
        out_shape=[st_shape, st_shape],
        scratch_shapes=[pltpu.VMEM((RET_WIDTH, LANES), F32), pltpu.VMEM((RET_WIDTH, LANES), F32)],
        compiler_params=_params(("parallel", "arbitrary")),
        name="retention_state",
    )(ret, ret, ret, ret, dk_f, dk_b, gc)

    st_spec = pl.BlockSpec((None, None, RET_WIDTH, LANES), lambda b, j: (b, j, 0, 0))
    return pl.pallas_call(
        _ret_out_kernel,
        grid=(bsz, nc),
        in_specs=[blk(0), blk(1), blk(2), blk(0), st_spec, st_spec,
                  _const_spec((RET_HEADS, c, c)), _const_spec((c, RET_WIDTH)),
                  _const_spec((c, RET_WIDTH)), _const_spec((RET_WIDTH, RET_WIDTH)),
                  _const_spec((1, RET_WIDTH)), _const_spec((1, RET_WIDTH))],
        out_specs=blk(0),
        out_shape=jax.ShapeDtypeStruct((bsz, seq, RET_WIDTH), BF16),
        compiler_params=_params(("parallel", "parallel")),
        name="retention_out",
    )(ret, ret, ret, g, sf, sb, decay, dq_f, dq_b, avg, gng, gnb)


def _cmul(ar, ai, br, bi):
    return ar * br - ai * bi, ar * bi + ai * br


def s5_matrices(A_re, A_im, log_dt, B_re, B_im, C_re, C_im, D):
    T, G, P, Cn = S5_CHUNK, S5_GROUPS, S5_STATE, S5_GROUP
    gh = S5_HALF // Cn
    step = jnp.exp(log_dt.astype(F32))[..., None]
    a_re = A_re.astype(F32)
    a_im = A_im.astype(F32)
    e = jnp.exp(step * a_re)
    abar_re = e * jnp.cos(step * a_im)
    abar_im = e * jnp.sin(step * a_im)
    den = a_re * a_re + a_im * a_im
    nr = abar_re - 1.0
    ni = abar_im
    coef_re = (nr * a_re + ni * a_im) / den
    coef_im = (ni * a_re - nr * a_im) / den
    b_re = B_re.astype(F32)
    b_im = B_im.astype(F32)
    bb_re = coef_re[..., None] * b_re - coef_im[..., None] * b_im
    bb_im = coef_re[..., None] * b_im + coef_im[..., None] * b_re
    pows = [(jnp.ones_like(abar_re), jnp.zeros_like(abar_re))]
    for _ in range(T):
        pows.append(_cmul(pows[-1][0], pows[-1][1], abar_re, abar_im))
    pw_re = jnp.stack([p[0] for p in pows])
    pw_im = jnp.stack([p[1] for p in pows])
    c_re = C_re.astype(F32)
    c_im = C_im.astype(F32)
    eye = jnp.eye(gh, dtype=F32)
    t_idx = jnp.arange(T)

    def state_in(direction, order):
        pr = pw_re[order, direction]
        pi = pw_im[order, direction]
        vr, vi = _cmul(pr[:, :, None, :], pi[:, :, None, :],
                       jnp.swapaxes(bb_re[direction], 1, 2)[None],
                       jnp.swapaxes(bb_im[direction], 1, 2)[None])
        return vr, vi

    def read_out(direction, order):
        pr = pw_re[order, direction][:, :, None, :]
        pi = pw_im[order, direction][:, :, None, :]
        er, ei = _cmul(c_re[direction][None], c_im[direction][None], pr, pi)
        return er, ei

    def lag_kernel(direction):
        pr = pw_re[:T, direction][:, :, :, None]
        pi = pw_im[:T, direction][:, :, :, None]
        wr, wi = _cmul(pr, pi, bb_re[direction][None], bb_im[direction][None])
        return (jnp.einsum('gop,dgpi->dgoi', c_re[direction], wr)
                - jnp.einsum('gop,dgpi->dgoi', c_im[direction], wi))

    vfr, vfi = state_in(0, T - 1 - t_idx)
    vbr, vbi = state_in(1, t_idx)
    efr, efi = read_out(0, t_idx + 1)
    ebr, ebi = read_out(1, T - t_idx)
    kf = lag_kernel(0)
    kb = lag_kernel(1)
    lag = t_idx[None, :] - t_idx[:, None]
    toe = (jnp.where((lag >= 0)[:, :, None, None, None], kf[jnp.clip(lag, 0, T - 1)], 0.0)
           + jnp.where((lag <= 0)[:, :, None, None, None], kb[jnp.clip(-lag, 0, T - 1)], 0.0))
    d_diag = D.astype(F32).reshape(G, Cn)[:, :, None] * jnp.eye(Cn, dtype=F32)[None]
    toe = toe + jnp.where((lag == 0)[:, :, None, None, None], d_diag[None, None], 0.0)

    mb, tp, mc, a8 = [], [], [], []
    for h in range(2):
        gs = slice(h * gh, (h + 1) * gh)
        sin_blk = lambda v: jnp.einsum('sgcp,gk->sgckp', v[:, gs], eye).reshape(T * gh * Cn, gh * P)
        mb.append(jnp.concatenate([sin_blk(vfr), sin_blk(vfi), sin_blk(vbr), sin_blk(vbi)], axis=1))
        tp.append(jnp.einsum('stgoi,gk->sgitko', toe[:, :, gs], eye).reshape(T * gh * Cn, T * gh * Cn))
        out_blk = lambda v: jnp.einsum('tgop,gk->gptko', v[:, gs], eye).reshape(gh * P, T * gh * Cn)
        mc.append(jnp.concatenate([out_blk(efr), -out_blk(efi), out_blk(ebr), -out_blk(ebi)], axis=0))
        rows = [pw_re[T, 0, gs].reshape(-1), pw_im[T, 0, gs].reshape(-1),
                pw_re[T, 1, gs].reshape(-1), pw_im[T, 1, gs].reshape(-1)]
        a8.append(jnp.stack(rows + rows))
    return (jnp.stack(mb).astype(BF16), jnp.stack(tp).astype(BF16),
            jnp.stack(mc).astype(BF16), jnp.stack(a8))


def _s5_kernel(u_ref, mb_ref, tp_ref, mc_ref, a8_ref, y_ref, u8_ref, w_ref, *, sub):
    rows = w_ref.shape[0]
    ns = a8_ref.shape[1]
    for r in range(0, rows, sub):
        steps = [u_ref[pl.ds(r * S5_CHUNK + s, sub, stride=S5_CHUNK), :].astype(BF16)
                 for s in range(S5_CHUNK)]
        u8 = jnp.concatenate(steps, axis=-1)
        u8_ref[r:r + sub, :] = u8
        w_ref[r:r + sub, :] = _dot(u8, mb_ref[...])

    afr, afi = a8_ref[0:1, :], a8_ref[1:2, :]
    abr, abi = a8_ref[2:3, :], a8_ref[3:4, :]

    def scan_step(j, carry):
        xfr, xfi, xbr, xbi = carry
        jb = rows - 1 - j
        wf_r = w_ref[pl.ds(j, 1), 0:ns]
        wf_i = w_ref[pl.ds(j, 1), ns:2 * ns]
        wb_r = w_ref[pl.ds(jb, 1), 2 * ns:3 * ns]
        wb_i = w_ref[pl.ds(jb, 1), 3 * ns:4 * ns]
        w_ref[pl.ds(j, 1), 0:ns] = xfr
        w_ref[pl.ds(j, 1), ns:2 * ns] = xfi
        w_ref[pl.ds(jb, 1), 2 * ns:3 * ns] = xbr
        w_ref[pl.ds(jb, 1), 3 * ns:4 * ns] = xbi
        nfr = afr * xfr - afi * xfi + wf_r
        nfi = afr * xfi + afi * xfr + wf_i
        nbr = abr * xbr - abi * xbi + wb_r
        nbi = abr * xbi + abi * xbr + wb_i
        return nfr, nfi, nbr, nbi

    z = jnp.zeros((1, ns), F32)
    lax.fori_loop(0, rows, scan_step, (z, z, z, z))

    for r in range(0, rows, sub):
        y8 = (_dot(u8_ref[r:r + sub, :], tp_ref[...])
              + _dot(w_ref[r:r + sub, :].astype(BF16), mc_ref[...]))
        for t in range(S5_CHUNK):
            y_ref[pl.ds(r * S5_CHUNK + t, sub, stride=S5_CHUNK), :] = y8[:, t * S5_HALF:(t + 1) * S5_HALF]


def s5_mixer(u, mats, bsz, seq):
    mb, tp, mc, a8 = mats
    n = bsz * seq
    rows = seq // S5_CHUNK
    width = S5_CHUNK * S5_HALF
    ns = (S5_HALF // S5_GROUP) * S5_STATE
    kern = functools.partial(_s5_kernel, sub=min(256, rows))
    wspec = lambda a: pl.BlockSpec((None,) + a.shape[1:], lambda h, b: (h, 0, 0),
                                   pipeline_mode=pl.Buffered(1))
    tokens = pl.BlockSpec((None, seq, S5_HALF), lambda h, b: (h, b, 0))
    return pl.pallas_call(
        kern,
        grid=(2, bsz),
        in_specs=[tokens, wspec(mb), wspec(tp), wspec(mc), wspec(a8)],
        out_specs=tokens,
        out_shape=jax.ShapeDtypeStruct((2, n, S5_HALF), F32),
        scratch_shapes=[pltpu.VMEM((rows, width), BF16), pltpu.VMEM((rows, 4 * ns), F32)],
        compiler_params=_params(("arbitrary", "arbitrary")),
        name="s5_mixer",
    )(u, mb, tp, mc, a8)


def _out_proj_kernel(x_ref, da_ref, ret_ref, y5_ref, gluw_ref, glub_ref, wout_ref,
                     g_ref, b_ref, o_ref, *, alpha):
    y = jnp.concatenate([y5_ref[0], y5_ref[1]], axis=-1)
    ya = _gelu_tanh(y)
    gate = _sigmoid(_dot(ya.astype(BF16), gluw_ref[...]) + glub_ref[...])
    ys5 = (ya * gate).astype(BF16)
    c1 = DA_WIDTH
    c2 = DA_WIDTH + RET_WIDTH
    mix = (_dot(da_ref[...], wout_ref[0:c1, :]) + _dot(ret_ref[...], wout_ref[c1:c2, :])
           + _dot(ys5, wout_ref[c2:, :]))
    o_ref[...] = _layer_norm(alpha * x_ref[...] + mix, g_ref[...], b_ref[...])


def out_proj(x, y_da, y_ret, y5, glu_w, glu_b, w_out, ln_g, ln_b, alpha, tm=512):
    n = x.shape[0]
    tm = min(tm, n)
    row = lambda i: (i, 0)
    return pl.pallas_call(
        functools.partial(_out_proj_kernel, alpha=alpha),
        grid=(n // tm,),
        in_specs=[pl.BlockSpec((tm, D_MODEL), row), pl.BlockSpec((tm, DA_WIDTH), row),
                  pl.BlockSpec((tm, RET_WIDTH), row),
                  pl.BlockSpec((2, tm, S5_HALF), lambda i: (0, i, 0)),
                  _const_spec((S5_WIDTH, S5_WIDTH)), _const_spec((1, S5_WIDTH)),
                  _const_spec((D_MODEL, D_MODEL)),
                  _const_spec((1, D_MODEL)), _const_spec((1, D_MODEL))],
        out_specs=pl.BlockSpec((tm, D_MODEL), row),
        out_shape=jax.ShapeDtypeStruct((n, D_MODEL), F32),
        compiler_params=_params(("parallel",)),
        name="out_proj",
    )(x, y_da, y_ret, y5, glu_w, glu_b, w_out, ln_g, ln_b)


def _ffn_kernel(x_ref, xp_ref, xn_ref, p_ref, wup_ref, cw_ref, cb_ref, wdn_ref,
                plew_ref, gatew_ref, g_ref, b_ref, o_ref, act_ref, *, alpha, tiles_per_seq):
    tm = x_ref.shape[0]
    i = pl.program_id(0)
    has_prev = ((i % tiles_per_seq) != 0).astype(F32)
    has_next = ((i % tiles_per_seq) != tiles_per_seq - 1).astype(F32)
    x = x_ref[...]
    xb = x.astype(BF16)
    xpb = xp_ref[...].astype(BF16)
    xnb = xn_ref[...].astype(BF16)
    row = lax.broadcasted_iota(jnp.int32, (tm, FF_CHUNK), 0)
    halo = xp_ref.shape[0]

    for c in range(0, D_FF, FF_CHUNK):
        wg = wup_ref[:, c:c + FF_CHUNK]
        gate = _dot(xb, wg)
        val = _dot(xb, wup_ref[:, D_FF + c:D_FF + c + FF_CHUNK])
        before = _dot(xpb, wg)[halo - 1:halo, :] * has_prev
        after = _dot(xnb, wg)[0:1, :] * has_next
        left = jnp.where(row == 0, before, pltpu.roll(gate, 1, 0))
        right = jnp.where(row == tm - 1, after, pltpu.roll(gate, tm - 1, 0))
        conv = (cw_ref[0:1, c:c + FF_CHUNK] * left + cw_ref[1:2, c:c + FF_CHUNK] * gate
                + cw_ref[2:3, c:c + FF_CHUNK] * right + cb_ref[:, c:c + FF_CHUNK])
        act_ref[:, c:c + FF_CHUNK] = (_gelu_tanh(conv) * val).astype(BF16)

    f = _dot(act_ref[...], wdn_ref[...])
    ple = _dot(p_ref[...].astype(BF16), plew_ref[...]) * _sigmoid(_dot(xb, gatew_ref[...]))
    o_ref[...] = _layer_norm(alpha * x + f + ple, g_ref[...], b_ref[...])


def conv_ffn_ple(x, p, w_up, conv_w, conv_b, w_down, ple_w, gate_w, ln_g, ln_b, alpha, seq, tm=512):
    n = x.shape[0]
    tm = min(tm, seq)
    halo = 8
    tiles_per_seq = seq // tm
    per = tm // halo
    last = n // halo - 1
    row = lambda i: (i, 0)
    kern = functools.partial(_ffn_kernel, alpha=alpha, tiles_per_seq=tiles_per_seq)
    return pl.pallas_call(
        kern,
        grid=(n // tm,),
        in_specs=[pl.BlockSpec((tm, D_MODEL), row),
                  pl.BlockSpec((halo, D_MODEL), lambda i: (jnp.maximum(i * per - 1, 0), 0)),
                  pl.BlockSpec((halo, D_MODEL), lambda i: (jnp.minimum((i + 1) * per, last), 0)),
                  pl.BlockSpec((tm, PLE_DIM), row),
                  _const_spec((D_MODEL, 2 * D_FF)), _const_spec((3, D_FF)), _const_spec((1, D_FF)),
                  _const_spec((D_FF, D_MODEL)), _const_spec((PLE_DIM, D_MODEL)),
                  _const_spec((D_MODEL, D_MODEL)),
                  _const_spec((1, D_MODEL)), _const_spec((1, D_MODEL))],
        out_specs=pl.BlockSpec((tm, D_MODEL), row),
        out_shape=jax.ShapeDtypeStruct((n, D_MODEL), F32),
        scratch_shapes=[pltpu.VMEM((tm, D_FF), BF16)],
        compiler_params=_params(("parallel",)),
        name="conv_ffn_ple",
    )(x, x, x, p, w_up, conv_w, conv_b, w_down, ple_w, gate_w, ln_g, ln_b)


def kernel(x, p, positions, w_in, da_lambda_q1, da_lambda_k1, da_lambda_q2, da_lambda_k2,
           da_subln_g, ret_gn_g, ret_gn_b, s5_A_re, s5_A_im, s5_log_dt, s5_B_re, s5_B_im,
           s5_C_re, s5_C_im, s5_D, s5_glu_w, s5_glu_b, w_out, ln1_g, ln1_b,
           ffn_w_up, ffn_conv_w, ffn_conv_b, ffn_w_down, ple_w, ple_gate_w, ln2_g, ln2_b):
    bsz, seq, _ = x.shape
    depth = w_in.shape[0]
    n = bsz * seq
    alpha = (2 * depth) ** 0.25
    cos, sin = rope_tables(positions)
    xf = x.reshape(n, D_MODEL)
    row = lambda v: v.reshape(1, -1).astype(F32)
    for i in range(depth):
        lambda_init = 0.8 - 0.6 * math.exp(-0.3 * i)
        lam = (jnp.exp(jnp.sum(da_lambda_q1[i].astype(F32) * da_lambda_k1[i].astype(F32)))
               - jnp.exp(jnp.sum(da_lambda_q2[i].astype(F32) * da_lambda_k2[i].astype(F32)))
               + lambda_init)
        lam_row = jnp.full((1, DA_V_DIM), lam, F32)
        da, ret, g, u = in_proj(xf, w_in[i].astype(BF16), cos, sin)
        y_da = diff_attention(da.reshape(bsz, seq, -1), lam_row, row(da_subln_g[i]),
                              1.0 - lambda_init, bsz, seq)
        y_ret = retention(ret.reshape(bsz, seq, -1), g.reshape(bsz, seq, -1),
                          ret_gn_g[i].astype(F32), ret_gn_b[i].astype(F32), bsz, seq)
        mats = s5_matrices(s5_A_re[i], s5_A_im[i], s5_log_dt[i], s5_B_re[i], s5_B_im[i],
                           s5_C_re[i], s5_C_im[i], s5_D[i])
        y5 = s5_mixer(u, mats, bsz, seq)
        x1 = out_proj(xf, y_da.reshape(n, -1), y_ret.reshape(n, -1), y5,
                      s5_glu_w[i].astype(BF16), row(s5_glu_b[i]), w_out[i].astype(BF16),
                      row(ln1_g[i]), row(ln1_b[i]), alpha)
        xf = conv_ffn_ple(x1, p[i].reshape(n, PLE_DIM), ffn_w_up[i].astype(BF16),
                          ffn_conv_w[i].astype(F32), row(ffn_conv_b[i]),
                          ffn_w_down[i].astype(BF16), ple_w[i].astype(BF16),
                          ple_gate_w[i].astype(BF16), row(ln2_g[i]), row(ln2_b[i]), alpha, seq)
    return xf.reshape(bsz, seq, D_MODEL)
```

```python
import functools
import math

import jax
import jax.numpy as jnp
from jax import lax
from jax.experimental import pallas as pl
from jax.experimental.pallas import tpu as pltpu

F32 = jnp.float32
BF16 = jnp.bfloat16

D_MODEL = 1024
PLE_DIM = 256
DA_HEADS = 4
DA_QK_DIM = 64
DA_V_DIM = 128
DA_WIDTH = DA_HEADS * DA_V_DIM
RET_HEADS = 4
RET_HEAD_DIM = 64
RET_WIDTH = RET_HEADS * RET_HEAD_DIM
S5_WIDTH = 256
S5_GROUP = 16
S5_GROUPS = S5_WIDTH // S5_GROUP
S5_STATE = 64
D_FF = 2816
ROPE_THETA = 10000.0
LN_EPS = 1e-5
RMS_EPS = 1e-6

COL_DA_Q = 0
COL_DA_K = COL_DA_Q + DA_HEADS * 2 * DA_QK_DIM
COL_DA_V = COL_DA_K + DA_HEADS * 2 * DA_QK_DIM
COL_RET_Q = COL_DA_V + DA_WIDTH
COL_RET_K = COL_RET_Q + RET_WIDTH
COL_RET_V = COL_RET_K + RET_WIDTH
COL_RET_G = COL_RET_V + RET_WIDTH
COL_S5_U = COL_RET_G + RET_WIDTH
IN_COLS = COL_S5_U + S5_WIDTH

LANES = 128
MXU_WIDTH = 256
S5_CHUNK = 8
S5_HALF = 128
RET_CHUNK = 256
FF_CHUNK = 256
LOG2E = 1.4426950408889634
NEG_BIG = -1e30
VMEM_LIMIT = 56 * 1024 * 1024


def _params(sem, vmem=VMEM_LIMIT):
    return pltpu.CompilerParams(dimension_semantics=sem, vmem_limit_bytes=vmem)


def _const_spec(shape):
    nd = len(shape)
    return pl.BlockSpec(shape, lambda *_: (0,) * nd, pipeline_mode=pl.Buffered(1))


def _layer_norm(x, g, b):
    mu = jnp.mean(x, axis=-1, keepdims=True)
    d = x - mu
    var = jnp.mean(d * d, axis=-1, keepdims=True)
    return d * lax.rsqrt(var + LN_EPS) * g + b


def _gelu_tanh(x):
    return 0.5 * x * (1.0 + jnp.tanh(math.sqrt(2.0 / math.pi) * (x + 0.044715 * (x * x * x))))


def _sigmoid(x):
    return 1.0 / (1.0 + jnp.exp(-x))


def _dot(a, b):
    return jnp.dot(a, b, preferred_element_type=F32)


def _dot_nt(a, b):
    return lax.dot_general(a, b, (((1,), (1,)), ((), ())), preferred_element_type=F32)


def _dot_tn(a, b):
    return lax.dot_general(a, b, (((0,), (0,)), ((), ())), preferred_element_type=F32)


def _dot_split(x, w):
    hi = x.astype(BF16)
    lo = (x - hi.astype(F32)).astype(BF16)
    return _dot(hi, w) + _dot(lo, w)


def _rope_table_kernel(pos_ref, freq_ref, sign_ref, cos_ref, sin_ref):
    ang = pos_ref[...].astype(F32) * freq_ref[...]
    cos_ref[...] = jnp.cos(ang)
    sin_ref[...] = jnp.sin(ang) * sign_ref[...]


def rope_tables(positions, tm=1024):
    n = positions.size
    half = DA_QK_DIM // 2
    inv_freq = ROPE_THETA ** (-jnp.arange(0, DA_QK_DIM, 2, dtype=F32) / DA_QK_DIM)
    freq_row = jnp.tile(inv_freq, LANES // half).reshape(1, LANES)
    lane = jnp.arange(LANES)
    sign_row = jnp.where(lane % DA_QK_DIM < half, -1.0, 1.0).astype(F32).reshape(1, LANES)
    pos = positions.reshape(n, 1)
    tm = min(tm, n)
    return pl.pallas_call(
        _rope_table_kernel,
        grid=(n // tm,),
        in_specs=[pl.BlockSpec((tm, 1), lambda i: (i, 0)),
                  _const_spec((1, LANES)), _const_spec((1, LANES))],
        out_specs=[pl.BlockSpec((tm, LANES), lambda i: (i, 0)),
                   pl.BlockSpec((tm, LANES), lambda i: (i, 0))],
        out_shape=[jax.ShapeDtypeStruct((n, LANES), F32)] * 2,
        compiler_params=_params(("parallel",)),
        name="rope_tables",
    )(pos, freq_row, sign_row)


def _rope(x, cos, sin, first_half):
    swapped = jnp.where(first_half, pltpu.roll(x, LANES - DA_QK_DIM // 2, 1),
                        pltpu.roll(x, DA_QK_DIM // 2, 1))
    return x * cos + swapped * sin


def _in_proj_kernel(x_ref, w_ref, cos_ref, sin_ref, da_ref, ret_ref, g_ref, u_ref):
    xb = x_ref[...].astype(BF16)
    cos = cos_ref[...]
    sin = sin_ref[...]
    lane = lax.broadcasted_iota(jnp.int32, cos.shape, 1)
    first_half = (lane % DA_QK_DIM) < (DA_QK_DIM // 2)
    q_scale = DA_QK_DIM ** -0.5 * LOG2E
    k_scale = RET_HEAD_DIM ** -0.5

    def proj(col):
        return _dot(xb, w_ref[:, col:col + MXU_WIDTH])

    def roped(z, scale):
        parts = [_rope(z[:, a:a + LANES], cos, sin, first_half) for a in (0, LANES)]
        out = jnp.concatenate(parts, axis=-1)
        return out if scale is None else out * scale

    for c in range(0, COL_DA_K, MXU_WIDTH):
        da_ref[:, c:c + MXU_WIDTH] = roped(proj(c), q_scale).astype(BF16)
    for c in range(COL_DA_K, COL_DA_V, MXU_WIDTH):
        da_ref[:, c:c + MXU_WIDTH] = roped(proj(c), None).astype(BF16)
    for c in range(COL_DA_V, COL_RET_Q, MXU_WIDTH):
        da_ref[:, c:c + MXU_WIDTH] = proj(c).astype(BF16)
    ret_ref[:, 0:RET_WIDTH] = roped(proj(COL_RET_Q), None).astype(BF16)
    ret_ref[:, RET_WIDTH:2 * RET_WIDTH] = roped(proj(COL_RET_K), k_scale).astype(BF16)
    ret_ref[:, 2 * RET_WIDTH:3 * RET_WIDTH] = proj(COL_RET_V).astype(BF16)
    g_ref[...] = proj(COL_RET_G)
    u = proj(COL_S5_U)
    u_ref[0] = u[:, :S5_HALF]
    u_ref[1] = u[:, S5_HALF:]


def in_proj(x, w_bf16, cos, sin, tm=512):
    n = x.shape[0]
    tm = min(tm, n)
    row = lambda i: (i, 0)
    return pl.pallas_call(
        _in_proj_kernel,
        grid=(n // tm,),
        in_specs=[pl.BlockSpec((tm, D_MODEL), row), _const_spec((D_MODEL, IN_COLS)),
                  pl.BlockSpec((tm, LANES), row), pl.BlockSpec((tm, LANES), row)],
        out_specs=[pl.BlockSpec((tm, COL_RET_Q), row), pl.BlockSpec((tm, 3 * RET_WIDTH), row),
                   pl.BlockSpec((tm, RET_WIDTH), row),
                   pl.BlockSpec((2, tm, S5_HALF), lambda i: (0, i, 0))],
        out_shape=[jax.ShapeDtypeStruct((n, COL_RET_Q), BF16),
                   jax.ShapeDtypeStruct((n, 3 * RET_WIDTH), BF16),
                   jax.ShapeDtypeStruct((n, RET_WIDTH), F32),
                   jax.ShapeDtypeStruct((2, n, S5_HALF), F32)],
        compiler_params=_params(("parallel",)),
        name="in_proj",
    )(x, w_bf16, cos, sin)


ONES_ROWS = 16


def _diff_attn_kernel(q_ref, k_ref, v_ref, lam_ref, g_ref, o_ref,
                      km_ref, vt_ref, m_ref, acc_ref, s_ref, *, tq, tk, out_scale):
    seq = k_ref.shape[0]
    n_kv = seq // tk
    n_q = seq // tq
    k_all = k_ref[...]
    lane = lax.broadcasted_iota(jnp.int32, k_all.shape, 1)
    zero = jnp.zeros_like(k_all)
    km_ref[0] = jnp.where(lane < DA_QK_DIM, k_all, zero)
    km_ref[1] = jnp.where(lane >= DA_QK_DIM, k_all, zero)
    for i in range(n_kv):
        vt_ref[i, :DA_V_DIM, :] = v_ref[i * tk:(i + 1) * tk, :].astype(F32).T.astype(BF16)
        vt_ref[i, DA_V_DIM:, :] = jnp.ones((ONES_ROWS, tk), BF16)

    def scores(qi, ki, slot):
        q = q_ref[pl.ds(pl.multiple_of(qi * tq, tq), tq), :]
        rows = pl.ds(pl.multiple_of(ki * tk, tk), tk)
        for mi in range(2):
            s_ref[slot, mi] = _dot_nt(km_ref[mi, rows, :], q)

    def consume(ki, slot):
        vt = vt_ref[ki]
        for mi in range(2):
            s = s_ref[slot, mi]
            m_old = m_ref[mi]
            m_new = jnp.maximum(m_old, jnp.max(s, axis=0, keepdims=True))
            alpha = jnp.exp2(m_old - m_new)
            p = jnp.exp2(s - m_new).astype(BF16)
            acc_ref[mi] = acc_ref[mi] * alpha + _dot(vt, p)
            m_ref[mi] = m_new

    scores(0, 0, 0)

    def q_tile(qi, carry):
        m_ref[...] = jnp.full(m_ref.shape, NEG_BIG, F32)
        acc_ref[...] = jnp.zeros(acc_ref.shape, F32)

        def kv_pair(j, c):
            for u in range(2):
                ki = 2 * j + u
                wrap = ki + 1 >= n_kv
                scores(jnp.where(wrap, jnp.minimum(qi + 1, n_q - 1), qi),
                       jnp.where(wrap, 0, ki + 1), 1 - u)
                consume(ki, u)
            return c

        lax.fori_loop(0, n_kv // 2, kv_pair, 0)
        a0 = acc_ref[0]
        a1 = acc_ref[1]
        o = (a0[:DA_V_DIM] / a0[DA_V_DIM:DA_V_DIM + 1]
             - lam_ref[...] * (a1[:DA_V_DIM] / a1[DA_V_DIM:DA_V_DIM + 1]))
        ms = jnp.mean(o * o, axis=0, keepdims=True)
        o = o * lax.rsqrt(ms + RMS_EPS) * g_ref[...] * out_scale
        rows = pl.ds(pl.multiple_of(qi * tq, tq), tq)
        o_ref[rows, :] = o.T.astype(o_ref.dtype)
        return carry

    lax.fori_loop(0, n_q, q_tile, 0)


def diff_attention(da, lam, subln_col, out_scale, bsz, seq, tq=256, tk=4096):
    tq = min(tq, seq)
    tk = min(tk, seq // 2)
    assert seq % (2 * tk) == 0 and seq % tq == 0
    kern = functools.partial(_diff_attn_kernel, tq=tq, tk=tk, out_scale=out_scale)
    head_block = lambda part: pl.BlockSpec((seq, LANES), lambda b, h: (b, part * DA_HEADS + h))
    vrows = DA_V_DIM + ONES_ROWS
    return pl.pallas_call(
        kern,
        grid=(bsz, DA_HEADS),
        in_specs=[head_block(0), head_block(1), head_block(2),
                  _const_spec((1, 1)), _const_spec((DA_V_DIM, 1))],
        out_specs=pl.BlockSpec((seq, DA_V_DIM), lambda b, h: (b, h)),
        out_shape=jax.ShapeDtypeStruct((bsz * seq, DA_WIDTH), BF16),
        scratch_shapes=[pltpu.VMEM((2, seq, LANES), BF16),
                        pltpu.VMEM((seq // tk, vrows, tk), BF16),
                        pltpu.VMEM((2, 1, tq), F32),
                        pltpu.VMEM((2, vrows, tq), F32),
                        pltpu.VMEM((2, 2, tk, tq), F32)],
        compiler_params=_params(("parallel", "parallel")),
        name="diff_attention",
    )(da, da, da, lam, subln_col)


def _ret_state_kernel(kf_ref, vf_ref, kb_ref, vb_ref, dkf_ref, dkb_ref, gc_ref,
                      sf_out, sb_out, sf_ref, sb_ref):
    @pl.when(pl.program_id(1) == 0)
    def _():
        sf_ref[...] = jnp.zeros(sf_ref.shape, F32)
        sb_ref[...] = jnp.zeros(sb_ref.shape, F32)

    sf_out[...] = sf_ref[...]
    sb_out[...] = sb_ref[...]

    def update(s_ref, k_ref, v_ref, dk_ref):
        kd = (k_ref[...].astype(F32) * dk_ref[...]).astype(BF16)
        v = v_ref[...]
        for pr in range(RET_WIDTH // LANES):
            sl = slice(pr * LANES, (pr + 1) * LANES)
            kv = _dot_tn(kd[:, sl], v[:, sl])
            keep = gc_ref[sl, :]
            s_ref[sl, :] = keep * s_ref[sl, :] + jnp.where(keep > 0.0, kv, 0.0)

    update(sf_ref, kf_ref, vf_ref, dkf_ref)
    update(sb_ref, kb_ref, vb_ref, dkb_ref)


def _ret_out_kernel(q_ref, k_ref, v_ref, g_ref, sf_ref, sb_ref, dec_ref, dqf_ref, dqb_ref,
                    avg_ref, gng_ref, gnb_ref, o_ref):
    q = q_ref[...]
    k = k_ref[...]
    v = v_ref[...]
    qf = q.astype(F32)
    lane = lax.broadcasted_iota(jnp.int32, (q.shape[0], LANES), 1)
    zero = jnp.zeros((q.shape[0], LANES), BF16)
    parts = []
    for pr in range(RET_WIDTH // LANES):
        sl = slice(pr * LANES, (pr + 1) * LANES)
        qp, kp, vp = q[:, sl], k[:, sl], v[:, sl]
        acc = _dot((qf[:, sl] * dqf_ref[:, sl]).astype(BF16), sf_ref[sl, :].astype(BF16))
        acc += _dot((qf[:, sl] * dqb_ref[:, sl]).astype(BF16), sb_ref[sl, :].astype(BF16))
        for hh in range(LANES // RET_HEAD_DIM):
            mine = (lane >= hh * RET_HEAD_DIM) & (lane < (hh + 1) * RET_HEAD_DIM)
            s = _dot_nt(jnp.where(mine, qp, zero), kp) * dec_ref[pr * 2 + hh]
            acc += _dot(s.astype(BF16), jnp.where(mine, vp, zero))
        parts.append(acc)
    o = jnp.concatenate(parts, axis=-1)
    avg = avg_ref[...]
    mu = _dot_split(o, avg)
    d = o - mu
    var = _dot_split(d * d, avg)
    y = d * lax.rsqrt(var + LN_EPS) * gng_ref[...] + gnb_ref[...]
    g = g_ref[...]
    o_ref[...] = (g * _sigmoid(g) * y).astype(o_ref.dtype)


def retention(ret, g, gn_g, gn_b, bsz, seq):
    c = min(RET_CHUNK, seq)
    nc = seq // c
    heads = jnp.arange(RET_HEADS, dtype=F32)
    log_gamma = jnp.log(1.0 - 2.0 ** (-5.0 - heads))
    lg_cols = jnp.repeat(log_gamma, RET_HEAD_DIM)[None, :]
    idx = jnp.arange(c, dtype=F32)[:, None]
    dk_f = jnp.exp(lg_cols * (c - 1 - idx))
    dk_b = jnp.exp(lg_cols * idx)
    dq_f = jnp.exp(lg_cols * (idx + 1))
    dq_b = jnp.exp(lg_cols * (c - idx))
    dist = jnp.abs(idx - idx.T)
    decay = jnp.exp(log_gamma[:, None, None] * dist[None])
    row_head = jnp.arange(RET_WIDTH)[:, None] // RET_HEAD_DIM
    col_head = (jnp.arange(LANES)[None, :] // RET_HEAD_DIM) + 2 * (jnp.arange(RET_WIDTH)[:, None] // LANES)
    gc = jnp.where(row_head == col_head, jnp.exp(lg_cols.T * c), 0.0).astype(F32)
    seg = jnp.arange(RET_WIDTH) // RET_HEAD_DIM
    avg = jnp.where(seg[:, None] == seg[None, :], 1.0 / RET_HEAD_DIM, 0.0).astype(BF16)
    gng = jnp.tile(gn_g, RET_HEADS)[None, :]
    gnb = jnp.tile(gn_b, RET_HEADS)[None, :]

    blk = lambda col: pl.BlockSpec((c, RET_WIDTH), lambda b, j: (b * nc + j, col))
    blk_rev = lambda col: pl.BlockSpec((c, RET_WIDTH), lambda b, j: (b * nc + nc - 1 - j, col))
    st_shape = jax.ShapeDtypeStruct((bsz, nc, RET_WIDTH, LANES), F32)
    sf, sb = pl.pallas_call(
        _ret_state_kernel,
        grid=(bsz, nc),
        in_specs=[blk(1), blk(2), blk_rev(1), blk_rev(2),
                  _const_spec((c, RET_WIDTH)), _const_spec((c, RET_WIDTH)),
                  _const_spec((RET_WIDTH, LANES))],
        out_specs=[pl.BlockSpec((None, None, RET_WIDTH, LANES), lambda b, j: (b, j, 0, 0)),
                   pl.BlockSpec((None, None, RET_WIDTH, LANES), lambda b, j: (b, nc - 1 - j, 0, 0))],
        out_shape=[st_shape, st_shape],
        scratch_shapes=[pltpu.VMEM((RET_WIDTH, LANES), F32), pltpu.VMEM((RET_WIDTH, LANES), F32)],
        compiler_params=_params(("parallel", "arbitrary")),
        name="retention_state",
    )(ret, ret, ret, ret, dk_f, dk_b, gc)

    st_spec = pl.BlockSpec((None, None, RET_WIDTH, LANES), lambda b, j: (b, j, 0, 0))
    return pl.pallas_call(
        _ret_out_kernel,
        grid=(bsz, nc),
        in_specs=[blk(0), blk(1), blk(2), blk(0), st_spec, st_spec,
                  _const_spec((RET_HEADS, c, c)), _const_spec((c, RET_WIDTH)),
                  _const_spec((c, RET_WIDTH)), _const_spec((RET_WIDTH, RET_WIDTH)),
                  _const_spec((1, RET_WIDTH)), _const_spec((1, RET_WIDTH))],
        out_specs=blk(0),
        out_shape=jax.ShapeDtypeStruct((bsz * seq, RET_WIDTH), BF16),
        compiler_params=_params(("parallel", "parallel")),
        name="retention_out",
    )(ret, ret, ret, g, sf, sb, decay, dq_f, dq_b, avg, gng, gnb)


def _cmul(ar, ai, br, bi):
    return ar * br - ai * bi, ar * bi + ai * br


def s5_matrices(A_re, A_im, log_dt, B_re, B_im, C_re, C_im, D):
    T, G, P, Cn = S5_CHUNK, S5_GROUPS, S5_STATE, S5_GROUP
    gh = S5_HALF // Cn
    depth = A_re.shape[0]
    step = jnp.exp(log_dt.astype(F32))[..., None]
    a_re = A_re.astype(F32)
    a_im = A_im.astype(F32)
    d = jnp.arange(T + 1, dtype=F32).reshape(T + 1, 1, 1, 1, 1)
    mag = jnp.exp(d * (step * a_re))
    pw_re = mag * jnp.cos(d * (step * a_im))
    pw_im = mag * jnp.sin(d * (step * a_im))
    den = a_re * a_re + a_im * a_im
    nr = pw_re[1] - 1.0
    ni = pw_im[1]
    coef_re = ((nr * a_re + ni * a_im) / den)[..., None]
    coef_im = ((ni * a_re - nr * a_im) / den)[..., None]
    b_re = B_re.astype(F32)
    b_im = B_im.astype(F32)
    bb_re = coef_re * b_re - coef_im * b_im
    bb_im = coef_re * b_im + coef_im * b_re
    c_re = C_re.astype(F32)
    c_im = C_im.astype(F32)
    eye = jnp.eye(gh, dtype=F32)
    t_idx = jnp.arange(T)
    fwd_in, bwd_in = T - 1 - t_idx, t_idx
    fwd_out, bwd_out = t_idx + 1, T - t_idx

    def state_in(direction, order):
        return _cmul(pw_re[order, :, direction, :, None, :], pw_im[order, :, direction, :, None, :],
                     jnp.swapaxes(bb_re[:, direction], -1, -2)[None],
                     jnp.swapaxes(bb_im[:, direction], -1, -2)[None])

    def read_out(direction, order):
        return _cmul(c_re[None, :, direction], c_im[None, :, direction],
                     pw_re[order, :, direction, :, None, :], pw_im[order, :, direction, :, None, :])

    def lag_kernel(direction):
        wr, wi = _cmul(pw_re[:T, :, direction, :, :, None], pw_im[:T, :, direction, :, :, None],
                       bb_re[None, :, direction], bb_im[None, :, direction])
        return (jnp.einsum('lgop,dlgpi->dlgoi', c_re[:, direction], wr)
                - jnp.einsum('lgop,dlgpi->dlgoi', c_im[:, direction], wi))

    def halves(v, lead):
        return v.reshape(v.shape[:lead] + (2, gh) + v.shape[lead + 1:])

    vfr, vfi = state_in(0, fwd_in)
    vbr, vbi = state_in(1, bwd_in)
    v_all = halves(jnp.stack([vfr, vfi, vbr, vbi]), 3)
    mb = jnp.einsum('ksldgcp,gj->ldsgckjp', v_all, eye).reshape(depth, 2, T * gh * Cn, 4 * gh * P)

    efr, efi = read_out(0, fwd_out)
    ebr, ebi = read_out(1, bwd_out)
    e_all = halves(jnp.stack([efr, -efi, ebr, -ebi]), 3)
    mc = jnp.einsum('ktldgop,gj->ldkgptjo', e_all, eye).reshape(depth, 2, 4 * gh * P, T * gh * Cn)

    lag = t_idx[None, :] - t_idx[:, None]
    sel = lambda cond: cond[:, :, None, None, None, None]
    d_diag = D.astype(F32).reshape(depth, G, Cn)[..., None] * jnp.eye(Cn, dtype=F32)
    toe = (jnp.where(sel(lag >= 0), lag_kernel(0)[jnp.clip(lag, 0, T - 1)], 0.0)
           + jnp.where(sel(lag <= 0), lag_kernel(1)[jnp.clip(-lag, 0, T - 1)], 0.0)
           + jnp.where(sel(lag == 0), d_diag[None, None], 0.0))
    tp = jnp.einsum('stldgoi,gj->ldsgitjo', halves(toe, 3), eye).reshape(depth, 2, T * gh * Cn, T * gh * Cn)

    a_rows = jnp.stack([pw_re[T, :, 0], pw_im[T, :, 0], pw_re[T, :, 1], pw_im[T, :, 1]], axis=1)
    a8 = jnp.swapaxes(a_rows.reshape(depth, 4, 2, gh * P), 1, 2)
    a8 = jnp.concatenate([a8, a8], axis=2)
    return mb.astype(BF16), tp.astype(BF16), mc.astype(BF16), a8


def _s5_kernel(u_ref, mb_ref, tp_ref, mc_ref, a8_ref, y_ref, u8_ref, w_ref, *, sub):
    rows = w_ref.shape[0]
    ns = a8_ref.shape[1]
    for r in range(0, rows, sub):
        steps = [u_ref[pl.ds(r * S5_CHUNK + s, sub, stride=S5_CHUNK), :].astype(BF16)
                 for s in range(S5_CHUNK)]
        u8 = jnp.concatenate(steps, axis=-1)
        u8_ref[r:r + sub, :] = u8
        w_ref[r:r + sub, :] = _dot(u8, mb_ref[...])

    afr, afi = a8_ref[0:1, :], a8_ref[1:2, :]
    abr, abi = a8_ref[2:3, :], a8_ref[3:4, :]

    def scan_step(j, carry):
        xfr, xfi, xbr, xbi = carry
        jb = rows - 1 - j
        wf_r = w_ref[pl.ds(j, 1), 0:ns]
        wf_i = w_ref[pl.ds(j, 1), ns:2 * ns]
        wb_r = w_ref[pl.ds(jb, 1), 2 * ns:3 * ns]
        wb_i = w_ref[pl.ds(jb, 1), 3 * ns:4 * ns]
        w_ref[pl.ds(j, 1), 0:ns] = xfr
        w_ref[pl.ds(j, 1), ns:2 * ns] = xfi
        w_ref[pl.ds(jb, 1), 2 * ns:3 * ns] = xbr
        w_ref[pl.ds(jb, 1), 3 * ns:4 * ns] = xbi
        nfr = afr * xfr - afi * xfi + wf_r
        nfi = afr * xfi + afi * xfr + wf_i
        nbr = abr * xbr - abi * xbi + wb_r
        nbi = abr * xbi + abi * xbr + wb_i
        return nfr, nfi, nbr, nbi

    z = jnp.zeros((1, ns), F32)
    lax.fori_loop(0, rows, scan_step, (z, z, z, z))

    for r in range(0, rows, sub):
        y8 = (_dot(u8_ref[r:r + sub, :], tp_ref[...])
              + _dot(w_ref[r:r + sub, :].astype(BF16), mc_ref[...]))
        for t in range(S5_CHUNK):
            y_ref[pl.ds(r * S5_CHUNK + t, sub, stride=S5_CHUNK), :] = y8[:, t * S5_HALF:(t + 1) * S5_HALF]


def s5_mixer(u, mats, layer, bsz, seq):
    mb, tp, mc, a8 = mats
    n = bsz * seq
    rows = seq // S5_CHUNK
    width = S5_CHUNK * S5_HALF
    ns = (S5_HALF // S5_GROUP) * S5_STATE
    kern = functools.partial(_s5_kernel, sub=min(256, rows))
    wspec = lambda a: pl.BlockSpec((None, None) + a.shape[2:], lambda h, b: (layer, h, 0, 0),
                                   pipeline_mode=pl.Buffered(1))
    tokens = pl.BlockSpec((None, seq, S5_HALF), lambda h, b: (h, b, 0))
    return pl.pallas_call(
        kern,
        grid=(2, bsz),
        in_specs=[tokens, wspec(mb), wspec(tp), wspec(mc), wspec(a8)],
        out_specs=tokens,
        out_shape=jax.ShapeDtypeStruct((2, n, S5_HALF), F32),
        scratch_shapes=[pltpu.VMEM((rows, width), BF16), pltpu.VMEM((rows, 4 * ns), F32)],
        compiler_params=_params(("arbitrary", "arbitrary")),
        name="s5_mixer",
    )(u, mb, tp, mc, a8)


def _out_proj_kernel(x_ref, da_ref, ret_ref, y5_ref, gluw_ref, glub_ref, wout_ref,
                     g_ref, b_ref, o_ref, *, alpha):
    y = jnp.concatenate([y5_ref[0], y5_ref[1]], axis=-1)
    ya = _gelu_tanh(y)
    gate = _sigmoid(_dot(ya.astype(BF16), gluw_ref[...]) + glub_ref[...])
    ys5 = (ya * gate).astype(BF16)
    c1 = DA_WIDTH
    c2 = DA_WIDTH + RET_WIDTH
    mix = (_dot(da_ref[...], wout_ref[0:c1, :]) + _dot(ret_ref[...], wout_ref[c1:c2, :])
           + _dot(ys5, wout_ref[c2:, :]))
    o_ref[...] = _layer_norm(alpha * x_ref[...] + mix, g_ref[...], b_ref[...])


def out_proj(x, y_da, y_ret, y5, glu_w, glu_b, w_out, ln_g, ln_b, alpha, tm=512):
    n = x.shape[0]
    tm = min(tm, n)
    row = lambda i: (i, 0)
    return pl.pallas_call(
        functools.partial(_out_proj_kernel, alpha=alpha),
        grid=(n // tm,),
        in_specs=[pl.BlockSpec((tm, D_MODEL), row), pl.BlockSpec((tm, DA_WIDTH), row),
                  pl.BlockSpec((tm, RET_WIDTH), row),
                  pl.BlockSpec((2, tm, S5_HALF), lambda i: (0, i, 0)),
                  _const_spec((S5_WIDTH, S5_WIDTH)), _const_spec((1, S5_WIDTH)),
                  _const_spec((D_MODEL, D_MODEL)),
                  _const_spec((1, D_MODEL)), _const_spec((1, D_MODEL))],
        out_specs=pl.BlockSpec((tm, D_MODEL), row),
        out_shape=jax.ShapeDtypeStruct((n, D_MODEL), F32),
        compiler_params=_params(("parallel",)),
        name="out_proj",
    )(x, y_da, y_ret, y5, glu_w, glu_b, w_out, ln_g, ln_b)


def _ffn_kernel(x_ref, xp_ref, xn_ref, p_ref, wup_ref, cw_ref, cb_ref, wdn_ref,
                plew_ref, gatew_ref, g_ref, b_ref, o_ref, act_ref, *, alpha, tiles_per_seq):
    tm = x_ref.shape[0]
    i = pl.program_id(0)
    has_prev = ((i % tiles_per_seq) != 0).astype(F32)
    has_next = ((i % tiles_per_seq) != tiles_per_seq - 1).astype(F32)
    x = x_ref[...]
    xb = x.astype(BF16)
    xpb = xp_ref[...].astype(BF16)
    xnb = xn_ref[...].astype(BF16)
    row = lax.broadcasted_iota(jnp.int32, (tm, FF_CHUNK), 0)
    halo = xp_ref.shape[0]

    for c in range(0, D_FF, FF_CHUNK):
        wg = wup_ref[:, c:c + FF_CHUNK]
        gate = _dot(xb, wg)
        val = _dot(xb, wup_ref[:, D_FF + c:D_FF + c + FF_CHUNK])
        before = _dot(xpb, wg)[halo - 1:halo, :] * has_prev
        after = _dot(xnb, wg)[0:1, :] * has_next
        left = jnp.where(row == 0, before, pltpu.roll(gate, 1, 0))
        right = jnp.where(row == tm - 1, after, pltpu.roll(gate, tm - 1, 0))
        conv = (cw_ref[0:1, c:c + FF_CHUNK] * left + cw_ref[1:2, c:c + FF_CHUNK] * gate
                + cw_ref[2:3, c:c + FF_CHUNK] * right + cb_ref[:, c:c + FF_CHUNK])
        act_ref[:, c:c + FF_CHUNK] = (_gelu_tanh(conv) * val).astype(BF16)

    f = _dot(act_ref[...], wdn_ref[...])
    ple = _dot(p_ref[...].astype(BF16), plew_ref[...]) * _sigmoid(_dot(xb, gatew_ref[...]))
    o_ref[...] = _layer_norm(alpha * x + f + ple, g_ref[...], b_ref[...])


def conv_ffn_ple(x, p, layer, w_up, conv_w, conv_b, w_down, ple_w, gate_w, ln_g, ln_b, alpha, seq, tm=512):
    n = x.shape[0]
    tm = min(tm, seq)
    p_base = layer * (n // tm)
    halo = 8
    tiles_per_seq = seq // tm
    per = tm // halo
    last = n // halo - 1
    row = lambda i: (i, 0)
    kern = functools.partial(_ffn_kernel, alpha=alpha, tiles_per_seq=tiles_per_seq)
    return pl.pallas_call(
        kern,
        grid=(n // tm,),
        in_specs=[pl.BlockSpec((tm, D_MODEL), row),
                  pl.BlockSpec((halo, D_MODEL), lambda i: (jnp.maximum(i * per - 1, 0), 0)),
                  pl.BlockSpec((halo, D_MODEL), lambda i: (jnp.minimum((i + 1) * per, last), 0)),
                  pl.BlockSpec((tm, PLE_DIM), lambda i: (p_base + i, 0)),
                  _const_spec((D_MODEL, 2 * D_FF)), _const_spec((3, D_FF)), _const_spec((1, D_FF)),
                  _const_spec((D_FF, D_MODEL)), _const_spec((PLE_DIM, D_MODEL)),
                  _const_spec((D_MODEL, D_MODEL)),
                  _const_spec((1, D_MODEL)), _const_spec((1, D_MODEL))],
        out_specs=pl.BlockSpec((tm, D_MODEL), row),
        out_shape=jax.ShapeDtypeStruct((n, D_MODEL), F32),
        scratch_shapes=[pltpu.VMEM((tm, D_FF), BF16)],
        compiler_params=_params(("parallel",)),
        name="conv_ffn_ple",
    )(x, x, x, p, w_up, conv_w, conv_b, w_down, ple_w, gate_w, ln_g, ln_b)


def kernel(x, p, positions, w_in, da_lambda_q1, da_lambda_k1, da_lambda_q2, da_lambda_k2,
           da_subln_g, ret_gn_g, ret_gn_b, s5_A_re, s5_A_im, s5_log_dt, s5_B_re, s5_B_im,
           s5_C_re, s5_C_im, s5_D, s5_glu_w, s5_glu_b, w_out, ln1_g, ln1_b,
           ffn_w_up, ffn_conv_w, ffn_conv_b, ffn_w_down, ple_w, ple_gate_w, ln2_g, ln2_b):
    bsz, seq, _ = x.shape
    depth = w_in.shape[0]
    n = bsz * seq
    alpha = (2 * depth) ** 0.25
    cos, sin = rope_tables(positions)
    xf = x.reshape(n, D_MODEL)
    p_all = p.reshape(depth * n, PLE_DIM)
    mats = s5_matrices(s5_A_re, s5_A_im, s5_log_dt, s5_B_re, s5_B_im, s5_C_re, s5_C_im, s5_D)
    row = lambda v: v.reshape(1, -1).astype(F32)
    for i in range(depth):
        lambda_init = 0.8 - 0.6 * math.exp(-0.3 * i)
        lam = (jnp.exp(jnp.sum(da_lambda_q1[i].astype(F32) * da_lambda_k1[i].astype(F32)))
               - jnp.exp(jnp.sum(da_lambda_q2[i].astype(F32) * da_lambda_k2[i].astype(F32)))
               + lambda_init)
        da, ret, g, u = in_proj(xf, w_in[i].astype(BF16), cos, sin)
        y_da = diff_attention(da, lam.reshape(1, 1), da_subln_g[i].astype(F32).reshape(-1, 1),
                              1.0 - lambda_init, bsz, seq)
        y_ret = retention(ret, g, ret_gn_g[i].astype(F32), ret_gn_b[i].astype(F32), bsz, seq)
        y5 = s5_mixer(u, mats, i, bsz, seq)
        x1 = out_proj(xf, y_da, y_ret, y5,
                      s5_glu_w[i].astype(BF16), row(s5_glu_b[i]), w_out[i].astype(BF16),
                      row(ln1_g[i]), row(ln1_b[i]), alpha)
        xf = conv_ffn_ple(x1, p_all, i, ffn_w_up[i].astype(BF16),
                          ffn_conv_w[i].astype(F32), row(ffn_conv_b[i]),
                          ffn_w_down[i].astype(BF16), ple_w[i].astype(BF16),
                          ple_gate_w[i].astype(BF16), row(ln2_g[i]), row(ln2_b[i]), alpha, seq)
    return xf.reshape(bsz, seq, D_MODEL)
```

```python
import functools
import math

import jax
import jax.numpy as jnp
from jax import lax
from jax.experimental import pallas as pl
from jax.experimental.pallas import tpu as pltpu

F32 = jnp.float32
BF16 = jnp.bfloat16

D_MODEL = 1024
PLE_DIM = 256
DA_HEADS = 4
DA_QK_DIM = 64
DA_V_DIM = 128
DA_WIDTH = DA_HEADS * DA_V_DIM
RET_HEADS = 4
RET_HEAD_DIM = 64
RET_WIDTH = RET_HEADS * RET_HEAD_DIM
S5_WIDTH = 256
S5_GROUP = 16
S5_GROUPS = S5_WIDTH // S5_GROUP
S5_STATE = 64
D_FF = 2816
ROPE_THETA = 10000.0
LN_EPS = 1e-5
RMS_EPS = 1e-6

COL_DA_Q = 0
COL_DA_K = COL_DA_Q + DA_HEADS * 2 * DA_QK_DIM
COL_DA_V = COL_DA_K + DA_HEADS * 2 * DA_QK_DIM
COL_RET_Q = COL_DA_V + DA_WIDTH
COL_RET_K = COL_RET_Q + RET_WIDTH
COL_RET_V = COL_RET_K + RET_WIDTH
COL_RET_G = COL_RET_V + RET_WIDTH
COL_S5_U = COL_RET_G + RET_WIDTH
IN_COLS = COL_S5_U + S5_WIDTH

LANES = 128
MXU_WIDTH = 256
S5_CHUNK = 8
S5_HALF = 128
RET_CHUNK = 256
RET_CHUNKS_PER_STEP = 4
FF_CHUNK = 256
EPILOGUE_ROWS = 256
LOG2E = 1.4426950408889634
NEG_BIG = -1e30
VMEM_LIMIT = 56 * 1024 * 1024


def _params(sem, vmem=VMEM_LIMIT):
    return pltpu.CompilerParams(dimension_semantics=sem, vmem_limit_bytes=vmem)


def _const_spec(shape):
    nd = len(shape)
    return pl.BlockSpec(shape, lambda *_: (0,) * nd, pipeline_mode=pl.Buffered(1))


def _layer_norm(x, g, b):
    mu = jnp.mean(x, axis=-1, keepdims=True)
    d = x - mu
    var = jnp.mean(d * d, axis=-1, keepdims=True)
    return d * lax.rsqrt(var + LN_EPS) * g + b


def _gelu_tanh(x):
    return 0.5 * x * (1.0 + jnp.tanh(math.sqrt(2.0 / math.pi) * (x + 0.044715 * (x * x * x))))


def _sigmoid(x):
    return 1.0 / (1.0 + jnp.exp(-x))


def _dot(a, b):
    return jnp.dot(a, b, preferred_element_type=F32)


def _dot_nt(a, b):
    return lax.dot_general(a, b, (((1,), (1,)), ((), ())), preferred_element_type=F32)


def _dot_tn(a, b):
    return lax.dot_general(a, b, (((0,), (0,)), ((), ())), preferred_element_type=F32)


def _dot_split(x, w):
    hi = x.astype(BF16)
    lo = (x - hi.astype(F32)).astype(BF16)
    return _dot(hi, w) + _dot(lo, w)


def _rope_table_kernel(pos_ref, freq_ref, sign_ref, cos_ref, sin_ref):
    ang = pos_ref[...].astype(F32) * freq_ref[...]
    cos_ref[...] = jnp.cos(ang)
    sin_ref[...] = jnp.sin(ang) * sign_ref[...]


def rope_tables(positions, tm=1024):
    n = positions.size
    half = DA_QK_DIM // 2
    inv_freq = ROPE_THETA ** (-jnp.arange(0, DA_QK_DIM, 2, dtype=F32) / DA_QK_DIM)
    freq_row = jnp.tile(inv_freq, LANES // half).reshape(1, LANES)
    lane = jnp.arange(LANES)
    sign_row = jnp.where(lane % DA_QK_DIM < half, -1.0, 1.0).astype(F32).reshape(1, LANES)
    pos = positions.reshape(n, 1)
    tm = min(tm, n)
    return pl.pallas_call(
        _rope_table_kernel,
        grid=(n // tm,),
        in_specs=[pl.BlockSpec((tm, 1), lambda i: (i, 0)),
                  _const_spec((1, LANES)), _const_spec((1, LANES))],
        out_specs=[pl.BlockSpec((tm, LANES), lambda i: (i, 0)),
                   pl.BlockSpec((tm, LANES), lambda i: (i, 0))],
        out_shape=[jax.ShapeDtypeStruct((n, LANES), F32)] * 2,
        compiler_params=_params(("parallel",)),
        name="rope_tables",
    )(pos, freq_row, sign_row)


def _rope(x, cos, sin, first_half):
    swapped = jnp.where(first_half, pltpu.roll(x, LANES - DA_QK_DIM // 2, 1),
                        pltpu.roll(x, DA_QK_DIM // 2, 1))
    return x * cos + swapped * sin


def _in_proj_kernel(x_ref, w_ref, cos_ref, sin_ref, da_ref, ret_ref, g_ref, u_ref):
    xb = x_ref[...].astype(BF16)
    cos = cos_ref[...]
    sin = sin_ref[...]
    lane = lax.broadcasted_iota(jnp.int32, cos.shape, 1)
    first_half = (lane % DA_QK_DIM) < (DA_QK_DIM // 2)
    q_scale = DA_QK_DIM ** -0.5 * LOG2E
    k_scale = RET_HEAD_DIM ** -0.5

    def proj(col):
        return _dot(xb, w_ref[:, col:col + MXU_WIDTH])

    def roped(z, scale):
        parts = [_rope(z[:, a:a + LANES], cos, sin, first_half) for a in (0, LANES)]
        out = jnp.concatenate(parts, axis=-1)
        return out if scale is None else out * scale

    for c in range(0, COL_DA_K, MXU_WIDTH):
        da_ref[:, c:c + MXU_WIDTH] = roped(proj(c), q_scale).astype(BF16)
    for c in range(COL_DA_K, COL_DA_V, MXU_WIDTH):
        da_ref[:, c:c + MXU_WIDTH] = roped(proj(c), None).astype(BF16)
    for c in range(COL_DA_V, COL_RET_Q, MXU_WIDTH):
        da_ref[:, c:c + MXU_WIDTH] = proj(c).astype(BF16)
    ret_ref[:, 0:RET_WIDTH] = roped(proj(COL_RET_Q), None).astype(BF16)
    ret_ref[:, RET_WIDTH:2 * RET_WIDTH] = roped(proj(COL_RET_K), k_scale).astype(BF16)
    ret_ref[:, 2 * RET_WIDTH:3 * RET_WIDTH] = proj(COL_RET_V).astype(BF16)
    g_ref[...] = proj(COL_RET_G)
    u = proj(COL_S5_U)
    u_ref[0] = u[:, :S5_HALF]
    u_ref[1] = u[:, S5_HALF:]


def in_proj(x, w_bf16, cos, sin, tm=512):
    n = x.shape[0]
    tm = min(tm, n)
    row = lambda i: (i, 0)
    return pl.pallas_call(
        _in_proj_kernel,
        grid=(n // tm,),
        in_specs=[pl.BlockSpec((tm, D_MODEL), row), _const_spec((D_MODEL, IN_COLS)),
                  pl.BlockSpec((tm, LANES), row), pl.BlockSpec((tm, LANES), row)],
        out_specs=[pl.BlockSpec((tm, COL_RET_Q), row), pl.BlockSpec((tm, 3 * RET_WIDTH), row),
                   pl.BlockSpec((tm, RET_WIDTH), row),
                   pl.BlockSpec((2, tm, S5_HALF), lambda i: (0, i, 0))],
        out_shape=[jax.ShapeDtypeStruct((n, COL_RET_Q), BF16),
                   jax.ShapeDtypeStruct((n, 3 * RET_WIDTH), BF16),
                   jax.ShapeDtypeStruct((n, RET_WIDTH), F32),
                   jax.ShapeDtypeStruct((2, n, S5_HALF), F32)],
        compiler_params=_params(("parallel",)),
        name="in_proj",
    )(x, w_bf16, cos, sin)


ONES_ROWS = 16


def _diff_attn_kernel(q_ref, k_ref, v_ref, lam_ref, g_ref, o_ref,
                      km_ref, vt_ref, m_ref, acc_ref, s_ref, *, tq, tk, out_scale):
    seq = k_ref.shape[0]
    n_kv = seq // tk
    n_q = seq // tq
    k_all = k_ref[...]
    lane = lax.broadcasted_iota(jnp.int32, k_all.shape, 1)
    zero = jnp.zeros_like(k_all)
    km_ref[0] = jnp.where(lane < DA_QK_DIM, k_all, zero)
    km_ref[1] = jnp.where(lane >= DA_QK_DIM, k_all, zero)
    for i in range(n_kv):
        vt_ref[i, :DA_V_DIM, :] = v_ref[i * tk:(i + 1) * tk, :].astype(F32).T.astype(BF16)
        vt_ref[i, DA_V_DIM:, :] = jnp.ones((ONES_ROWS, tk), BF16)

    def scores(qi, ki, slot):
        q = q_ref[pl.ds(pl.multiple_of(qi * tq, tq), tq), :]
        rows = pl.ds(pl.multiple_of(ki * tk, tk), tk)
        for mi in range(2):
            s_ref[slot, mi] = _dot_nt(km_ref[mi, rows, :], q)

    def consume(ki, slot):
        vt = vt_ref[ki]
        for mi in range(2):
            s = s_ref[slot, mi]
            m_old = m_ref[mi]
            m_new = jnp.maximum(m_old, jnp.max(s, axis=0, keepdims=True))
            alpha = jnp.exp2(m_old - m_new)
            p = jnp.exp2(s - m_new).astype(BF16)
            acc_ref[mi] = acc_ref[mi] * alpha + _dot(vt, p)
            m_ref[mi] = m_new

    scores(0, 0, 0)

    def q_tile(qi, carry):
        m_ref[...] = jnp.full(m_ref.shape, NEG_BIG, F32)
        acc_ref[...] = jnp.zeros(acc_ref.shape, F32)

        def kv_pair(j, c):
            for u in range(2):
                ki = 2 * j + u
                wrap = ki + 1 >= n_kv
                scores(jnp.where(wrap, jnp.minimum(qi + 1, n_q - 1), qi),
                       jnp.where(wrap, 0, ki + 1), 1 - u)
                consume(ki, u)
            return c

        lax.fori_loop(0, n_kv // 2, kv_pair, 0)
        a0 = acc_ref[0]
        a1 = acc_ref[1]
        o = (a0[:DA_V_DIM] / a0[DA_V_DIM:DA_V_DIM + 1]
             - lam_ref[...] * (a1[:DA_V_DIM] / a1[DA_V_DIM:DA_V_DIM + 1]))
        ms = jnp.mean(o * o, axis=0, keepdims=True)
        o = o * lax.rsqrt(ms + RMS_EPS) * g_ref[...] * out_scale
        rows = pl.ds(pl.multiple_of(qi * tq, tq), tq)
        o_ref[rows, :] = o.T.astype(o_ref.dtype)
        return carry

    lax.fori_loop(0, n_q, q_tile, 0)


def diff_attention(da, lam, subln_col, out_scale, bsz, seq, tq=256, tk=4096):
    tq = min(tq, seq)
    tk = min(tk, seq // 2)
    assert seq % (2 * tk) == 0 and seq % tq == 0
    kern = functools.partial(_diff_attn_kernel, tq=tq, tk=tk, out_scale=out_scale)
    head_block = lambda part: pl.BlockSpec((seq, LANES), lambda b, h: (b, part * DA_HEADS + h))
    vrows = DA_V_DIM + ONES_ROWS
    return pl.pallas_call(
        kern,
        grid=(bsz, DA_HEADS),
        in_specs=[head_block(0), head_block(1), head_block(2),
                  _const_spec((1, 1)), _const_spec((DA_V_DIM, 1))],
        out_specs=pl.BlockSpec((seq, DA_V_DIM), lambda b, h: (b, h)),
        out_shape=jax.ShapeDtypeStruct((bsz * seq, DA_WIDTH), BF16),
        scratch_shapes=[pltpu.VMEM((2, seq, LANES), BF16),
                        pltpu.VMEM((seq // tk, vrows, tk), BF16),
                        pltpu.VMEM((2, 1, tq), F32),
                        pltpu.VMEM((2, vrows, tq), F32),
                        pltpu.VMEM((2, 2, tk, tq), F32)],
        compiler_params=_params(("parallel", "parallel")),
        name="diff_attention",
    )(da, da, da, lam, subln_col)


def _ret_state_kernel(kf_ref, vf_ref, kb_ref, vb_ref, dkf_ref, dkb_ref, gc_ref,
                      sf_out, sb_out, sf_ref, sb_ref):
    @pl.when(pl.program_id(1) == 0)
    def _():
        sf_ref[...] = jnp.zeros(sf_ref.shape, F32)
        sb_ref[...] = jnp.zeros(sb_ref.shape, F32)

    c = dkf_ref.shape[0]
    per_step = sf_out.shape[0]

    def update(s_ref, k, v, dk_ref):
        kd = (k.astype(F32) * dk_ref[...]).astype(BF16)
        for pr in range(RET_WIDTH // LANES):
            sl = slice(pr * LANES, (pr + 1) * LANES)
            kv = _dot_tn(kd[:, sl], v[:, sl])
            keep = gc_ref[sl, :]
            s_ref[sl, :] = keep * s_ref[sl, :] + jnp.where(keep > 0.0, kv, 0.0)

    for u in range(per_step):
        lo = slice(u * c, (u + 1) * c)
        hi = slice((per_step - 1 - u) * c, (per_step - u) * c)
        sf_out[u] = sf_ref[...]
        sb_out[per_step - 1 - u] = sb_ref[...]
        update(sf_ref, kf_ref[lo, :], vf_ref[lo, :], dkf_ref)
        update(sb_ref, kb_ref[hi, :], vb_ref[hi, :], dkb_ref)


def _ret_out_kernel(q_ref, k_ref, v_ref, g_ref, sf_ref, sb_ref, dec_ref, dqf_ref, dqb_ref,
                    avg_ref, gng_ref, gnb_ref, o_ref):
    c = dqf_ref.shape[0]
    lane = lax.broadcasted_iota(jnp.int32, (c, LANES), 1)
    zero = jnp.zeros((c, LANES), BF16)
    avg = avg_ref[...]
    for u in range(sf_ref.shape[0]):
        rows = slice(u * c, (u + 1) * c)
        q = q_ref[rows, :]
        k = k_ref[rows, :]
        v = v_ref[rows, :]
        qf = q.astype(F32)
        parts = []
        for pr in range(RET_WIDTH // LANES):
            sl = slice(pr * LANES, (pr + 1) * LANES)
            qp, kp, vp = q[:, sl], k[:, sl], v[:, sl]
            acc = _dot((qf[:, sl] * dqf_ref[:, sl]).astype(BF16), sf_ref[u, sl, :].astype(BF16))
            acc += _dot((qf[:, sl] * dqb_ref[:, sl]).astype(BF16), sb_ref[u, sl, :].astype(BF16))
            for hh in range(LANES // RET_HEAD_DIM):
                mine = (lane >= hh * RET_HEAD_DIM) & (lane < (hh + 1) * RET_HEAD_DIM)
                s = _dot_nt(jnp.where(mine, qp, zero), kp) * dec_ref[pr * 2 + hh]
                acc += _dot(s.astype(BF16), jnp.where(mine, vp, zero))
            parts.append(acc)
        o = jnp.concatenate(parts, axis=-1)
        mu = _dot_split(o, avg)
        d = o - mu
        var = _dot_split(d * d, avg)
        y = d * lax.rsqrt(var + LN_EPS) * gng_ref[...] + gnb_ref[...]
        g = g_ref[rows, :]
        o_ref[rows, :] = (g * _sigmoid(g) * y).astype(o_ref.dtype)


def retention(ret, g, gn_g, gn_b, bsz, seq):
    c = min(RET_CHUNK, seq)
    nc = seq // c
    heads = jnp.arange(RET_HEADS, dtype=F32)
    log_gamma = jnp.log(1.0 - 2.0 ** (-5.0 - heads))
    lg_cols = jnp.repeat(log_gamma, RET_HEAD_DIM)[None, :]
    idx = jnp.arange(c, dtype=F32)[:, None]
    dk_f = jnp.exp(lg_cols * (c - 1 - idx))
    dk_b = jnp.exp(lg_cols * idx)
    dq_f = jnp.exp(lg_cols * (idx + 1))
    dq_b = jnp.exp(lg_cols * (c - idx))
    dist = jnp.abs(idx - idx.T)
    decay = jnp.exp(log_gamma[:, None, None] * dist[None])
    row_head = jnp.arange(RET_WIDTH)[:, None] // RET_HEAD_DIM
    col_head = (jnp.arange(LANES)[None, :] // RET_HEAD_DIM) + 2 * (jnp.arange(RET_WIDTH)[:, None] // LANES)
    gc = jnp.where(row_head == col_head, jnp.exp(lg_cols.T * c), 0.0).astype(F32)
    seg = jnp.arange(RET_WIDTH) // RET_HEAD_DIM
    avg = jnp.where(seg[:, None] == seg[None, :], 1.0 / RET_HEAD_DIM, 0.0).astype(BF16)
    gng = jnp.tile(gn_g, RET_HEADS)[None, :]
    gnb = jnp.tile(gn_b, RET_HEADS)[None, :]

    per_step = RET_CHUNKS_PER_STEP if nc % RET_CHUNKS_PER_STEP == 0 else 1
    ns = nc // per_step
    rows = per_step * c
    blk = lambda col: pl.BlockSpec((rows, RET_WIDTH), lambda b, j: (b * ns + j, col))
    blk_rev = lambda col: pl.BlockSpec((rows, RET_WIDTH), lambda b, j: (b * ns + ns - 1 - j, col))
    st_shape = jax.ShapeDtypeStruct((bsz, nc, RET_WIDTH, LANES), F32)
    st_blk = (None, per_step, RET_WIDTH, LANES)
    sf, sb = pl.pallas_call(
        _ret_state_kernel,
        grid=(bsz, ns),
        in_specs=[blk(1), blk(2), blk_rev(1), blk_rev(2),
                  _const_spec((c, RET_WIDTH)), _const_spec((c, RET_WIDTH)),
                  _const_spec((RET_WIDTH, LANES))],
        out_specs=[pl.BlockSpec(st_blk, lambda b, j: (b, j, 0, 0)),
                   pl.BlockSpec(st_blk, lambda b, j: (b, ns - 1 - j, 0, 0))],
        out_shape=[st_shape, st_shape],
        scratch_shapes=[pltpu.VMEM((RET_WIDTH, LANES), F32), pltpu.VMEM((RET_WIDTH, LANES), F32)],
        compiler_params=_params(("parallel", "arbitrary")),
        name="retention_state",
    )(ret, ret, ret, ret, dk_f, dk_b, gc)

    st_spec = pl.BlockSpec(st_blk, lambda b, j: (b, j, 0, 0))
    return pl.pallas_call(
        _ret_out_kernel,
        grid=(bsz, ns),
        in_specs=[blk(0), blk(1), blk(2), blk(0), st_spec, st_spec,
                  _const_spec((RET_HEADS, c, c)), _const_spec((c, RET_WIDTH)),
                  _const_spec((c, RET_WIDTH)), _const_spec((RET_WIDTH, RET_WIDTH)),
                  _const_spec((1, RET_WIDTH)), _const_spec((1, RET_WIDTH))],
        out_specs=blk(0),
        out_shape=jax.ShapeDtypeStruct((bsz * seq, RET_WIDTH), BF16),
        compiler_params=_params(("parallel", "parallel")),
        name="retention_out",
    )(ret, ret, ret, g, sf, sb, decay, dq_f, dq_b, avg, gng, gnb)


def _cmul(ar, ai, br, bi):
    return ar * br - ai * bi, ar * bi + ai * br


def s5_matrices(A_re, A_im, log_dt, B_re, B_im, C_re, C_im, D):
    T, G, P, Cn = S5_CHUNK, S5_GROUPS, S5_STATE, S5_GROUP
    gh = S5_HALF // Cn
    depth = A_re.shape[0]
    step = jnp.exp(log_dt.astype(F32))[..., None]
    a_re = A_re.astype(F32)
    a_im = A_im.astype(F32)
    d = jnp.arange(T + 1, dtype=F32).reshape(T + 1, 1, 1, 1, 1)
    mag = jnp.exp(d * (step * a_re))
    pw_re = mag * jnp.cos(d * (step * a_im))
    pw_im = mag * jnp.sin(d * (step * a_im))
    den = a_re * a_re + a_im * a_im
    nr = pw_re[1] - 1.0
    ni = pw_im[1]
    coef_re = ((nr * a_re + ni * a_im) / den)[..., None]
    coef_im = ((ni * a_re - nr * a_im) / den)[..., None]
    b_re = B_re.astype(F32)
    b_im = B_im.astype(F32)
    bb_re = coef_re * b_re - coef_im * b_im
    bb_im = coef_re * b_im + coef_im * b_re
    c_re = C_re.astype(F32)
    c_im = C_im.astype(F32)
    eye = jnp.eye(gh, dtype=F32)
    t_idx = jnp.arange(T)
    fwd_in, bwd_in = T - 1 - t_idx, t_idx
    fwd_out, bwd_out = t_idx + 1, T - t_idx

    def state_in(direction, order):
        return _cmul(pw_re[order, :, direction, :, None, :], pw_im[order, :, direction, :, None, :],
                     jnp.swapaxes(bb_re[:, direction], -1, -2)[None],
                     jnp.swapaxes(bb_im[:, direction], -1, -2)[None])

    def read_out(direction, order):
        return _cmul(c_re[None, :, direction], c_im[None, :, direction],
                     pw_re[order, :, direction, :, None, :], pw_im[order, :, direction, :, None, :])

    def lag_kernel(direction):
        wr, wi = _cmul(pw_re[:T, :, direction, :, :, None], pw_im[:T, :, direction, :, :, None],
                       bb_re[None, :, direction], bb_im[None, :, direction])
        return (jnp.einsum('lgop,dlgpi->dlgoi', c_re[:, direction], wr)
                - jnp.einsum('lgop,dlgpi->dlgoi', c_im[:, direction], wi))

    def halves(v, lead):
        return v.reshape(v.shape[:lead] + (2, gh) + v.shape[lead + 1:])

    vfr, vfi = state_in(0, fwd_in)
    vbr, vbi = state_in(1, bwd_in)
    eye = eye.astype(BF16)
    v_all = halves(jnp.stack([vfr, vfi, vbr, vbi]).astype(BF16), 3)
    mb = jnp.einsum('ksldgcp,gj->ldsgckjp', v_all, eye).reshape(depth, 2, T * gh * Cn, 4 * gh * P)

    efr, efi = read_out(0, fwd_out)
    ebr, ebi = read_out(1, bwd_out)
    e_all = halves(jnp.stack([efr, -efi, ebr, -ebi]).astype(BF16), 3)
    mc = jnp.einsum('ktldgop,gj->ldkgptjo', e_all, eye).reshape(depth, 2, 4 * gh * P, T * gh * Cn)

    lag = t_idx[None, :] - t_idx[:, None]
    sel = lambda cond: cond[:, :, None, None, None, None]
    d_diag = D.astype(F32).reshape(depth, G, Cn)[..., None] * jnp.eye(Cn, dtype=F32)
    toe = (jnp.where(sel(lag >= 0), lag_kernel(0)[jnp.clip(lag, 0, T - 1)], 0.0)
           + jnp.where(sel(lag <= 0), lag_kernel(1)[jnp.clip(-lag, 0, T - 1)], 0.0)
           + jnp.where(sel(lag == 0), d_diag[None, None], 0.0))
    tp = jnp.einsum('stldgoi,gj->ldsgitjo', halves(toe.astype(BF16), 3), eye).reshape(
        depth, 2, T * gh * Cn, T * gh * Cn)

    a_rows = jnp.stack([pw_re[T, :, 0], pw_im[T, :, 0], pw_re[T, :, 1], pw_im[T, :, 1]], axis=1)
    a8 = jnp.swapaxes(a_rows.reshape(depth, 4, 2, gh * P), 1, 2)
    a8 = jnp.concatenate([a8, a8], axis=2)
    return mb, tp, mc, a8


def _s5_kernel(u_ref, mb_ref, tp_ref, mc_ref, a8_ref, y_ref, u8_ref, w_ref, *, sub):
    rows = w_ref.shape[0]
    ns = a8_ref.shape[1]
    for r in range(0, rows, sub):
        steps = [u_ref[pl.ds(r * S5_CHUNK + s, sub, stride=S5_CHUNK), :].astype(BF16)
                 for s in range(S5_CHUNK)]
        u8 = jnp.concatenate(steps, axis=-1)
        u8_ref[r:r + sub, :] = u8
        w_ref[r:r + sub, :] = _dot(u8, mb_ref[...])

    afr, afi = a8_ref[0:1, :], a8_ref[1:2, :]
    abr, abi = a8_ref[2:3, :], a8_ref[3:4, :]

    def scan_step(j, carry):
        xfr, xfi, xbr, xbi = carry
        jb = rows - 1 - j
        wf_r = w_ref[pl.ds(j, 1), 0:ns]
        wf_i = w_ref[pl.ds(j, 1), ns:2 * ns]
        wb_r = w_ref[pl.ds(jb, 1), 2 * ns:3 * ns]
        wb_i = w_ref[pl.ds(jb, 1), 3 * ns:4 * ns]
        w_ref[pl.ds(j, 1), 0:ns] = xfr
        w_ref[pl.ds(j, 1), ns:2 * ns] = xfi
        w_ref[pl.ds(jb, 1), 2 * ns:3 * ns] = xbr
        w_ref[pl.ds(jb, 1), 3 * ns:4 * ns] = xbi
        nfr = afr * xfr - afi * xfi + wf_r
        nfi = afr * xfi + afi * xfr + wf_i
        nbr = abr * xbr - abi * xbi + wb_r
        nbi = abr * xbi + abi * xbr + wb_i
        return nfr, nfi, nbr, nbi

    z = jnp.zeros((1, ns), F32)
    lax.fori_loop(0, rows, scan_step, (z, z, z, z))

    for r in range(0, rows, sub):
        y8 = (_dot(u8_ref[r:r + sub, :], tp_ref[...])
              + _dot(w_ref[r:r + sub, :].astype(BF16), mc_ref[...]))
        for t in range(S5_CHUNK):
            y_ref[pl.ds(r * S5_CHUNK + t, sub, stride=S5_CHUNK), :] = y8[:, t * S5_HALF:(t + 1) * S5_HALF]


def s5_mixer(u, mats, layer, bsz, seq):
    mb, tp, mc, a8 = mats
    n = bsz * seq
    rows = seq // S5_CHUNK
    width = S5_CHUNK * S5_HALF
    ns = (S5_HALF // S5_GROUP) * S5_STATE
    kern = functools.partial(_s5_kernel, sub=min(256, rows))
    wspec = lambda a: pl.BlockSpec((None, None) + a.shape[2:], lambda h, b: (layer, h, 0, 0),
                                   pipeline_mode=pl.Buffered(1))
    tokens = pl.BlockSpec((None, seq, S5_HALF), lambda h, b: (h, b, 0))
    return pl.pallas_call(
        kern,
        grid=(2, bsz),
        in_specs=[tokens, wspec(mb), wspec(tp), wspec(mc), wspec(a8)],
        out_specs=tokens,
        out_shape=jax.ShapeDtypeStruct((2, n, S5_HALF), F32),
        scratch_shapes=[pltpu.VMEM((rows, width), BF16), pltpu.VMEM((rows, 4 * ns), F32)],
        compiler_params=_params(("arbitrary", "arbitrary")),
        name="s5_mixer",
    )(u, mb, tp, mc, a8)


def _out_proj_kernel(x_ref, da_ref, ret_ref, y5_ref, gluw_ref, glub_ref, wout_ref,
                     g_ref, b_ref, o_ref, *, alpha):
    y = jnp.concatenate([y5_ref[0], y5_ref[1]], axis=-1)
    ya = _gelu_tanh(y)
    gate = _sigmoid(_dot(ya.astype(BF16), gluw_ref[...]) + glub_ref[...])
    ys5 = (ya * gate).astype(BF16)
    c1 = DA_WIDTH
    c2 = DA_WIDTH + RET_WIDTH
    mix = (_dot(da_ref[...], wout_ref[0:c1, :]) + _dot(ret_ref[...], wout_ref[c1:c2, :])
           + _dot(ys5, wout_ref[c2:, :]))
    o_ref[...] = _layer_norm(alpha * x_ref[...] + mix, g_ref[...], b_ref[...])


def out_proj(x, y_da, y_ret, y5, glu_w, glu_b, w_out, ln_g, ln_b, alpha, tm=512):
    n = x.shape[0]
    tm = min(tm, n)
    row = lambda i: (i, 0)
    return pl.pallas_call(
        functools.partial(_out_proj_kernel, alpha=alpha),
        grid=(n // tm,),
        in_specs=[pl.BlockSpec((tm, D_MODEL), row), pl.BlockSpec((tm, DA_WIDTH), row),
                  pl.BlockSpec((tm, RET_WIDTH), row),
                  pl.BlockSpec((2, tm, S5_HALF), lambda i: (0, i, 0)),
                  _const_spec((S5_WIDTH, S5_WIDTH)), _const_spec((1, S5_WIDTH)),
                  _const_spec((D_MODEL, D_MODEL)),
                  _const_spec((1, D_MODEL)), _const_spec((1, D_MODEL))],
        out_specs=pl.BlockSpec((tm, D_MODEL), row),
        out_shape=jax.ShapeDtypeStruct((n, D_MODEL), F32),
        compiler_params=_params(("parallel",)),
        name="out_proj",
    )(x, y_da, y_ret, y5, glu_w, glu_b, w_out, ln_g, ln_b)


def _ffn_kernel(x_ref, xp_ref, xn_ref, p_ref, wup_ref, cw_ref, cb_ref, wdn_ref,
                plew_ref, gatew_ref, g_ref, b_ref, o_ref, act_ref, *, alpha, tiles_per_seq):
    tm = x_ref.shape[0]
    i = pl.program_id(0)
    has_prev = ((i % tiles_per_seq) != 0).astype(F32)
    has_next = ((i % tiles_per_seq) != tiles_per_seq - 1).astype(F32)
    x = x_ref[...]
    xb = x.astype(BF16)
    xpb = xp_ref[...].astype(BF16)
    xnb = xn_ref[...].astype(BF16)
    halo = xp_ref.shape[0]
    row = lax.broadcasted_iota(jnp.int32, (halo, FF_CHUNK), 0)

    for c in range(0, D_FF, FF_CHUNK):
        wg = wup_ref[:, c:c + FF_CHUNK]
        gate = _dot(xb, wg)
        val = _dot(xb, wup_ref[:, D_FF + c:D_FF + c + FF_CHUNK])
        before = _dot(xpb, wg)[halo - 1:halo, :] * has_prev
        after = _dot(xnb, wg)[0:1, :] * has_next
        left = pltpu.roll(gate, 1, 0)
        left = jnp.concatenate([jnp.where(row == 0, before, left[:halo]), left[halo:]], axis=0)
        right = pltpu.roll(gate, tm - 1, 0)
        right = jnp.concatenate(
            [right[:tm - halo], jnp.where(row == halo - 1, after, right[tm - halo:])], axis=0)
        conv = (cw_ref[0:1, c:c + FF_CHUNK] * left + cw_ref[1:2, c:c + FF_CHUNK] * gate
                + cw_ref[2:3, c:c + FF_CHUNK] * right + cb_ref[:, c:c + FF_CHUNK])
        act_ref[:, c:c + FF_CHUNK] = (_gelu_tanh(conv) * val).astype(BF16)

    for r in range(0, tm, EPILOGUE_ROWS):
        rs = slice(r, r + EPILOGUE_ROWS)
        f = _dot(act_ref[rs, :], wdn_ref[...])
        ple = (_dot(p_ref[rs, :].astype(BF16), plew_ref[...])
               * _sigmoid(_dot(xb[rs], gatew_ref[...])))
        o_ref[rs, :] = _layer_norm(alpha * x[rs] + f + ple, g_ref[...], b_ref[...])


def conv_ffn_ple(x, p, layer, w_up, conv_w, conv_b, w_down, ple_w, gate_w, ln_g, ln_b, alpha, seq, tm=512):
    n = x.shape[0]
    tm = min(tm, seq)
    p_base = layer * (n // tm)
    halo = 8
    tiles_per_seq = seq // tm
    per = tm // halo
    last = n // halo - 1
    row = lambda i: (i, 0)
    kern = functools.partial(_ffn_kernel, alpha=alpha, tiles_per_seq=tiles_per_seq)
    return pl.pallas_call(
        kern,
        grid=(n // tm,),
        in_specs=[pl.BlockSpec((tm, D_MODEL), row),
                  pl.BlockSpec((halo, D_MODEL), lambda i: (jnp.maximum(i * per - 1, 0), 0)),
                  pl.BlockSpec((halo, D_MODEL), lambda i: (jnp.minimum((i + 1) * per, last), 0)),
                  pl.BlockSpec((tm, PLE_DIM), lambda i: (p_base + i, 0)),
                  _const_spec((D_MODEL, 2 * D_FF)), _const_spec((3, D_FF)), _const_spec((1, D_FF)),
                  _const_spec((D_FF, D_MODEL)), _const_spec((PLE_DIM, D_MODEL)),
                  _const_spec((D_MODEL, D_MODEL)),
                  _const_spec((1, D_MODEL)), _const_spec((1, D_MODEL))],
        out_specs=pl.BlockSpec((tm, D_MODEL), row),
        out_shape=jax.ShapeDtypeStruct((n, D_MODEL), F32),
        scratch_shapes=[pltpu.VMEM((tm, D_FF), BF16)],
        compiler_params=_params(("parallel",)),
        name="conv_ffn_ple",
    )(x, x, x, p, w_up, conv_w, conv_b, w_down, ple_w, gate_w, ln_g, ln_b)


def kernel(x, p, positions, w_in, da_lambda_q1, da_lambda_k1, da_lambda_q2, da_lambda_k2,
           da_subln_g, ret_gn_g, ret_gn_b, s5_A_re, s5_A_im, s5_log_dt, s5_B_re, s5_B_im,
           s5_C_re, s5_C_im, s5_D, s5_glu_w, s5_glu_b, w_out, ln1_g, ln1_b,
           ffn_w_up, ffn_conv_w, ffn_conv_b, ffn_w_down, ple_w, ple_gate_w, ln2_g, ln2_b):
    bsz, seq, _ = x.shape
    depth = w_in.shape[0]
    n = bsz * seq
    alpha = (2 * depth) ** 0.25
    cos, sin = rope_tables(positions)
    xf = x.reshape(n, D_MODEL)
    p_all = p.reshape(depth * n, PLE_DIM)
    mats = s5_matrices(s5_A_re, s5_A_im, s5_log_dt, s5_B_re, s5_B_im, s5_C_re, s5_C_im, s5_D)
    row = lambda v: v.reshape(1, -1).astype(F32)
    for i in range(depth):
        lambda_init = 0.8 - 0.6 * math.exp(-0.3 * i)
        lam = (jnp.exp(jnp.sum(da_lambda_q1[i].astype(F32) * da_lambda_k1[i].astype(F32)))
               - jnp.exp(jnp.sum(da_lambda_q2[i].astype(F32) * da_lambda_k2[i].astype(F32)))
               + lambda_init)
        da, ret, g, u = in_proj(xf, w_in[i].astype(BF16), cos, sin)
        y_da = diff_attention(da, lam.reshape(1, 1), da_subln_g[i].astype(F32).reshape(-1, 1),
                              1.0 - lambda_init, bsz, seq)
        y_ret = retention(ret, g, ret_gn_g[i].astype(F32), ret_gn_b[i].astype(F32), bsz, seq)
        y5 = s5_mixer(u, mats, i, bsz, seq)
        x1 = out_proj(xf, y_da, y_ret, y5,
                      s5_glu_w[i].astype(BF16), row(s5_glu_b[i]), w_out[i].astype(BF16),
                      row(ln1_g[i]), row(ln1_b[i]), alpha)
        xf = conv_ffn_ple(x1, p_all, i, ffn_w_up[i].astype(BF16),
                          ffn_conv_w[i].astype(F32), row(ffn_conv_b[i]),
                          ffn_w_down[i].astype(BF16), ple_w[i].astype(BF16),
                          ple_gate_w[i].astype(BF16), row(ln2_g[i]), row(ln2_b[i]), alpha, seq)
    return xf.reshape(bsz, seq, D_MODEL)
```

```python
import functools
import math

import jax
import jax.numpy as jnp
from jax import lax
from jax.experimental import pallas as pl
from jax.experimental.pallas import tpu as pltpu

F32 = jnp.float32
BF16 = jnp.bfloat16

D_MODEL = 1024
PLE_DIM = 256
DA_HEADS = 4
DA_QK_DIM = 64
DA_V_DIM = 128
DA_WIDTH = DA_HEADS * DA_V_DIM
RET_HEADS = 4
RET_HEAD_DIM = 64
RET_WIDTH = RET_HEADS * RET_HEAD_DIM
S5_WIDTH = 256
S5_GROUP = 16
S5_GROUPS = S5_WIDTH // S5_GROUP
S5_STATE = 64
D_FF = 2816
ROPE_THETA = 10000.0
LN_EPS = 1e-5
RMS_EPS = 1e-6

COL_DA_Q = 0
COL_DA_K = COL_DA_Q + DA_HEADS * 2 * DA_QK_DIM
COL_DA_V = COL_DA_K + DA_HEADS * 2 * DA_QK_DIM
COL_RET_Q = COL_DA_V + DA_WIDTH
COL_RET_K = COL_RET_Q + RET_WIDTH
COL_RET_V = COL_RET_K + RET_WIDTH
COL_RET_G = COL_RET_V + RET_WIDTH
COL_S5_U = COL_RET_G + RET_WIDTH
IN_COLS = COL_S5_U + S5_WIDTH

LANES = 128
MXU_WIDTH = 256
S5_CHUNK = 8
S5_HALF = 128
RET_CHUNK = 256
RET_CHUNKS_PER_STEP = 4
FF_CHUNK = 256
ROW_TILE = 1024
EPILOGUE_ROWS = 256
LOG2E = 1.4426950408889634
NEG_BIG = -1e30
VMEM_LIMIT = 56 * 1024 * 1024


def _params(sem, vmem=VMEM_LIMIT):
    return pltpu.CompilerParams(dimension_semantics=sem, vmem_limit_bytes=vmem)


def _const_spec(shape):
    nd = len(shape)
    return pl.BlockSpec(shape, lambda *_: (0,) * nd, pipeline_mode=pl.Buffered(1))


def _layer_norm(x, g, b):
    mu = jnp.mean(x, axis=-1, keepdims=True)
    d = x - mu
    var = jnp.mean(d * d, axis=-1, keepdims=True)
    return d * lax.rsqrt(var + LN_EPS) * g + b


def _gelu_tanh(x):
    return 0.5 * x * (1.0 + jnp.tanh(math.sqrt(2.0 / math.pi) * (x + 0.044715 * (x * x * x))))


def _sigmoid(x):
    return 1.0 / (1.0 + jnp.exp(-x))


def _dot(a, b):
    return jnp.dot(a, b, preferred_element_type=F32)


def _dot_nt(a, b):
    return lax.dot_general(a, b, (((1,), (1,)), ((), ())), preferred_element_type=F32)


def _dot_tn(a, b):
    return lax.dot_general(a, b, (((0,), (0,)), ((), ())), preferred_element_type=F32)


def _dot_split(x, w):
    hi = x.astype(BF16)
    lo = (x - hi.astype(F32)).astype(BF16)
    return _dot(hi, w) + _dot(lo, w)


def _rope_table_kernel(pos_ref, freq_ref, sign_ref, cos_ref, sin_ref):
    ang = pos_ref[...].astype(F32) * freq_ref[...]
    cos_ref[...] = jnp.cos(ang)
    sin_ref[...] = jnp.sin(ang) * sign_ref[...]


def rope_tables(positions, tm=1024):
    n = positions.size
    half = DA_QK_DIM // 2
    inv_freq = ROPE_THETA ** (-jnp.arange(0, DA_QK_DIM, 2, dtype=F32) / DA_QK_DIM)
    freq_row = jnp.tile(inv_freq, LANES // half).reshape(1, LANES)
    lane = jnp.arange(LANES)
    sign_row = jnp.where(lane % DA_QK_DIM < half, -1.0, 1.0).astype(F32).reshape(1, LANES)
    pos = positions.reshape(n, 1)
    tm = min(tm, n)
    return pl.pallas_call(
        _rope_table_kernel,
        grid=(n // tm,),
        in_specs=[pl.BlockSpec((tm, 1), lambda i: (i, 0)),
                  _const_spec((1, LANES)), _const_spec((1, LANES))],
        out_specs=[pl.BlockSpec((tm, LANES), lambda i: (i, 0)),
                   pl.BlockSpec((tm, LANES), lambda i: (i, 0))],
        out_shape=[jax.ShapeDtypeStruct((n, LANES), F32)] * 2,
        compiler_params=_params(("parallel",)),
        name="rope_tables",
    )(pos, freq_row, sign_row)


def _rope(x, cos, sin, first_half):
    swapped = jnp.where(first_half, pltpu.roll(x, LANES - DA_QK_DIM // 2, 1),
                        pltpu.roll(x, DA_QK_DIM // 2, 1))
    return x * cos + swapped * sin


def _in_proj_kernel(x_ref, w_ref, cos_ref, sin_ref, da_ref, ret_ref, g_ref, u_ref):
    xb = x_ref[...].astype(BF16)
    cos = cos_ref[...]
    sin = sin_ref[...]
    lane = lax.broadcasted_iota(jnp.int32, cos.shape, 1)
    first_half = (lane % DA_QK_DIM) < (DA_QK_DIM // 2)
    q_scale = DA_QK_DIM ** -0.5 * LOG2E
    k_scale = RET_HEAD_DIM ** -0.5

    def proj(col):
        return _dot(xb, w_ref[:, col:col + MXU_WIDTH])

    def roped(z, scale):
        parts = [_rope(z[:, a:a + LANES], cos, sin, first_half) for a in (0, LANES)]
        out = jnp.concatenate(parts, axis=-1)
        return out if scale is None else out * scale

    for c in range(0, COL_DA_K, MXU_WIDTH):
        da_ref[:, c:c + MXU_WIDTH] = roped(proj(c), q_scale).astype(BF16)
    for c in range(COL_DA_K, COL_DA_V, MXU_WIDTH):
        da_ref[:, c:c + MXU_WIDTH] = roped(proj(c), None).astype(BF16)
    for c in range(COL_DA_V, COL_RET_Q, MXU_WIDTH):
        da_ref[:, c:c + MXU_WIDTH] = proj(c).astype(BF16)
    ret_ref[:, 0:RET_WIDTH] = roped(proj(COL_RET_Q), None).astype(BF16)
    ret_ref[:, RET_WIDTH:2 * RET_WIDTH] = roped(proj(COL_RET_K), k_scale).astype(BF16)
    ret_ref[:, 2 * RET_WIDTH:3 * RET_WIDTH] = proj(COL_RET_V).astype(BF16)
    g_ref[...] = proj(COL_RET_G)
    u = proj(COL_S5_U)
    u_ref[0] = u[:, :S5_HALF]
    u_ref[1] = u[:, S5_HALF:]


def in_proj(x, w_bf16, cos, sin, tm=ROW_TILE):
    n = x.shape[0]
    tm = min(tm, n)
    row = lambda i: (i, 0)
    return pl.pallas_call(
        _in_proj_kernel,
        grid=(n // tm,),
        in_specs=[pl.BlockSpec((tm, D_MODEL), row), _const_spec((D_MODEL, IN_COLS)),
                  pl.BlockSpec((tm, LANES), row), pl.BlockSpec((tm, LANES), row)],
        out_specs=[pl.BlockSpec((tm, COL_RET_Q), row), pl.BlockSpec((tm, 3 * RET_WIDTH), row),
                   pl.BlockSpec((tm, RET_WIDTH), row),
                   pl.BlockSpec((2, tm, S5_HALF), lambda i: (0, i, 0))],
        out_shape=[jax.ShapeDtypeStruct((n, COL_RET_Q), BF16),
                   jax.ShapeDtypeStruct((n, 3 * RET_WIDTH), BF16),
                   jax.ShapeDtypeStruct((n, RET_WIDTH), F32),
                   jax.ShapeDtypeStruct((2, n, S5_HALF), F32)],
        compiler_params=_params(("parallel",)),
        name="in_proj",
    )(x, w_bf16, cos, sin)


ONES_ROWS = 16


def _diff_attn_kernel(q_ref, k_ref, v_ref, lam_ref, g_ref, o_ref,
                      km_ref, vt_ref, m_ref, acc_ref, s_ref, *, tq, tk, out_scale):
    seq = k_ref.shape[0]
    n_kv = seq // tk
    n_q = seq // tq
    k_all = k_ref[...]
    lane = lax.broadcasted_iota(jnp.int32, k_all.shape, 1)
    zero = jnp.zeros_like(k_all)
    km_ref[0] = jnp.where(lane < DA_QK_DIM, k_all, zero)
    km_ref[1] = jnp.where(lane >= DA_QK_DIM, k_all, zero)
    for i in range(n_kv):
        vt_ref[i, :DA_V_DIM, :] = v_ref[i * tk:(i + 1) * tk, :].astype(F32).T.astype(BF16)
        vt_ref[i, DA_V_DIM:, :] = jnp.ones((ONES_ROWS, tk), BF16)

    def scores(qi, ki, slot):
        q = q_ref[pl.ds(pl.multiple_of(qi * tq, tq), tq), :]
        rows = pl.ds(pl.multiple_of(ki * tk, tk), tk)
        for mi in range(2):
            s_ref[slot, mi] = _dot_nt(km_ref[mi, rows, :], q)

    def consume(ki, slot):
        vt = vt_ref[ki]
        for mi in range(2):
            s = s_ref[slot, mi]
            m_old = m_ref[mi]
            m_new = jnp.maximum(m_old, jnp.max(s, axis=0, keepdims=True))
            alpha = jnp.exp2(m_old - m_new)
            p = jnp.exp2(s - m_new).astype(BF16)
            acc_ref[mi] = acc_ref[mi] * alpha + _dot(vt, p)
            m_ref[mi] = m_new

    scores(0, 0, 0)

    def q_tile(qi, carry):
        m_ref[...] = jnp.full(m_ref.shape, NEG_BIG, F32)
        acc_ref[...] = jnp.zeros(acc_ref.shape, F32)

        def kv_pair(j, c):
            for u in range(2):
                ki = 2 * j + u
                wrap = ki + 1 >= n_kv
                scores(jnp.where(wrap, jnp.minimum(qi + 1, n_q - 1), qi),
                       jnp.where(wrap, 0, ki + 1), 1 - u)
                consume(ki, u)
            return c

        lax.fori_loop(0, n_kv // 2, kv_pair, 0)
        a0 = acc_ref[0]
        a1 = acc_ref[1]
        o = (a0[:DA_V_DIM] / a0[DA_V_DIM:DA_V_DIM + 1]
             - lam_ref[...] * (a1[:DA_V_DIM] / a1[DA_V_DIM:DA_V_DIM + 1]))
        ms = jnp.mean(o * o, axis=0, keepdims=True)
        o = o * lax.rsqrt(ms + RMS_EPS) * g_ref[...] * out_scale
        rows = pl.ds(pl.multiple_of(qi * tq, tq), tq)
        o_ref[rows, :] = o.T.astype(o_ref.dtype)
        return carry

    lax.fori_loop(0, n_q, q_tile, 0)


def diff_attention(da, lam, subln_col, out_scale, bsz, seq, tq=256, tk=4096):
    tq = min(tq, seq)
    tk = min(tk, seq // 2)
    assert seq % (2 * tk) == 0 and seq % tq == 0
    kern = functools.partial(_diff_attn_kernel, tq=tq, tk=tk, out_scale=out_scale)
    head_block = lambda part: pl.BlockSpec((seq, LANES), lambda b, h: (b, part * DA_HEADS + h))
    vrows = DA_V_DIM + ONES_ROWS
    return pl.pallas_call(
        kern,
        grid=(bsz, DA_HEADS),
        in_specs=[head_block(0), head_block(1), head_block(2),
                  _const_spec((1, 1)), _const_spec((DA_V_DIM, 1))],
        out_specs=pl.BlockSpec((seq, DA_V_DIM), lambda b, h: (b, h)),
        out_shape=jax.ShapeDtypeStruct((bsz * seq, DA_WIDTH), BF16),
        scratch_shapes=[pltpu.VMEM((2, seq, LANES), BF16),
                        pltpu.VMEM((seq // tk, vrows, tk), BF16),
                        pltpu.VMEM((2, 1, tq), F32),
                        pltpu.VMEM((2, vrows, tq), F32),
                        pltpu.VMEM((2, 2, tk, tq), F32)],
        compiler_params=_params(("parallel", "parallel")),
        name="diff_attention",
    )(da, da, da, lam, subln_col)


def _ret_state_kernel(kf_ref, vf_ref, kb_ref, vb_ref, dkf_ref, dkb_ref, gc_ref,
                      sf_out, sb_out, sf_ref, sb_ref):
    @pl.when(pl.program_id(1) == 0)
    def _():
        sf_ref[...] = jnp.zeros(sf_ref.shape, F32)
        sb_ref[...] = jnp.zeros(sb_ref.shape, F32)

    c = dkf_ref.shape[0]
    per_step = sf_out.shape[0]

    def update(s_ref, k, v, dk_ref):
        kd = (k.astype(F32) * dk_ref[...]).astype(BF16)
        for pr in range(RET_WIDTH // LANES):
            sl = slice(pr * LANES, (pr + 1) * LANES)
            kv = _dot_tn(kd[:, sl], v[:, sl])
            keep = gc_ref[sl, :]
            s_ref[sl, :] = keep * s_ref[sl, :] + jnp.where(keep > 0.0, kv, 0.0)

    for u in range(per_step):
        lo = slice(u * c, (u + 1) * c)
        hi = slice((per_step - 1 - u) * c, (per_step - u) * c)
        sf_out[u] = sf_ref[...]
        sb_out[per_step - 1 - u] = sb_ref[...]
        update(sf_ref, kf_ref[lo, :], vf_ref[lo, :], dkf_ref)
        update(sb_ref, kb_ref[hi, :], vb_ref[hi, :], dkb_ref)


def _ret_out_kernel(q_ref, k_ref, v_ref, g_ref, sf_ref, sb_ref, dec_ref, dqf_ref, dqb_ref,
                    avg_ref, gng_ref, gnb_ref, o_ref):
    c = dqf_ref.shape[0]
    lane = lax.broadcasted_iota(jnp.int32, (c, LANES), 1)
    zero = jnp.zeros((c, LANES), BF16)
    avg = avg_ref[...]
    for u in range(sf_ref.shape[0]):
        rows = slice(u * c, (u + 1) * c)
        q = q_ref[rows, :]
        k = k_ref[rows, :]
        v = v_ref[rows, :]
        qf = q.astype(F32)
        parts = []
        for pr in range(RET_WIDTH // LANES):
            sl = slice(pr * LANES, (pr + 1) * LANES)
            qp, kp, vp = q[:, sl], k[:, sl], v[:, sl]
            acc = _dot((qf[:, sl] * dqf_ref[:, sl]).astype(BF16), sf_ref[u, sl, :].astype(BF16))
            acc += _dot((qf[:, sl] * dqb_ref[:, sl]).astype(BF16), sb_ref[u, sl, :].astype(BF16))
            for hh in range(LANES // RET_HEAD_DIM):
                mine = (lane >= hh * RET_HEAD_DIM) & (lane < (hh + 1) * RET_HEAD_DIM)
                s = _dot_nt(jnp.where(mine, qp, zero), kp) * dec_ref[pr * 2 + hh]
                acc += _dot(s.astype(BF16), jnp.where(mine, vp, zero))
            parts.append(acc)
        o = jnp.concatenate(parts, axis=-1)
        mu = _dot_split(o, avg)
        d = o - mu
        var = _dot_split(d * d, avg)
        y = d * lax.rsqrt(var + LN_EPS) * gng_ref[...] + gnb_ref[...]
        g = g_ref[rows, :]
        o_ref[rows, :] = (g * _sigmoid(g) * y).astype(o_ref.dtype)


def retention(ret, g, gn_g, gn_b, bsz, seq):
    c = min(RET_CHUNK, seq)
    nc = seq // c
    heads = jnp.arange(RET_HEADS, dtype=F32)
    log_gamma = jnp.log(1.0 - 2.0 ** (-5.0 - heads))
    lg_cols = jnp.repeat(log_gamma, RET_HEAD_DIM)[None, :]
    idx = jnp.arange(c, dtype=F32)[:, None]
    dk_f = jnp.exp(lg_cols * (c - 1 - idx))
    dk_b = jnp.exp(lg_cols * idx)
    dq_f = jnp.exp(lg_cols * (idx + 1))
    dq_b = jnp.exp(lg_cols * (c - idx))
    dist = jnp.abs(idx - idx.T)
    decay = jnp.exp(log_gamma[:, None, None] * dist[None])
    row_head = jnp.arange(RET_WIDTH)[:, None] // RET_HEAD_DIM
    col_head = (jnp.arange(LANES)[None, :] // RET_HEAD_DIM) + 2 * (jnp.arange(RET_WIDTH)[:, None] // LANES)
    gc = jnp.where(row_head == col_head, jnp.exp(lg_cols.T * c), 0.0).astype(F32)
    seg = jnp.arange(RET_WIDTH) // RET_HEAD_DIM
    avg = jnp.where(seg[:, None] == seg[None, :], 1.0 / RET_HEAD_DIM, 0.0).astype(BF16)
    gng = jnp.tile(gn_g, RET_HEADS)[None, :]
    gnb = jnp.tile(gn_b, RET_HEADS)[None, :]

    per_step = RET_CHUNKS_PER_STEP if nc % RET_CHUNKS_PER_STEP == 0 else 1
    ns = nc // per_step
    rows = per_step * c
    blk = lambda col: pl.BlockSpec((rows, RET_WIDTH), lambda b, j: (b * ns + j, col))
    blk_rev = lambda col: pl.BlockSpec((rows, RET_WIDTH), lambda b, j: (b * ns + ns - 1 - j, col))
    st_shape = jax.ShapeDtypeStruct((bsz, nc, RET_WIDTH, LANES), F32)
    st_blk = (None, per_step, RET_WIDTH, LANES)
    sf, sb = pl.pallas_call(
        _ret_state_kernel,
        grid=(bsz, ns),
        in_specs=[blk(1), blk(2), blk_rev(1), blk_rev(2),
                  _const_spec((c, RET_WIDTH)), _const_spec((c, RET_WIDTH)),
                  _const_spec((RET_WIDTH, LANES))],
        out_specs=[pl.BlockSpec(st_blk, lambda b, j: (b, j, 0, 0)),
                   pl.BlockSpec(st_blk, lambda b, j: (b, ns - 1 - j, 0, 0))],
        out_shape=[st_shape, st_shape],
        scratch_shapes=[pltpu.VMEM((RET_WIDTH, LANES), F32), pltpu.VMEM((RET_WIDTH, LANES), F32)],
        compiler_params=_params(("parallel", "arbitrary")),
        name="retention_state",
    )(ret, ret, ret, ret, dk_f, dk_b, gc)

    st_spec = pl.BlockSpec(st_blk, lambda b, j: (b, j, 0, 0))
    return pl.pallas_call(
        _ret_out_kernel,
        grid=(bsz, ns),
        in_specs=[blk(0), blk(1), blk(2), blk(0), st_spec, st_spec,
                  _const_spec((RET_HEADS, c, c)), _const_spec((c, RET_WIDTH)),
                  _const_spec((c, RET_WIDTH)), _const_spec((RET_WIDTH, RET_WIDTH)),
                  _const_spec((1, RET_WIDTH)), _const_spec((1, RET_WIDTH))],
        out_specs=blk(0),
        out_shape=jax.ShapeDtypeStruct((bsz * seq, RET_WIDTH), BF16),
        compiler_params=_params(("parallel", "parallel")),
        name="retention_out",
    )(ret, ret, ret, g, sf, sb, decay, dq_f, dq_b, avg, gng, gnb)


def _cmul(ar, ai, br, bi):
    return ar * br - ai * bi, ar * bi + ai * br


def s5_matrices(A_re, A_im, log_dt, B_re, B_im, C_re, C_im, D):
    T, G, P, Cn = S5_CHUNK, S5_GROUPS, S5_STATE, S5_GROUP
    gh = S5_HALF // Cn
    depth = A_re.shape[0]
    step = jnp.exp(log_dt.astype(F32))[..., None]
    a_re = A_re.astype(F32)
    a_im = A_im.astype(F32)
    d = jnp.arange(T + 1, dtype=F32).reshape(T + 1, 1, 1, 1, 1)
    mag = jnp.exp(d * (step * a_re))
    pw_re = mag * jnp.cos(d * (step * a_im))
    pw_im = mag * jnp.sin(d * (step * a_im))
    den = a_re * a_re + a_im * a_im
    nr = pw_re[1] - 1.0
    ni = pw_im[1]
    coef_re = ((nr * a_re + ni * a_im) / den)[..., None]
    coef_im = ((ni * a_re - nr * a_im) / den)[..., None]
    b_re = B_re.astype(F32)
    b_im = B_im.astype(F32)
    bb_re = coef_re * b_re - coef_im * b_im
    bb_im = coef_re * b_im + coef_im * b_re
    c_re = C_re.astype(F32)
    c_im = C_im.astype(F32)
    t_idx = jnp.arange(T)
    fwd_in, bwd_in = T - 1 - t_idx, t_idx
    fwd_out, bwd_out = t_idx + 1, T - t_idx

    def state_in(direction, order):
        return _cmul(pw_re[order, :, direction, :, None, :], pw_im[order, :, direction, :, None, :],
                     jnp.swapaxes(bb_re[:, direction], -1, -2)[None],
                     jnp.swapaxes(bb_im[:, direction], -1, -2)[None])

    def read_out(direction, order):
        return _cmul(c_re[None, :, direction], c_im[None, :, direction],
                     pw_re[order, :, direction, :, None, :], pw_im[order, :, direction, :, None, :])

    def lag_kernel(direction):
        wr, wi = _cmul(pw_re[:T, :, direction, :, :, None], pw_im[:T, :, direction, :, :, None],
                       bb_re[None, :, direction], bb_im[None, :, direction])
        return (jnp.einsum('lgop,dlgpi->dlgoi', c_re[:, direction], wr)
                - jnp.einsum('lgop,dlgpi->dlgoi', c_im[:, direction], wi))

    def halves(v, lead):
        return v.reshape(v.shape[:lead] + (2, gh) + v.shape[lead + 1:])

    vfr, vfi = state_in(0, fwd_in)
    vbr, vbi = state_in(1, bwd_in)
    def spread(compact, n_inner, row_group):
        n_in = compact.shape[-1]
        n_out = n_in * gh
        src = jnp.arange(n_in)
        dst = jnp.arange(n_out)
        same = ((src[:, None] // n_inner == dst[None, :] // (gh * n_inner))
                & (src[:, None] % n_inner == dst[None, :] % n_inner))
        keep = row_group[:, None] == (dst[None, :] // n_inner) % gh
        wide = jnp.einsum('ldrk,kc->ldrc', compact, same.astype(BF16))
        return jnp.where(keep, wide, jnp.zeros((), BF16))

    in_rows = (jnp.arange(T * gh * Cn) // Cn) % gh
    state_rows = (jnp.arange(4 * gh * P) // P) % gh
    v_all = halves(jnp.stack([vfr, vfi, vbr, vbi]).astype(BF16), 3)
    v_compact = jnp.transpose(v_all, (2, 3, 1, 4, 5, 0, 6)).reshape(depth, 2, T * gh * Cn, 4 * P)
    mb = spread(v_compact, P, in_rows)

    efr, efi = read_out(0, fwd_out)
    ebr, ebi = read_out(1, bwd_out)
    e_all = halves(jnp.stack([efr, -efi, ebr, -ebi]).astype(BF16), 3)
    e_compact = jnp.transpose(e_all, (2, 3, 0, 4, 6, 1, 5)).reshape(depth, 2, 4 * gh * P, T * Cn)
    mc = spread(e_compact, Cn, state_rows)

    lag = t_idx[None, :] - t_idx[:, None]
    sel = lambda cond: cond[:, :, None, None, None, None]
    d_diag = D.astype(F32).reshape(depth, G, Cn)[..., None] * jnp.eye(Cn, dtype=F32)
    toe = (jnp.where(sel(lag >= 0), lag_kernel(0)[jnp.clip(lag, 0, T - 1)], 0.0)
           + jnp.where(sel(lag <= 0), lag_kernel(1)[jnp.clip(-lag, 0, T - 1)], 0.0)
           + jnp.where(sel(lag == 0), d_diag[None, None], 0.0))
    t_compact = jnp.transpose(halves(toe.astype(BF16), 3), (2, 3, 0, 4, 6, 1, 5)).reshape(
        depth, 2, T * gh * Cn, T * Cn)
    tp = spread(t_compact, Cn, in_rows)

    a_rows = jnp.stack([pw_re[T, :, 0], pw_im[T, :, 0], pw_re[T, :, 1], pw_im[T, :, 1]], axis=1)
    a8 = jnp.swapaxes(a_rows.reshape(depth, 4, 2, gh * P), 1, 2)
    a8 = jnp.concatenate([a8, a8], axis=2)
    return mb, tp, mc, a8


def _s5_kernel(u_ref, mb_ref, tp_ref, mc_ref, a8_ref, y_ref, u8_ref, w_ref, *, sub):
    rows = w_ref.shape[0]
    ns = a8_ref.shape[1]
    for r in range(0, rows, sub):
        steps = [u_ref[pl.ds(r * S5_CHUNK + s, sub, stride=S5_CHUNK), :].astype(BF16)
                 for s in range(S5_CHUNK)]
        u8 = jnp.concatenate(steps, axis=-1)
        u8_ref[r:r + sub, :] = u8
        w_ref[r:r + sub, :] = _dot(u8, mb_ref[...])

    afr, afi = a8_ref[0:1, :], a8_ref[1:2, :]
    abr, abi = a8_ref[2:3, :], a8_ref[3:4, :]

    def scan_step(j, carry):
        xfr, xfi, xbr, xbi = carry
        jb = rows - 1 - j
        wf_r = w_ref[pl.ds(j, 1), 0:ns]
        wf_i = w_ref[pl.ds(j, 1), ns:2 * ns]
        wb_r = w_ref[pl.ds(jb, 1), 2 * ns:3 * ns]
        wb_i = w_ref[pl.ds(jb, 1), 3 * ns:4 * ns]
        w_ref[pl.ds(j, 1), 0:ns] = xfr
        w_ref[pl.ds(j, 1), ns:2 * ns] = xfi
        w_ref[pl.ds(jb, 1), 2 * ns:3 * ns] = xbr
        w_ref[pl.ds(jb, 1), 3 * ns:4 * ns] = xbi
        nfr = afr * xfr - afi * xfi + wf_r
        nfi = afr * xfi + afi * xfr + wf_i
        nbr = abr * xbr - abi * xbi + wb_r
        nbi = abr * xbi + abi * xbr + wb_i
        return nfr, nfi, nbr, nbi

    z = jnp.zeros((1, ns), F32)
    lax.fori_loop(0, rows, scan_step, (z, z, z, z))

    for r in range(0, rows, sub):
        y8 = (_dot(u8_ref[r:r + sub, :], tp_ref[...])
              + _dot(w_ref[r:r + sub, :].astype(BF16), mc_ref[...]))
        for t in range(S5_CHUNK):
            y_ref[pl.ds(r * S5_CHUNK + t, sub, stride=S5_CHUNK), :] = y8[:, t * S5_HALF:(t + 1) * S5_HALF]


def s5_mixer(u, mats, layer, bsz, seq):
    mb, tp, mc, a8 = mats
    n = bsz * seq
    rows = seq // S5_CHUNK
    width = S5_CHUNK * S5_HALF
    ns = (S5_HALF // S5_GROUP) * S5_STATE
    kern = functools.partial(_s5_kernel, sub=min(256, rows))
    wspec = lambda a: pl.BlockSpec((None, None) + a.shape[2:], lambda h, b: (layer, h, 0, 0),
                                   pipeline_mode=pl.Buffered(1))
    tokens = pl.BlockSpec((None, seq, S5_HALF), lambda h, b: (h, b, 0))
    return pl.pallas_call(
        kern,
        grid=(2, bsz),
        in_specs=[tokens, wspec(mb), wspec(tp), wspec(mc), wspec(a8)],
        out_specs=tokens,
        out_shape=jax.ShapeDtypeStruct((2, n, S5_HALF), F32),
        scratch_shapes=[pltpu.VMEM((rows, width), BF16), pltpu.VMEM((rows, 4 * ns), F32)],
        compiler_params=_params(("arbitrary", "arbitrary")),
        name="s5_mixer",
    )(u, mb, tp, mc, a8)


def _out_proj_kernel(x_ref, da_ref, ret_ref, y5_ref, gluw_ref, glub_ref, wout_ref,
                     g_ref, b_ref, o_ref, *, alpha):
    c1 = DA_WIDTH
    c2 = DA_WIDTH + RET_WIDTH
    for r in range(0, x_ref.shape[0], EPILOGUE_ROWS):
        rs = slice(r, r + EPILOGUE_ROWS)
        y = jnp.concatenate([y5_ref[0, rs, :], y5_ref[1, rs, :]], axis=-1)
        ya = _gelu_tanh(y)
        gate = _sigmoid(_dot(ya.astype(BF16), gluw_ref[...]) + glub_ref[...])
        ys5 = (ya * gate).astype(BF16)
        mix = (_dot(da_ref[rs, :], wout_ref[0:c1, :]) + _dot(ret_ref[rs, :], wout_ref[c1:c2, :])
               + _dot(ys5, wout_ref[c2:, :]))
        o_ref[rs, :] = _layer_norm(alpha * x_ref[rs, :] + mix, g_ref[...], b_ref[...])


def out_proj(x, y_da, y_ret, y5, glu_w, glu_b, w_out, ln_g, ln_b, alpha, tm=ROW_TILE):
    n = x.shape[0]
    tm = min(tm, n)
    row = lambda i: (i, 0)
    return pl.pallas_call(
        functools.partial(_out_proj_kernel, alpha=alpha),
        grid=(n // tm,),
        in_specs=[pl.BlockSpec((tm, D_MODEL), row), pl.BlockSpec((tm, DA_WIDTH), row),
                  pl.BlockSpec((tm, RET_WIDTH), row),
                  pl.BlockSpec((2, tm, S5_HALF), lambda i: (0, i, 0)),
                  _const_spec((S5_WIDTH, S5_WIDTH)), _const_spec((1, S5_WIDTH)),
                  _const_spec((D_MODEL, D_MODEL)),
                  _const_spec((1, D_MODEL)), _const_spec((1, D_MODEL))],
        out_specs=pl.BlockSpec((tm, D_MODEL), row),
        out_shape=jax.ShapeDtypeStruct((n, D_MODEL), F32),
        compiler_params=_params(("parallel",)),
        name="out_proj",
    )(x, y_da, y_ret, y5, glu_w, glu_b, w_out, ln_g, ln_b)


def _ffn_kernel(x_ref, xp_ref, xn_ref, p_ref, wup_ref, cw_ref, cb_ref, wdn_ref,
                plew_ref, gatew_ref, g_ref, b_ref, o_ref, act_ref, *, alpha, tiles_per_seq):
    tm = x_ref.shape[0]
    i = pl.program_id(0)
    has_prev = ((i % tiles_per_seq) != 0).astype(F32)
    has_next = ((i % tiles_per_seq) != tiles_per_seq - 1).astype(F32)
    x = x_ref[...]
    xb = x.astype(BF16)
    xpb = xp_ref[...].astype(BF16)
    xnb = xn_ref[...].astype(BF16)
    halo = xp_ref.shape[0]
    row = lax.broadcasted_iota(jnp.int32, (halo, FF_CHUNK), 0)

    for c in range(0, D_FF, FF_CHUNK):
        wg = wup_ref[:, c:c + FF_CHUNK]
        gate = _dot(xb, wg)
        val = _dot(xb, wup_ref[:, D_FF + c:D_FF + c + FF_CHUNK])
        before = _dot(xpb, wg)[halo - 1:halo, :] * has_prev
        after = _dot(xnb, wg)[0:1, :] * has_next
        left = pltpu.roll(gate, 1, 0)
        left = jnp.concatenate([jnp.where(row == 0, before, left[:halo]), left[halo:]], axis=0)
        right = pltpu.roll(gate, tm - 1, 0)
        right = jnp.concatenate(
            [right[:tm - halo], jnp.where(row == halo - 1, after, right[tm - halo:])], axis=0)
        conv = (cw_ref[0:1, c:c + FF_CHUNK] * left + cw_ref[1:2, c:c + FF_CHUNK] * gate
                + cw_ref[2:3, c:c + FF_CHUNK] * right + cb_ref[:, c:c + FF_CHUNK])
        act_ref[:, c:c + FF_CHUNK] = (_gelu_tanh(conv) * val).astype(BF16)

    for r in range(0, tm, EPILOGUE_ROWS):
        rs = slice(r, r + EPILOGUE_ROWS)
        f = _dot(act_ref[rs, :], wdn_ref[...])
        ple = (_dot(p_ref[rs, :].astype(BF16), plew_ref[...])
               * _sigmoid(_dot(xb[rs], gatew_ref[...])))
        o_ref[rs, :] = _layer_norm(alpha * x[rs] + f + ple, g_ref[...], b_ref[...])


def conv_ffn_ple(x, p, layer, w_up, conv_w, conv_b, w_down, ple_w, gate_w, ln_g, ln_b, alpha, seq, tm=ROW_TILE):
    n = x.shape[0]
    tm = min(tm, seq)
    p_base = layer * (n // tm)
    halo = 8
    tiles_per_seq = seq // tm
    per = tm // halo
    last = n // halo - 1
    row = lambda i: (i, 0)
    kern = functools.partial(_ffn_kernel, alpha=alpha, tiles_per_seq=tiles_per_seq)
    return pl.pallas_call(
        kern,
        grid=(n // tm,),
        in_specs=[pl.BlockSpec((tm, D_MODEL), row),
                  pl.BlockSpec((halo, D_MODEL), lambda i: (jnp.maximum(i * per - 1, 0), 0)),
                  pl.BlockSpec((halo, D_MODEL), lambda i: (jnp.minimum((i + 1) * per, last), 0)),
                  pl.BlockSpec((tm, PLE_DIM), lambda i: (p_base + i, 0)),
                  _const_spec((D_MODEL, 2 * D_FF)), _const_spec((3, D_FF)), _const_spec((1, D_FF)),
                  _const_spec((D_FF, D_MODEL)), _const_spec((PLE_DIM, D_MODEL)),
                  _const_spec((D_MODEL, D_MODEL)),
                  _const_spec((1, D_MODEL)), _const_spec((1, D_MODEL))],
        out_specs=pl.BlockSpec((tm, D_MODEL), row),
        out_shape=jax.ShapeDtypeStruct((n, D_MODEL), F32),
        scratch_shapes=[pltpu.VMEM((tm, D_FF), BF16)],
        compiler_params=_params(("parallel",)),
        name="conv_ffn_ple",
    )(x, x, x, p, w_up, conv_w, conv_b, w_down, ple_w, gate_w, ln_g, ln_b)


def kernel(x, p, positions, w_in, da_lambda_q1, da_lambda_k1, da_lambda_q2, da_lambda_k2,
           da_subln_g, ret_gn_g, ret_gn_b, s5_A_re, s5_A_im, s5_log_dt, s5_B_re, s5_B_im,
           s5_C_re, s5_C_im, s5_D, s5_glu_w, s5_glu_b, w_out, ln1_g, ln1_b,
           ffn_w_up, ffn_conv_w, ffn_conv_b, ffn_w_down, ple_w, ple_gate_w, ln2_g, ln2_b):
    bsz, seq, _ = x.shape
    depth = w_in.shape[0]
    n = bsz * seq
    alpha = (2 * depth) ** 0.25
    cos, sin = rope_tables(positions)
    xf = x.reshape(n, D_MODEL)
    p_all = p.reshape(depth * n, PLE_DIM)
    mats = s5_matrices(s5_A_re, s5_A_im, s5_log_dt, s5_B_re, s5_B_im, s5_C_re, s5_C_im, s5_D)
    row = lambda v: v.reshape(1, -1).astype(F32)
    for i in range(depth):
        lambda_init = 0.8 - 0.6 * math.exp(-0.3 * i)
        lam = (jnp.exp(jnp.sum(da_lambda_q1[i].astype(F32) * da_lambda_k1[i].astype(F32)))
               - jnp.exp(jnp.sum(da_lambda_q2[i].astype(F32) * da_lambda_k2[i].astype(F32)))
               + lambda_init)
        da, ret, g, u = in_proj(xf, w_in[i].astype(BF16), cos, sin)
        y_da = diff_attention(da, lam.reshape(1, 1), da_subln_g[i].astype(F32).reshape(-1, 1),
                              1.0 - lambda_init, bsz, seq)
        y_ret = retention(ret, g, ret_gn_g[i].astype(F32), ret_gn_b[i].astype(F32), bsz, seq)
        y5 = s5_mixer(u, mats, i, bsz, seq)
        x1 = out_proj(xf, y_da, y_ret, y5,
                      s5_glu_w[i].astype(BF16), row(s5_glu_b[i]), w_out[i].astype(BF16),
                      row(ln1_g[i]), row(ln1_b[i]), alpha)
        xf = conv_ffn_ple(x1, p_all, i, ffn_w_up[i].astype(BF16),
                          ffn_conv_w[i].astype(F32), row(ffn_conv_b[i]),
                          ffn_w_down[i].astype(BF16), ple_w[i].astype(BF16),
                          ple_gate_w[i].astype(BF16), row(ln2_g[i]), row(ln2_b[i]), alpha, seq)
    return xf.reshape(bsz, seq, D_MODEL)
```

```python
import functools
import math

import jax
import jax.numpy as jnp
from jax import lax
from jax.experimental import pallas as pl
from jax.experimental.pallas import tpu as pltpu

F32 = jnp.float32
BF16 = jnp.bfloat16

D_MODEL = 1024
PLE_DIM = 256
DA_HEADS = 4
DA_QK_DIM = 64
DA_V_DIM = 128
DA_WIDTH = DA_HEADS * DA_V_DIM
RET_HEADS = 4
RET_HEAD_DIM = 64
RET_WIDTH = RET_HEADS * RET_HEAD_DIM
S5_WIDTH = 256
S5_GROUP = 16
S5_GROUPS = S5_WIDTH // S5_GROUP
S5_STATE = 64
D_FF = 2816
ROPE_THETA = 10000.0
LN_EPS = 1e-5
RMS_EPS = 1e-6

COL_DA_Q = 0
COL_DA_K = COL_DA_Q + DA_HEADS * 2 * DA_QK_DIM
COL_DA_V = COL_DA_K + DA_HEADS * 2 * DA_QK_DIM
COL_RET_Q = COL_DA_V + DA_WIDTH
COL_RET_K = COL_RET_Q + RET_WIDTH
COL_RET_V = COL_RET_K + RET_WIDTH
COL_RET_G = COL_RET_V + RET_WIDTH
COL_S5_U = COL_RET_G + RET_WIDTH
IN_COLS = COL_S5_U + S5_WIDTH

LANES = 128
MXU_WIDTH = 256
S5_CHUNK = 8
S5_HALF = 128
RET_CHUNK = 256
RET_CHUNKS_PER_STEP = 4
FF_CHUNK = 256
ROW_TILE = 1024
EPILOGUE_ROWS = 256
LOG2E = 1.4426950408889634
NEG_BIG = -1e30
VMEM_LIMIT = 56 * 1024 * 1024


def _params(sem, vmem=VMEM_LIMIT):
    return pltpu.CompilerParams(dimension_semantics=sem, vmem_limit_bytes=vmem)


def _const_spec(shape):
    nd = len(shape)
    return pl.BlockSpec(shape, lambda *_: (0,) * nd, pipeline_mode=pl.Buffered(1))


def _layer_spec(shape, layer):
    nd = len(shape)
    return pl.BlockSpec((None,) + tuple(shape), lambda *_: (layer,) + (0,) * nd,
                        pipeline_mode=pl.Buffered(1))


def _layer_norm(x, g, b):
    mu = jnp.mean(x, axis=-1, keepdims=True)
    d = x - mu
    var = jnp.mean(d * d, axis=-1, keepdims=True)
    return d * lax.rsqrt(var + LN_EPS) * g + b


def _gelu_tanh(x):
    return 0.5 * x * (1.0 + jnp.tanh(math.sqrt(2.0 / math.pi) * (x + 0.044715 * (x * x * x))))


def _sigmoid(x):
    return 1.0 / (1.0 + jnp.exp(-x))


def _dot(a, b):
    return jnp.dot(a, b, preferred_element_type=F32)


def _dot_nt(a, b):
    return lax.dot_general(a, b, (((1,), (1,)), ((), ())), preferred_element_type=F32)


def _dot_tn(a, b):
    return lax.dot_general(a, b, (((0,), (0,)), ((), ())), preferred_element_type=F32)


def _dot_split(x, w):
    hi = x.astype(BF16)
    lo = (x - hi.astype(F32)).astype(BF16)
    return _dot(hi, w) + _dot(lo, w)


def _rope_table_kernel(pos_ref, freq_ref, sign_ref, cos_ref, sin_ref):
    ang = pos_ref[...].astype(F32) * freq_ref[...]
    cos_ref[...] = jnp.cos(ang)
    sin_ref[...] = jnp.sin(ang) * sign_ref[...]


def rope_tables(positions, tm=1024):
    n = positions.size
    half = DA_QK_DIM // 2
    inv_freq = ROPE_THETA ** (-jnp.arange(0, DA_QK_DIM, 2, dtype=F32) / DA_QK_DIM)
    freq_row = jnp.tile(inv_freq, LANES // half).reshape(1, LANES)
    lane = jnp.arange(LANES)
    sign_row = jnp.where(lane % DA_QK_DIM < half, -1.0, 1.0).astype(F32).reshape(1, LANES)
    pos = positions.reshape(n, 1)
    tm = min(tm, n)
    return pl.pallas_call(
        _rope_table_kernel,
        grid=(n // tm,),
        in_specs=[pl.BlockSpec((tm, 1), lambda i: (i, 0)),
                  _const_spec((1, LANES)), _const_spec((1, LANES))],
        out_specs=[pl.BlockSpec((tm, LANES), lambda i: (i, 0)),
                   pl.BlockSpec((tm, LANES), lambda i: (i, 0))],
        out_shape=[jax.ShapeDtypeStruct((n, LANES), F32)] * 2,
        compiler_params=_params(("parallel",)),
        name="rope_tables",
    )(pos, freq_row, sign_row)


def _rope(x, cos, sin, first_half):
    swapped = jnp.where(first_half, pltpu.roll(x, LANES - DA_QK_DIM // 2, 1),
                        pltpu.roll(x, DA_QK_DIM // 2, 1))
    return x * cos + swapped * sin


def _in_proj_kernel(x_ref, w_ref, cos_ref, sin_ref, da_ref, ret_ref, g_ref, u_ref):
    xb = x_ref[...].astype(BF16)
    cos = cos_ref[...]
    sin = sin_ref[...]
    lane = lax.broadcasted_iota(jnp.int32, cos.shape, 1)
    first_half = (lane % DA_QK_DIM) < (DA_QK_DIM // 2)
    q_scale = DA_QK_DIM ** -0.5 * LOG2E
    k_scale = RET_HEAD_DIM ** -0.5

    def proj(col):
        return _dot(xb, w_ref[:, col:col + MXU_WIDTH])

    def roped(z, scale):
        parts = [_rope(z[:, a:a + LANES], cos, sin, first_half) for a in (0, LANES)]
        out = jnp.concatenate(parts, axis=-1)
        return out if scale is None else out * scale

    for c in range(0, COL_DA_K, MXU_WIDTH):
        da_ref[:, c:c + MXU_WIDTH] = roped(proj(c), q_scale).astype(BF16)
    for c in range(COL_DA_K, COL_DA_V, MXU_WIDTH):
        da_ref[:, c:c + MXU_WIDTH] = roped(proj(c), None).astype(BF16)
    for c in range(COL_DA_V, COL_RET_Q, MXU_WIDTH):
        da_ref[:, c:c + MXU_WIDTH] = proj(c).astype(BF16)
    ret_ref[:, 0:RET_WIDTH] = roped(proj(COL_RET_Q), None).astype(BF16)
    ret_ref[:, RET_WIDTH:2 * RET_WIDTH] = roped(proj(COL_RET_K), k_scale).astype(BF16)
    ret_ref[:, 2 * RET_WIDTH:3 * RET_WIDTH] = proj(COL_RET_V).astype(BF16)
    g_ref[...] = proj(COL_RET_G)
    u = proj(COL_S5_U)
    u_ref[0] = u[:, :S5_HALF]
    u_ref[1] = u[:, S5_HALF:]


def in_proj(x, w_bf16, layer, cos, sin, tm=ROW_TILE):
    n = x.shape[0]
    tm = min(tm, n)
    row = lambda i: (i, 0)
    return pl.pallas_call(
        _in_proj_kernel,
        grid=(n // tm,),
        in_specs=[pl.BlockSpec((tm, D_MODEL), row), _layer_spec((D_MODEL, IN_COLS), layer),
                  pl.BlockSpec((tm, LANES), row), pl.BlockSpec((tm, LANES), row)],
        out_specs=[pl.BlockSpec((tm, COL_RET_Q), row), pl.BlockSpec((tm, 3 * RET_WIDTH), row),
                   pl.BlockSpec((tm, RET_WIDTH), row),
                   pl.BlockSpec((2, tm, S5_HALF), lambda i: (0, i, 0))],
        out_shape=[jax.ShapeDtypeStruct((n, COL_RET_Q), BF16),
                   jax.ShapeDtypeStruct((n, 3 * RET_WIDTH), BF16),
                   jax.ShapeDtypeStruct((n, RET_WIDTH), F32),
                   jax.ShapeDtypeStruct((2, n, S5_HALF), F32)],
        compiler_params=_params(("parallel",)),
        name="in_proj",
    )(x, w_bf16, cos, sin)


ONES_ROWS = 16


def _diff_attn_kernel(q_ref, k_ref, v_ref, lam_ref, g_ref, o_ref,
                      km_ref, vt_ref, m_ref, acc_ref, s_ref, *, tq, tk, out_scale):
    seq = k_ref.shape[0]
    n_kv = seq // tk
    n_q = seq // tq
    k_all = k_ref[...]
    lane = lax.broadcasted_iota(jnp.int32, k_all.shape, 1)
    zero = jnp.zeros_like(k_all)
    km_ref[0] = jnp.where(lane < DA_QK_DIM, k_all, zero)
    km_ref[1] = jnp.where(lane >= DA_QK_DIM, k_all, zero)
    for i in range(n_kv):
        vt_ref[i, :DA_V_DIM, :] = v_ref[i * tk:(i + 1) * tk, :].astype(F32).T.astype(BF16)
        vt_ref[i, DA_V_DIM:, :] = jnp.ones((ONES_ROWS, tk), BF16)

    def scores(qi, ki, slot):
        q = q_ref[pl.ds(pl.multiple_of(qi * tq, tq), tq), :]
        rows = pl.ds(pl.multiple_of(ki * tk, tk), tk)
        for mi in range(2):
            s_ref[slot, mi] = _dot_nt(km_ref[mi, rows, :], q)

    def consume(ki, slot):
        vt = vt_ref[ki]
        for mi in range(2):
            s = s_ref[slot, mi]
            m_old = m_ref[mi]
            m_new = jnp.maximum(m_old, jnp.max(s, axis=0, keepdims=True))
            alpha = jnp.exp2(m_old - m_new)
            p = jnp.exp2(s - m_new).astype(BF16)
            acc_ref[mi] = acc_ref[mi] * alpha + _dot(vt, p)
            m_ref[mi] = m_new

    scores(0, 0, 0)

    def q_tile(qi, carry):
        m_ref[...] = jnp.full(m_ref.shape, NEG_BIG, F32)
        acc_ref[...] = jnp.zeros(acc_ref.shape, F32)

        def kv_pair(j, c):
            for u in range(2):
                ki = 2 * j + u
                wrap = ki + 1 >= n_kv
                scores(jnp.where(wrap, jnp.minimum(qi + 1, n_q - 1), qi),
                       jnp.where(wrap, 0, ki + 1), 1 - u)
                consume(ki, u)
            return c

        lax.fori_loop(0, n_kv // 2, kv_pair, 0)
        a0 = acc_ref[0]
        a1 = acc_ref[1]
        o = (a0[:DA_V_DIM] / a0[DA_V_DIM:DA_V_DIM + 1]
             - lam_ref[...] * (a1[:DA_V_DIM] / a1[DA_V_DIM:DA_V_DIM + 1]))
        ms = jnp.mean(o * o, axis=0, keepdims=True)
        o = o * lax.rsqrt(ms + RMS_EPS) * g_ref[...] * out_scale
        rows = pl.ds(pl.multiple_of(qi * tq, tq), tq)
        o_ref[rows, :] = o.T.astype(o_ref.dtype)
        return carry

    lax.fori_loop(0, n_q, q_tile, 0)


def diff_attention(da, lam, subln_col, out_scale, bsz, seq, tq=256, tk=4096):
    tq = min(tq, seq)
    tk = min(tk, seq // 2)
    assert seq % (2 * tk) == 0 and seq % tq == 0
    kern = functools.partial(_diff_attn_kernel, tq=tq, tk=tk, out_scale=out_scale)
    head_block = lambda part: pl.BlockSpec((seq, LANES), lambda b, h: (b, part * DA_HEADS + h))
    vrows = DA_V_DIM + ONES_ROWS
    return pl.pallas_call(
        kern,
        grid=(bsz, DA_HEADS),
        in_specs=[head_block(0), head_block(1), head_block(2),
                  _const_spec((1, 1)), _const_spec((DA_V_DIM, 1))],
        out_specs=pl.BlockSpec((seq, DA_V_DIM), lambda b, h: (b, h)),
        out_shape=jax.ShapeDtypeStruct((bsz * seq, DA_WIDTH), BF16),
        scratch_shapes=[pltpu.VMEM((2, seq, LANES), BF16),
                        pltpu.VMEM((seq // tk, vrows, tk), BF16),
                        pltpu.VMEM((2, 1, tq), F32),
                        pltpu.VMEM((2, vrows, tq), F32),
                        pltpu.VMEM((2, 2, tk, tq), F32)],
        compiler_params=_params(("parallel", "parallel")),
        name="diff_attention",
    )(da, da, da, lam, subln_col)


def _ret_state_kernel(kf_ref, vf_ref, kb_ref, vb_ref, dkf_ref, dkb_ref, gc_ref,
                      sf_out, sb_out, sf_ref, sb_ref):
    @pl.when(pl.program_id(1) == 0)
    def _():
        sf_ref[...] = jnp.zeros(sf_ref.shape, F32)
        sb_ref[...] = jnp.zeros(sb_ref.shape, F32)

    c = dkf_ref.shape[0]
    per_step = sf_out.shape[0]

    def update(s_ref, k, v, dk_ref):
        kd = (k.astype(F32) * dk_ref[...]).astype(BF16)
        for pr in range(RET_WIDTH // LANES):
            sl = slice(pr * LANES, (pr + 1) * LANES)
            kv = _dot_tn(kd[:, sl], v[:, sl])
            keep = gc_ref[sl, :]
            s_ref[sl, :] = keep * s_ref[sl, :] + jnp.where(keep > 0.0, kv, 0.0)

    for u in range(per_step):
        lo = slice(u * c, (u + 1) * c)
        hi = slice((per_step - 1 - u) * c, (per_step - u) * c)
        sf_out[u] = sf_ref[...]
        sb_out[per_step - 1 - u] = sb_ref[...]
        update(sf_ref, kf_ref[lo, :], vf_ref[lo, :], dkf_ref)
        update(sb_ref, kb_ref[hi, :], vb_ref[hi, :], dkb_ref)


def _ret_out_kernel(q_ref, k_ref, v_ref, g_ref, sf_ref, sb_ref, dec_ref, dqf_ref, dqb_ref,
                    avg_ref, gng_ref, gnb_ref, o_ref):
    c = dqf_ref.shape[0]
    lane = lax.broadcasted_iota(jnp.int32, (c, LANES), 1)
    zero = jnp.zeros((c, LANES), BF16)
    avg = avg_ref[...]
    for u in range(sf_ref.shape[0]):
        rows = slice(u * c, (u + 1) * c)
        q = q_ref[rows, :]
        k = k_ref[rows, :]
        v = v_ref[rows, :]
        qf = q.astype(F32)
        parts = []
        for pr in range(RET_WIDTH // LANES):
            sl = slice(pr * LANES, (pr + 1) * LANES)
            qp, kp, vp = q[:, sl], k[:, sl], v[:, sl]
            acc = _dot((qf[:, sl] * dqf_ref[:, sl]).astype(BF16), sf_ref[u, sl, :].astype(BF16))
            acc += _dot((qf[:, sl] * dqb_ref[:, sl]).astype(BF16), sb_ref[u, sl, :].astype(BF16))
            for hh in range(LANES // RET_HEAD_DIM):
                mine = (lane >= hh * RET_HEAD_DIM) & (lane < (hh + 1) * RET_HEAD_DIM)
                s = _dot_nt(jnp.where(mine, qp, zero), kp) * dec_ref[pr * 2 + hh]
                acc += _dot(s.astype(BF16), jnp.where(mine, vp, zero))
            parts.append(acc)
        o = jnp.concatenate(parts, axis=-1)
        mu = _dot_split(o, avg)
        d = o - mu
        var = _dot_split(d * d, avg)
        y = d * lax.rsqrt(var + LN_EPS) * gng_ref[...] + gnb_ref[...]
        g = g_ref[rows, :]
        o_ref[rows, :] = (g * _sigmoid(g) * y).astype(o_ref.dtype)


def retention(ret, g, gn_g, gn_b, bsz, seq):
    c = min(RET_CHUNK, seq)
    nc = seq // c
    heads = jnp.arange(RET_HEADS, dtype=F32)
    log_gamma = jnp.log(1.0 - 2.0 ** (-5.0 - heads))
    lg_cols = jnp.repeat(log_gamma, RET_HEAD_DIM)[None, :]
    idx = jnp.arange(c, dtype=F32)[:, None]
    dk_f = jnp.exp(lg_cols * (c - 1 - idx))
    dk_b = jnp.exp(lg_cols * idx)
    dq_f = jnp.exp(lg_cols * (idx + 1))
    dq_b = jnp.exp(lg_cols * (c - idx))
    dist = jnp.abs(idx - idx.T)
    decay = jnp.exp(log_gamma[:, None, None] * dist[None])
    row_head = jnp.arange(RET_WIDTH)[:, None] // RET_HEAD_DIM
    col_head = (jnp.arange(LANES)[None, :] // RET_HEAD_DIM) + 2 * (jnp.arange(RET_WIDTH)[:, None] // LANES)
    gc = jnp.where(row_head == col_head, jnp.exp(lg_cols.T * c), 0.0).astype(F32)
    seg = jnp.arange(RET_WIDTH) // RET_HEAD_DIM
    avg = jnp.where(seg[:, None] == seg[None, :], 1.0 / RET_HEAD_DIM, 0.0).astype(BF16)
    gng = jnp.tile(gn_g, RET_HEADS)[None, :]
    gnb = jnp.tile(gn_b, RET_HEADS)[None, :]

    per_step = RET_CHUNKS_PER_STEP if nc % RET_CHUNKS_PER_STEP == 0 else 1
    ns = nc // per_step
    rows = per_step * c
    blk = lambda col: pl.BlockSpec((rows, RET_WIDTH), lambda b, j: (b * ns + j, col))
    blk_rev = lambda col: pl.BlockSpec((rows, RET_WIDTH), lambda b, j: (b * ns + ns - 1 - j, col))
    st_shape = jax.ShapeDtypeStruct((bsz, nc, RET_WIDTH, LANES), F32)
    st_blk = (None, per_step, RET_WIDTH, LANES)
    sf, sb = pl.pallas_call(
        _ret_state_kernel,
        grid=(bsz, ns),
        in_specs=[blk(1), blk(2), blk_rev(1), blk_rev(2),
                  _const_spec((c, RET_WIDTH)), _const_spec((c, RET_WIDTH)),
                  _const_spec((RET_WIDTH, LANES))],
        out_specs=[pl.BlockSpec(st_blk, lambda b, j: (b, j, 0, 0)),
                   pl.BlockSpec(st_blk, lambda b, j: (b, ns - 1 - j, 0, 0))],
        out_shape=[st_shape, st_shape],
        scratch_shapes=[pltpu.VMEM((RET_WIDTH, LANES), F32), pltpu.VMEM((RET_WIDTH, LANES), F32)],
        compiler_params=_params(("parallel", "arbitrary")),
        name="retention_state",
    )(ret, ret, ret, ret, dk_f, dk_b, gc)

    st_spec = pl.BlockSpec(st_blk, lambda b, j: (b, j, 0, 0))
    return pl.pallas_call(
        _ret_out_kernel,
        grid=(bsz, ns),
        in_specs=[blk(0), blk(1), blk(2), blk(0), st_spec, st_spec,
                  _const_spec((RET_HEADS, c, c)), _const_spec((c, RET_WIDTH)),
                  _const_spec((c, RET_WIDTH)), _const_spec((RET_WIDTH, RET_WIDTH)),
                  _const_spec((1, RET_WIDTH)), _const_spec((1, RET_WIDTH))],
        out_specs=blk(0),
        out_shape=jax.ShapeDtypeStruct((bsz * seq, RET_WIDTH), BF16),
        compiler_params=_params(("parallel", "parallel")),
        name="retention_out",
    )(ret, ret, ret, g, sf, sb, decay, dq_f, dq_b, avg, gng, gnb)


def _cmul(ar, ai, br, bi):
    return ar * br - ai * bi, ar * bi + ai * br


def s5_matrices(A_re, A_im, log_dt, B_re, B_im, C_re, C_im, D):
    T, G, P, Cn = S5_CHUNK, S5_GROUPS, S5_STATE, S5_GROUP
    gh = S5_HALF // Cn
    depth = A_re.shape[0]
    step = jnp.exp(log_dt.astype(F32))[..., None]
    a_re = A_re.astype(F32)
    a_im = A_im.astype(F32)
    d = jnp.arange(T + 1, dtype=F32).reshape(T + 1, 1, 1, 1, 1)
    mag = jnp.exp(d * (step * a_re))
    pw_re = mag * jnp.cos(d * (step * a_im))
    pw_im = mag * jnp.sin(d * (step * a_im))
    den = a_re * a_re + a_im * a_im
    nr = pw_re[1] - 1.0
    ni = pw_im[1]
    coef_re = ((nr * a_re + ni * a_im) / den)[..., None]
    coef_im = ((ni * a_re - nr * a_im) / den)[..., None]
    b_re = B_re.astype(F32)
    b_im = B_im.astype(F32)
    bb_re = coef_re * b_re - coef_im * b_im
    bb_im = coef_re * b_im + coef_im * b_re
    c_re = C_re.astype(F32)
    c_im = C_im.astype(F32)
    t_idx = jnp.arange(T)
    fwd_in, bwd_in = T - 1 - t_idx, t_idx
    fwd_out, bwd_out = t_idx + 1, T - t_idx

    def state_in(direction, order):
        return _cmul(pw_re[order, :, direction, :, None, :], pw_im[order, :, direction, :, None, :],
                     jnp.swapaxes(bb_re[:, direction], -1, -2)[None],
                     jnp.swapaxes(bb_im[:, direction], -1, -2)[None])

    def read_out(direction, order):
        return _cmul(c_re[None, :, direction], c_im[None, :, direction],
                     pw_re[order, :, direction, :, None, :], pw_im[order, :, direction, :, None, :])

    def lag_kernel(direction):
        wr, wi = _cmul(pw_re[:T, :, direction, :, :, None], pw_im[:T, :, direction, :, :, None],
                       bb_re[None, :, direction], bb_im[None, :, direction])
        return (jnp.einsum('lgop,dlgpi->dlgoi', c_re[:, direction], wr)
                - jnp.einsum('lgop,dlgpi->dlgoi', c_im[:, direction], wi))

    def halves(v, lead):
        return v.reshape(v.shape[:lead] + (2, gh) + v.shape[lead + 1:])

    vfr, vfi = state_in(0, fwd_in)
    vbr, vbi = state_in(1, bwd_in)
    def spread(compact, n_inner, row_group):
        n_in = compact.shape[-1]
        n_out = n_in * gh
        src = jnp.arange(n_in)
        dst = jnp.arange(n_out)
        same = ((src[:, None] // n_inner == dst[None, :] // (gh * n_inner))
                & (src[:, None] % n_inner == dst[None, :] % n_inner))
        keep = row_group[:, None] == (dst[None, :] // n_inner) % gh
        wide = jnp.einsum('ldrk,kc->ldrc', compact, same.astype(BF16))
        return jnp.where(keep, wide, jnp.zeros((), BF16))

    in_rows = (jnp.arange(T * gh * Cn) // Cn) % gh
    state_rows = (jnp.arange(4 * gh * P) // P) % gh
    v_all = halves(jnp.stack([vfr, vfi, vbr, vbi]).astype(BF16), 3)
    v_compact = jnp.transpose(v_all, (2, 3, 1, 4, 5, 0, 6)).reshape(depth, 2, T * gh * Cn, 4 * P)
    mb = spread(v_compact, P, in_rows)

    efr, efi = read_out(0, fwd_out)
    ebr, ebi = read_out(1, bwd_out)
    e_all = halves(jnp.stack([efr, -efi, ebr, -ebi]).astype(BF16), 3)
    e_compact = jnp.transpose(e_all, (2, 3, 0, 4, 6, 1, 5)).reshape(depth, 2, 4 * gh * P, T * Cn)
    mc = spread(e_compact, Cn, state_rows)

    lag = t_idx[None, :] - t_idx[:, None]
    sel = lambda cond: cond[:, :, None, None, None, None]
    d_diag = D.astype(F32).reshape(depth, G, Cn)[..., None] * jnp.eye(Cn, dtype=F32)
    toe = (jnp.where(sel(lag >= 0), lag_kernel(0)[jnp.clip(lag, 0, T - 1)], 0.0)
           + jnp.where(sel(lag <= 0), lag_kernel(1)[jnp.clip(-lag, 0, T - 1)], 0.0)
           + jnp.where(sel(lag == 0), d_diag[None, None], 0.0))
    t_compact = jnp.transpose(halves(toe.astype(BF16), 3), (2, 3, 0, 4, 6, 1, 5)).reshape(
        depth, 2, T * gh * Cn, T * Cn)
    tp = spread(t_compact, Cn, in_rows)

    a_rows = jnp.stack([pw_re[T, :, 0], pw_im[T, :, 0], pw_re[T, :, 1], pw_im[T, :, 1]], axis=1)
    a8 = jnp.swapaxes(a_rows.reshape(depth, 4, 2, gh * P), 1, 2)
    a8 = jnp.concatenate([a8, a8], axis=2)
    return mb, tp, mc, a8


def _s5_kernel(u_ref, mb_ref, tp_ref, mc_ref, a8_ref, y_ref, u8_ref, w_ref, *, sub):
    rows = w_ref.shape[0]
    ns = a8_ref.shape[1]
    for r in range(0, rows, sub):
        steps = [u_ref[pl.ds(r * S5_CHUNK + s, sub, stride=S5_CHUNK), :].astype(BF16)
                 for s in range(S5_CHUNK)]
        u8 = jnp.concatenate(steps, axis=-1)
        u8_ref[r:r + sub, :] = u8
        w_ref[r:r + sub, :] = _dot(u8, mb_ref[...])

    def axpy(a, x, w):
        return a[0] * x[0] - a[1] * x[1] + w[0], a[0] * x[1] + a[1] * x[0] + w[1]

    zero = (jnp.zeros((1, ns), F32),) * 2
    a_f = (a8_ref[0:1, :], a8_ref[1:2, :])
    a_b = (a8_ref[2:3, :], a8_ref[3:4, :])
    a2_f = axpy(a_f, a_f, zero)
    a2_b = axpy(a_b, a_b, zero)

    def sweep(x, a, a2, r0, r1, col):
        re, im = slice(col, col + ns), slice(col + ns, col + 2 * ns)
        w0 = (w_ref[pl.ds(r0, 1), re], w_ref[pl.ds(r0, 1), im])
        w1 = (w_ref[pl.ds(r1, 1), re], w_ref[pl.ds(r1, 1), im])
        x1 = axpy(a, x, w0)
        w_ref[pl.ds(r0, 1), re] = x[0]
        w_ref[pl.ds(r0, 1), im] = x[1]
        w_ref[pl.ds(r1, 1), re] = x1[0]
        w_ref[pl.ds(r1, 1), im] = x1[1]
        return axpy(a2, x, axpy(a, w0, w1))

    def scan_step(i, carry):
        xf, xb = carry
        j = 2 * i
        jb = rows - 1 - j
        return sweep(xf, a_f, a2_f, j, j + 1, 0), sweep(xb, a_b, a2_b, jb, jb - 1, 2 * ns)

    lax.fori_loop(0, rows // 2, scan_step, (zero, zero))

    for r in range(0, rows, sub):
        y8 = (_dot(u8_ref[r:r + sub, :], tp_ref[...])
              + _dot(w_ref[r:r + sub, :].astype(BF16), mc_ref[...]))
        for t in range(S5_CHUNK):
            y_ref[pl.ds(r * S5_CHUNK + t, sub, stride=S5_CHUNK), :] = y8[:, t * S5_HALF:(t + 1) * S5_HALF]


def s5_mixer(u, mats, layer, bsz, seq):
    mb, tp, mc, a8 = mats
    n = bsz * seq
    rows = seq // S5_CHUNK
    width = S5_CHUNK * S5_HALF
    ns = (S5_HALF // S5_GROUP) * S5_STATE
    kern = functools.partial(_s5_kernel, sub=min(256, rows))
    wspec = lambda a: pl.BlockSpec((None, None) + a.shape[2:], lambda h, b: (layer, h, 0, 0),
                                   pipeline_mode=pl.Buffered(1))
    tokens = pl.BlockSpec((None, seq, S5_HALF), lambda h, b: (h, b, 0))
    return pl.pallas_call(
        kern,
        grid=(2, bsz),
        in_specs=[tokens, wspec(mb), wspec(tp), wspec(mc), wspec(a8)],
        out_specs=tokens,
        out_shape=jax.ShapeDtypeStruct((2, n, S5_HALF), F32),
        scratch_shapes=[pltpu.VMEM((rows, width), BF16), pltpu.VMEM((rows, 4 * ns), F32)],
        compiler_params=_params(("arbitrary", "arbitrary")),
        name="s5_mixer",
    )(u, mb, tp, mc, a8)


def _out_proj_kernel(x_ref, da_ref, ret_ref, y5_ref, gluw_ref, glub_ref, wout_ref,
                     g_ref, b_ref, o_ref, *, alpha):
    c1 = DA_WIDTH
    c2 = DA_WIDTH + RET_WIDTH
    for r in range(0, x_ref.shape[0], EPILOGUE_ROWS):
        rs = slice(r, r + EPILOGUE_ROWS)
        y = jnp.concatenate([y5_ref[0, rs, :], y5_ref[1, rs, :]], axis=-1)
        ya = _gelu_tanh(y)
        gate = _sigmoid(_dot(ya.astype(BF16), gluw_ref[...]) + glub_ref[...])
        ys5 = (ya * gate).astype(BF16)
        mix = (_dot(da_ref[rs, :], wout_ref[0:c1, :]) + _dot(ret_ref[rs, :], wout_ref[c1:c2, :])
               + _dot(ys5, wout_ref[c2:, :]))
        o_ref[rs, :] = _layer_norm(alpha * x_ref[rs, :] + mix, g_ref[...], b_ref[...])


def out_proj(x, y_da, y_ret, y5, layer, glu_w, glu_b, w_out, ln_g, ln_b, alpha, tm=ROW_TILE):
    n = x.shape[0]
    tm = min(tm, n)
    row = lambda i: (i, 0)
    return pl.pallas_call(
        functools.partial(_out_proj_kernel, alpha=alpha),
        grid=(n // tm,),
        in_specs=[pl.BlockSpec((tm, D_MODEL), row), pl.BlockSpec((tm, DA_WIDTH), row),
                  pl.BlockSpec((tm, RET_WIDTH), row),
                  pl.BlockSpec((2, tm, S5_HALF), lambda i: (0, i, 0)),
                  _layer_spec((S5_WIDTH, S5_WIDTH), layer), _const_spec((1, S5_WIDTH)),
                  _layer_spec((D_MODEL, D_MODEL), layer),
                  _const_spec((1, D_MODEL)), _const_spec((1, D_MODEL))],
        out_specs=pl.BlockSpec((tm, D_MODEL), row),
        out_shape=jax.ShapeDtypeStruct((n, D_MODEL), F32),
        compiler_params=_params(("parallel",)),
        name="out_proj",
    )(x, y_da, y_ret, y5, glu_w, glu_b, w_out, ln_g, ln_b)


def _ffn_kernel(x_ref, xp_ref, xn_ref, p_ref, wup_ref, cw_ref, cb_ref, wdn_ref,
                plew_ref, gatew_ref, g_ref, b_ref, o_ref, act_ref, *, alpha, tiles_per_seq):
    tm = x_ref.shape[0]
    i = pl.program_id(0)
    has_prev = ((i % tiles_per_seq) != 0).astype(F32)
    has_next = ((i % tiles_per_seq) != tiles_per_seq - 1).astype(F32)
    x = x_ref[...]
    xb = x.astype(BF16)
    xpb = xp_ref[...].astype(BF16)
    xnb = xn_ref[...].astype(BF16)
    halo = xp_ref.shape[0]
    row = lax.broadcasted_iota(jnp.int32, (halo, FF_CHUNK), 0)

    for c in range(0, D_FF, FF_CHUNK):
        wg = wup_ref[:, c:c + FF_CHUNK]
        gate = _dot(xb, wg)
        val = _dot(xb, wup_ref[:, D_FF + c:D_FF + c + FF_CHUNK])
        before = _dot(xpb, wg)[halo - 1:halo, :] * has_prev
        after = _dot(xnb, wg)[0:1, :] * has_next
        left = pltpu.roll(gate, 1, 0)
        left = jnp.concatenate([jnp.where(row == 0, before, left[:halo]), left[halo:]], axis=0)
        right = pltpu.roll(gate, tm - 1, 0)
        right = jnp.concatenate(
            [right[:tm - halo], jnp.where(row == halo - 1, after, right[tm - halo:])], axis=0)
        conv = (cw_ref[0:1, c:c + FF_CHUNK] * left + cw_ref[1:2, c:c + FF_CHUNK] * gate
                + cw_ref[2:3, c:c + FF_CHUNK] * right + cb_ref[:, c:c + FF_CHUNK])
        act_ref[:, c:c + FF_CHUNK] = (_gelu_tanh(conv) * val).astype(BF16)

    for r in range(0, tm, EPILOGUE_ROWS):
        rs = slice(r, r + EPILOGUE_ROWS)
        f = _dot(act_ref[rs, :], wdn_ref[...])
        ple = (_dot(p_ref[rs, :].astype(BF16), plew_ref[...])
               * _sigmoid(_dot(xb[rs], gatew_ref[...])))
        o_ref[rs, :] = _layer_norm(alpha * x[rs] + f + ple, g_ref[...], b_ref[...])


def conv_ffn_ple(x, p, layer, w_up, conv_w, conv_b, w_down, ple_w, gate_w, ln_g, ln_b, alpha, seq, tm=ROW_TILE):
    n = x.shape[0]
    tm = min(tm, seq)
    p_base = layer * (n // tm)
    halo = 8
    tiles_per_seq = seq // tm
    per = tm // halo
    last = n // halo - 1
    row = lambda i: (i, 0)
    kern = functools.partial(_ffn_kernel, alpha=alpha, tiles_per_seq=tiles_per_seq)
    return pl.pallas_call(
        kern,
        grid=(n // tm,),
        in_specs=[pl.BlockSpec((tm, D_MODEL), row),
                  pl.BlockSpec((halo, D_MODEL), lambda i: (jnp.maximum(i * per - 1, 0), 0)),
                  pl.BlockSpec((halo, D_MODEL), lambda i: (jnp.minimum((i + 1) * per, last), 0)),
                  pl.BlockSpec((tm, PLE_DIM), lambda i: (p_base + i, 0)),
                  _layer_spec((D_MODEL, 2 * D_FF), layer), _const_spec((3, D_FF)),
                  _const_spec((1, D_FF)),
                  _layer_spec((D_FF, D_MODEL), layer), _layer_spec((PLE_DIM, D_MODEL), layer),
                  _layer_spec((D_MODEL, D_MODEL), layer),
                  _const_spec((1, D_MODEL)), _const_spec((1, D_MODEL))],
        out_specs=pl.BlockSpec((tm, D_MODEL), row),
        out_shape=jax.ShapeDtypeStruct((n, D_MODEL), F32),
        scratch_shapes=[pltpu.VMEM((tm, D_FF), BF16)],
        compiler_params=_params(("parallel",)),
        name="conv_ffn_ple",
    )(x, x, x, p, w_up, conv_w, conv_b, w_down, ple_w, gate_w, ln_g, ln_b)


def kernel(x, p, positions, w_in, da_lambda_q1, da_lambda_k1, da_lambda_q2, da_lambda_k2,
           da_subln_g, ret_gn_g, ret_gn_b, s5_A_re, s5_A_im, s5_log_dt, s5_B_re, s5_B_im,
           s5_C_re, s5_C_im, s5_D, s5_glu_w, s5_glu_b, w_out, ln1_g, ln1_b,
           ffn_w_up, ffn_conv_w, ffn_conv_b, ffn_w_down, ple_w, ple_gate_w, ln2_g, ln2_b):
    bsz, seq, _ = x.shape
    depth = w_in.shape[0]
    n = bsz * seq
    alpha = (2 * depth) ** 0.25
    cos, sin = rope_tables(positions)
    xf = x.reshape(n, D_MODEL)
    p_all = p.reshape(depth * n, PLE_DIM)
    mats = s5_matrices(s5_A_re, s5_A_im, s5_log_dt, s5_B_re, s5_B_im, s5_C_re, s5_C_im, s5_D)
    row = lambda v: v.reshape(1, -1).astype(F32)
    w_in_b, glu_w_b, w_out_b = w_in.astype(BF16), s5_glu_w.astype(BF16), w_out.astype(BF16)
    w_up_b, w_down_b = ffn_w_up.astype(BF16), ffn_w_down.astype(BF16)
    ple_w_b, gate_w_b = ple_w.astype(BF16), ple_gate_w.astype(BF16)
    for i in range(depth):
        lambda_init = 0.8 - 0.6 * math.exp(-0.3 * i)
        lam = (jnp.exp(jnp.sum(da_lambda_q1[i].astype(F32) * da_lambda_k1[i].astype(F32)))
               - jnp.exp(jnp.sum(da_lambda_q2[i].astype(F32) * da_lambda_k2[i].astype(F32)))
               + lambda_init)
        da, ret, g, u = in_proj(xf, w_in_b, i, cos, sin)
        y_da = diff_attention(da, lam.reshape(1, 1), da_subln_g[i].astype(F32).reshape(-1, 1),
                              1.0 - lambda_init, bsz, seq)
        y_ret = retention(ret, g, ret_gn_g[i].astype(F32), ret_gn_b[i].astype(F32), bsz, seq)
        y5 = s5_mixer(u, mats, i, bsz, seq)
        x1 = out_proj(xf, y_da, y_ret, y5, i, glu_w_b, row(s5_glu_b[i]), w_out_b,
                      row(ln1_g[i]), row(ln1_b[i]), alpha)
        xf = conv_ffn_ple(x1, p_all, i, w_up_b, ffn_conv_w[i].astype(F32), row(ffn_conv_b[i]),
                          w_down_b, ple_w_b, gate_w_b, row(ln2_g[i]), row(ln2_b[i]), alpha, seq)
    return xf.reshape(bsz, seq, D_MODEL)
```

```python
import functools
import math

import jax
import jax.numpy as jnp
from jax import lax
from jax.experimental import pallas as pl
from jax.experimental.pallas import tpu as pltpu

F32 = jnp.float32
BF16 = jnp.bfloat16

D_MODEL = 1024
PLE_DIM = 256
DA_HEADS = 4
DA_QK_DIM = 64
DA_V_DIM = 128
DA_WIDTH = DA_HEADS * DA_V_DIM
RET_HEADS = 4
RET_HEAD_DIM = 64
RET_WIDTH = RET_HEADS * RET_HEAD_DIM
S5_WIDTH = 256
S5_GROUP = 16
S5_GROUPS = S5_WIDTH // S5_GROUP
S5_STATE = 64
D_FF = 2816
ROPE_THETA = 10000.0
LN_EPS = 1e-5
RMS_EPS = 1e-6

COL_DA_Q = 0
COL_DA_K = COL_DA_Q + DA_HEADS * 2 * DA_QK_DIM
COL_DA_V = COL_DA_K + DA_HEADS * 2 * DA_QK_DIM
COL_RET_Q = COL_DA_V + DA_WIDTH
COL_RET_K = COL_RET_Q + RET_WIDTH
COL_RET_V = COL_RET_K + RET_WIDTH
COL_RET_G = COL_RET_V + RET_WIDTH
COL_S5_U = COL_RET_G + RET_WIDTH
IN_COLS = COL_S5_U + S5_WIDTH

LANES = 128
MXU_WIDTH = 256
S5_CHUNK = 8
S5_HALF = 128
RET_CHUNK = 256
RET_CHUNKS_PER_STEP = 4
FF_CHUNK = 256
ROW_TILE = 1024
EPILOGUE_ROWS = 256
LOG2E = 1.4426950408889634
NEG_BIG = -1e30
VMEM_LIMIT = 56 * 1024 * 1024


def _params(sem, vmem=VMEM_LIMIT):
    return pltpu.CompilerParams(dimension_semantics=sem, vmem_limit_bytes=vmem)


def _const_spec(shape):
    nd = len(shape)
    return pl.BlockSpec(shape, lambda *_: (0,) * nd, pipeline_mode=pl.Buffered(1))


def _layer_spec(shape, layer):
    nd = len(shape)
    return pl.BlockSpec((None,) + tuple(shape), lambda *_: (layer,) + (0,) * nd,
                        pipeline_mode=pl.Buffered(1))


def _layer_norm(x, g, b):
    mu = jnp.mean(x, axis=-1, keepdims=True)
    d = x - mu
    var = jnp.mean(d * d, axis=-1, keepdims=True)
    return d * lax.rsqrt(var + LN_EPS) * g + b


def _gelu_tanh(x):
    return 0.5 * x * (1.0 + jnp.tanh(math.sqrt(2.0 / math.pi) * (x + 0.044715 * (x * x * x))))


def _sigmoid(x):
    return 1.0 / (1.0 + jnp.exp(-x))


def _dot(a, b):
    return jnp.dot(a, b, preferred_element_type=F32)


def _dot_nt(a, b):
    return lax.dot_general(a, b, (((1,), (1,)), ((), ())), preferred_element_type=F32)


def _dot_tn(a, b):
    return lax.dot_general(a, b, (((0,), (0,)), ((), ())), preferred_element_type=F32)


def _dot_split(x, w):
    hi = x.astype(BF16)
    lo = (x - hi.astype(F32)).astype(BF16)
    return _dot(hi, w) + _dot(lo, w)


def _rope_table_kernel(pos_ref, freq_ref, sign_ref, cos_ref, sin_ref):
    ang = pos_ref[...].astype(F32) * freq_ref[...]
    cos_ref[...] = jnp.cos(ang)
    sin_ref[...] = jnp.sin(ang) * sign_ref[...]


def rope_tables(positions, tm=1024):
    n = positions.size
    half = DA_QK_DIM // 2
    inv_freq = ROPE_THETA ** (-jnp.arange(0, DA_QK_DIM, 2, dtype=F32) / DA_QK_DIM)
    freq_row = jnp.tile(inv_freq, LANES // half).reshape(1, LANES)
    lane = jnp.arange(LANES)
    sign_row = jnp.where(lane % DA_QK_DIM < half, -1.0, 1.0).astype(F32).reshape(1, LANES)
    pos = positions.reshape(n, 1)
    tm = min(tm, n)
    return pl.pallas_call(
        _rope_table_kernel,
        grid=(n // tm,),
        in_specs=[pl.BlockSpec((tm, 1), lambda i: (i, 0)),
                  _const_spec((1, LANES)), _const_spec((1, LANES))],
        out_specs=[pl.BlockSpec((tm, LANES), lambda i: (i, 0)),
                   pl.BlockSpec((tm, LANES), lambda i: (i, 0))],
        out_shape=[jax.ShapeDtypeStruct((n, LANES), F32)] * 2,
        compiler_params=_params(("parallel",)),
        name="rope_tables",
    )(pos, freq_row, sign_row)


def _rope(x, cos, sin, first_half):
    swapped = jnp.where(first_half, pltpu.roll(x, LANES - DA_QK_DIM // 2, 1),
                        pltpu.roll(x, DA_QK_DIM // 2, 1))
    return x * cos + swapped * sin


def _in_proj_kernel(x_ref, w_ref, cos_ref, sin_ref, da_ref, ret_ref, g_ref, u_ref):
    xb = x_ref[...].astype(BF16)
    cos = cos_ref[...]
    sin = sin_ref[...]
    lane = lax.broadcasted_iota(jnp.int32, cos.shape, 1)
    first_half = (lane % DA_QK_DIM) < (DA_QK_DIM // 2)
    q_scale = DA_QK_DIM ** -0.5 * LOG2E
    k_scale = RET_HEAD_DIM ** -0.5

    def proj(col):
        return _dot(xb, w_ref[:, col:col + MXU_WIDTH])

    def roped(z, scale):
        parts = [_rope(z[:, a:a + LANES], cos, sin, first_half) for a in (0, LANES)]
        out = jnp.concatenate(parts, axis=-1)
        return out if scale is None else out * scale

    for c in range(0, COL_DA_K, MXU_WIDTH):
        da_ref[:, c:c + MXU_WIDTH] = roped(proj(c), q_scale).astype(BF16)
    for c in range(COL_DA_K, COL_DA_V, MXU_WIDTH):
        da_ref[:, c:c + MXU_WIDTH] = roped(proj(c), None).astype(BF16)
    for c in range(COL_DA_V, COL_RET_Q, MXU_WIDTH):
        da_ref[:, c:c + MXU_WIDTH] = proj(c).astype(BF16)
    ret_ref[:, 0:RET_WIDTH] = roped(proj(COL_RET_Q), None).astype(BF16)
    ret_ref[:, RET_WIDTH:2 * RET_WIDTH] = roped(proj(COL_RET_K), k_scale).astype(BF16)
    ret_ref[:, 2 * RET_WIDTH:3 * RET_WIDTH] = proj(COL_RET_V).astype(BF16)
    g_ref[...] = proj(COL_RET_G)
    u = proj(COL_S5_U)
    u_ref[0] = u[:, :S5_HALF]
    u_ref[1] = u[:, S5_HALF:]


def in_proj(x, w_bf16, layer, cos, sin, tm=ROW_TILE):
    n = x.shape[0]
    tm = min(tm, n)
    row = lambda i: (i, 0)
    return pl.pallas_call(
        _in_proj_kernel,
        grid=(n // tm,),
        in_specs=[pl.BlockSpec((tm, D_MODEL), row), _layer_spec((D_MODEL, IN_COLS), layer),
                  pl.BlockSpec((tm, LANES), row), pl.BlockSpec((tm, LANES), row)],
        out_specs=[pl.BlockSpec((tm, COL_RET_Q), row), pl.BlockSpec((tm, 3 * RET_WIDTH), row),
                   pl.BlockSpec((tm, RET_WIDTH), row),
                   pl.BlockSpec((2, tm, S5_HALF), lambda i: (0, i, 0))],
        out_shape=[jax.ShapeDtypeStruct((n, COL_RET_Q), BF16),
                   jax.ShapeDtypeStruct((n, 3 * RET_WIDTH), BF16),
                   jax.ShapeDtypeStruct((n, RET_WIDTH), F32),
                   jax.ShapeDtypeStruct((2, n, S5_HALF), F32)],
        compiler_params=_params(("parallel",)),
        name="in_proj",
    )(x, w_bf16, cos, sin)


ONES_ROWS = 16


def _diff_attn_kernel(q_ref, k_ref, v_ref, lam_ref, g_ref, o_ref,
                      km_ref, vt_ref, m_ref, acc_ref, s_ref, *, tq, tk, out_scale):
    seq = k_ref.shape[0]
    n_kv = seq // tk
    n_q = seq // tq
    k_all = k_ref[...]
    lane = lax.broadcasted_iota(jnp.int32, k_all.shape, 1)
    zero = jnp.zeros_like(k_all)
    km_ref[0] = jnp.where(lane < DA_QK_DIM, k_all, zero)
    km_ref[1] = jnp.where(lane >= DA_QK_DIM, k_all, zero)
    for i in range(n_kv):
        vt_ref[i, :DA_V_DIM, :] = v_ref[i * tk:(i + 1) * tk, :].astype(F32).T.astype(BF16)
        vt_ref[i, DA_V_DIM:, :] = jnp.ones((ONES_ROWS, tk), BF16)

    def scores(qi, ki, slot):
        q = q_ref[pl.ds(pl.multiple_of(qi * tq, tq), tq), :]
        rows = pl.ds(pl.multiple_of(ki * tk, tk), tk)
        for mi in range(2):
            s_ref[slot, mi] = _dot_nt(km_ref[mi, rows, :], q)

    def consume(ki, slot):
        vt = vt_ref[ki]
        for mi in range(2):
            s = s_ref[slot, mi]
            m_old = m_ref[mi]
            m_new = jnp.maximum(m_old, jnp.max(s, axis=0, keepdims=True))
            alpha = jnp.exp2(m_old - m_new)
            p = jnp.exp2(s - m_new).astype(BF16)
            acc_ref[mi] = acc_ref[mi] * alpha + _dot(vt, p)
            m_ref[mi] = m_new

    scores(0, 0, 0)

    def q_tile(qi, carry):
        m_ref[...] = jnp.full(m_ref.shape, NEG_BIG, F32)
        acc_ref[...] = jnp.zeros(acc_ref.shape, F32)

        def kv_pair(j, c):
            for u in range(2):
                ki = 2 * j + u
                wrap = ki + 1 >= n_kv
                scores(jnp.where(wrap, jnp.minimum(qi + 1, n_q - 1), qi),
                       jnp.where(wrap, 0, ki + 1), 1 - u)
                consume(ki, u)
            return c

        lax.fori_loop(0, n_kv // 2, kv_pair, 0)
        a0 = acc_ref[0]
        a1 = acc_ref[1]
        o = (a0[:DA_V_DIM] / a0[DA_V_DIM:DA_V_DIM + 1]
             - lam_ref[...] * (a1[:DA_V_DIM] / a1[DA_V_DIM:DA_V_DIM + 1]))
        ms = jnp.mean(o * o, axis=0, keepdims=True)
        o = o * lax.rsqrt(ms + RMS_EPS) * g_ref[...] * out_scale
        rows = pl.ds(pl.multiple_of(qi * tq, tq), tq)
        o_ref[rows, :] = o.T.astype(o_ref.dtype)
        return carry

    lax.fori_loop(0, n_q, q_tile, 0)


def diff_attention(da, lam, subln_col, out_scale, bsz, seq, tq=256, tk=4096):
    tq = min(tq, seq)
    tk = min(tk, seq // 2)
    assert seq % (2 * tk) == 0 and seq % tq == 0
    kern = functools.partial(_diff_attn_kernel, tq=tq, tk=tk, out_scale=out_scale)
    head_block = lambda part: pl.BlockSpec((seq, LANES), lambda b, h: (b, part * DA_HEADS + h))
    vrows = DA_V_DIM + ONES_ROWS
    return pl.pallas_call(
        kern,
        grid=(bsz, DA_HEADS),
        in_specs=[head_block(0), head_block(1), head_block(2),
                  _const_spec((1, 1)), _const_spec((DA_V_DIM, 1))],
        out_specs=pl.BlockSpec((seq, DA_V_DIM), lambda b, h: (b, h)),
        out_shape=jax.ShapeDtypeStruct((bsz * seq, DA_WIDTH), BF16),
        scratch_shapes=[pltpu.VMEM((2, seq, LANES), BF16),
                        pltpu.VMEM((seq // tk, vrows, tk), BF16),
                        pltpu.VMEM((2, 1, tq), F32),
                        pltpu.VMEM((2, vrows, tq), F32),
                        pltpu.VMEM((2, 2, tk, tq), F32)],
        compiler_params=_params(("parallel", "parallel")),
        name="diff_attention",
    )(da, da, da, lam, subln_col)


def _ret_state_kernel(kf_ref, vf_ref, kb_ref, vb_ref, dkf_ref, dkb_ref, gc_ref,
                      sf_out, sb_out, sf_ref, sb_ref):
    @pl.when(pl.program_id(1) == 0)
    def _():
        sf_ref[...] = jnp.zeros(sf_ref.shape, F32)
        sb_ref[...] = jnp.zeros(sb_ref.shape, F32)

    c = dkf_ref.shape[0]
    per_step = sf_out.shape[0]

    def update(s_ref, k, v, dk_ref):
        kd = (k.astype(F32) * dk_ref[...]).astype(BF16)
        for pr in range(RET_WIDTH // LANES):
            sl = slice(pr * LANES, (pr + 1) * LANES)
            kv = _dot_tn(kd[:, sl], v[:, sl])
            keep = gc_ref[sl, :]
            s_ref[sl, :] = keep * s_ref[sl, :] + jnp.where(keep > 0.0, kv, 0.0)

    for u in range(per_step):
        lo = slice(u * c, (u + 1) * c)
        hi = slice((per_step - 1 - u) * c, (per_step - u) * c)
        sf_out[u] = sf_ref[...]
        sb_out[per_step - 1 - u] = sb_ref[...]
        update(sf_ref, kf_ref[lo, :], vf_ref[lo, :], dkf_ref)
        update(sb_ref, kb_ref[hi, :], vb_ref[hi, :], dkb_ref)


def _ret_out_kernel(q_ref, k_ref, v_ref, g_ref, sf_ref, sb_ref, dec_ref, dqf_ref, dqb_ref,
                    avg_ref, gng_ref, gnb_ref, o_ref):
    c = dqf_ref.shape[0]
    chunks = range(sf_ref.shape[0])
    pairs = range(RET_WIDTH // LANES)
    halves = range(LANES // RET_HEAD_DIM)
    lane = lax.broadcasted_iota(jnp.int32, (c, LANES), 1)
    zero = jnp.zeros((c, LANES), BF16)
    mine = [(lane >= hh * RET_HEAD_DIM) & (lane < (hh + 1) * RET_HEAD_DIM) for hh in halves]
    avg = avg_ref[...]
    rows = [slice(u * c, (u + 1) * c) for u in chunks]
    cols = [slice(pr * LANES, (pr + 1) * LANES) for pr in pairs]
    scores, carried = {}, {}
    for u in chunks:
        q = q_ref[rows[u], :]
        k = k_ref[rows[u], :]
        qf = q.astype(F32)
        for pr in pairs:
            sl = cols[pr]
            carried[u, pr] = (
                _dot((qf[:, sl] * dqf_ref[:, sl]).astype(BF16), sf_ref[u, sl, :].astype(BF16))
                + _dot((qf[:, sl] * dqb_ref[:, sl]).astype(BF16), sb_ref[u, sl, :].astype(BF16)))
            for hh in halves:
                scores[u, pr, hh] = _dot_nt(jnp.where(mine[hh], q[:, sl], zero), k[:, sl])
    outs = {}
    for u in chunks:
        v = v_ref[rows[u], :]
        parts = []
        for pr in pairs:
            acc = carried[u, pr]
            for hh in halves:
                s = (scores[u, pr, hh] * dec_ref[pr * 2 + hh]).astype(BF16)
                acc += _dot(s, jnp.where(mine[hh], v[:, cols[pr]], zero))
            parts.append(acc)
        outs[u] = jnp.concatenate(parts, axis=-1)
    mean = {u: _dot_split(outs[u], avg) for u in chunks}
    dev = {u: outs[u] - mean[u] for u in chunks}
    var = {u: _dot_split(dev[u] * dev[u], avg) for u in chunks}
    for u in chunks:
        y = dev[u] * lax.rsqrt(var[u] + LN_EPS) * gng_ref[...] + gnb_ref[...]
        g = g_ref[rows[u], :]
        o_ref[rows[u], :] = (g * _sigmoid(g) * y).astype(o_ref.dtype)


def retention(ret, g, gn_g, gn_b, bsz, seq):
    c = min(RET_CHUNK, seq)
    nc = seq // c
    heads = jnp.arange(RET_HEADS, dtype=F32)
    log_gamma = jnp.log(1.0 - 2.0 ** (-5.0 - heads))
    lg_cols = jnp.repeat(log_gamma, RET_HEAD_DIM)[None, :]
    idx = jnp.arange(c, dtype=F32)[:, None]
    dk_f = jnp.exp(lg_cols * (c - 1 - idx))
    dk_b = jnp.exp(lg_cols * idx)
    dq_f = jnp.exp(lg_cols * (idx + 1))
    dq_b = jnp.exp(lg_cols * (c - idx))
    dist = jnp.abs(idx - idx.T)
    decay = jnp.exp(log_gamma[:, None, None] * dist[None])
    row_head = jnp.arange(RET_WIDTH)[:, None] // RET_HEAD_DIM
    col_head = (jnp.arange(LANES)[None, :] // RET_HEAD_DIM) + 2 * (jnp.arange(RET_WIDTH)[:, None] // LANES)
    gc = jnp.where(row_head == col_head, jnp.exp(lg_cols.T * c), 0.0).astype(F32)
    seg = jnp.arange(RET_WIDTH) // RET_HEAD_DIM
    avg = jnp.where(seg[:, None] == seg[None, :], 1.0 / RET_HEAD_DIM, 0.0).astype(BF16)
    gng = jnp.tile(gn_g, RET_HEADS)[None, :]
    gnb = jnp.tile(gn_b, RET_HEADS)[None, :]

    per_step = RET_CHUNKS_PER_STEP if nc % RET_CHUNKS_PER_STEP == 0 else 1
    ns = nc // per_step
    rows = per_step * c
    blk = lambda col: pl.BlockSpec((rows, RET_WIDTH), lambda b, j: (b * ns + j, col))
    blk_rev = lambda col: pl.BlockSpec((rows, RET_WIDTH), lambda b, j: (b * ns + ns - 1 - j, col))
    st_shape = jax.ShapeDtypeStruct((bsz, nc, RET_WIDTH, LANES), F32)
    st_blk = (None, per_step, RET_WIDTH, LANES)
    sf, sb = pl.pallas_call(
        _ret_state_kernel,
        grid=(bsz, ns),
        in_specs=[blk(1), blk(2), blk_rev(1), blk_rev(2),
                  _const_spec((c, RET_WIDTH)), _const_spec((c, RET_WIDTH)),
                  _const_spec((RET_WIDTH, LANES))],
        out_specs=[pl.BlockSpec(st_blk, lambda b, j: (b, j, 0, 0)),
                   pl.BlockSpec(st_blk, lambda b, j: (b, ns - 1 - j, 0, 0))],
        out_shape=[st_shape, st_shape],
        scratch_shapes=[pltpu.VMEM((RET_WIDTH, LANES), F32), pltpu.VMEM((RET_WIDTH, LANES), F32)],
        compiler_params=_params(("parallel", "arbitrary")),
        name="retention_state",
    )(ret, ret, ret, ret, dk_f, dk_b, gc)

    st_spec = pl.BlockSpec(st_blk, lambda b, j: (b, j, 0, 0))
    return pl.pallas_call(
        _ret_out_kernel,
        grid=(bsz, ns),
        in_specs=[blk(0), blk(1), blk(2), blk(0), st_spec, st_spec,
                  _const_spec((RET_HEADS, c, c)), _const_spec((c, RET_WIDTH)),
                  _const_spec((c, RET_WIDTH)), _const_spec((RET_WIDTH, RET_WIDTH)),
                  _const_spec((1, RET_WIDTH)), _const_spec((1, RET_WIDTH))],
        out_specs=blk(0),
        out_shape=jax.ShapeDtypeStruct((bsz * seq, RET_WIDTH), BF16),
        compiler_params=_params(("parallel", "parallel")),
        name="retention_out",
    )(ret, ret, ret, g, sf, sb, decay, dq_f, dq_b, avg, gng, gnb)


def _cmul(ar, ai, br, bi):
    return ar * br - ai * bi, ar * bi + ai * br


def s5_matrices(A_re, A_im, log_dt, B_re, B_im, C_re, C_im, D):
    T, G, P, Cn = S5_CHUNK, S5_GROUPS, S5_STATE, S5_GROUP
    gh = S5_HALF // Cn
    depth = A_re.shape[0]
    step = jnp.exp(log_dt.astype(F32))[..., None]
    a_re = A_re.astype(F32)
    a_im = A_im.astype(F32)
    d = jnp.arange(T + 1, dtype=F32).reshape(T + 1, 1, 1, 1, 1)
    mag = jnp.exp(d * (step * a_re))
    pw_re = mag * jnp.cos(d * (step * a_im))
    pw_im = mag * jnp.sin(d * (step * a_im))
    den = a_re * a_re + a_im * a_im
    nr = pw_re[1] - 1.0
    ni = pw_im[1]
    coef_re = ((nr * a_re + ni * a_im) / den)[..., None]
    coef_im = ((ni * a_re - nr * a_im) / den)[..., None]
    b_re = B_re.astype(F32)
    b_im = B_im.astype(F32)
    bb_re = coef_re * b_re - coef_im * b_im
    bb_im = coef_re * b_im + coef_im * b_re
    c_re = C_re.astype(F32)
    c_im = C_im.astype(F32)
    t_idx = jnp.arange(T)
    fwd_in, bwd_in = T - 1 - t_idx, t_idx
    fwd_out, bwd_out = t_idx + 1, T - t_idx

    def state_in(direction, order):
        return _cmul(pw_re[order, :, direction, :, None, :], pw_im[order, :, direction, :, None, :],
                     jnp.swapaxes(bb_re[:, direction], -1, -2)[None],
                     jnp.swapaxes(bb_im[:, direction], -1, -2)[None])

    def read_out(direction, order):
        return _cmul(c_re[None, :, direction], c_im[None, :, direction],
                     pw_re[order, :, direction, :, None, :], pw_im[order, :, direction, :, None, :])

    def lag_kernel(direction):
        wr, wi = _cmul(pw_re[:T, :, direction, :, :, None], pw_im[:T, :, direction, :, :, None],
                       bb_re[None, :, direction], bb_im[None, :, direction])
        return (jnp.einsum('lgop,dlgpi->dlgoi', c_re[:, direction], wr)
                - jnp.einsum('lgop,dlgpi->dlgoi', c_im[:, direction], wi))

    def halves(v, lead):
        return v.reshape(v.shape[:lead] + (2, gh) + v.shape[lead + 1:])

    vfr, vfi = state_in(0, fwd_in)
    vbr, vbi = state_in(1, bwd_in)
    def spread(compact, n_inner, row_group):
        n_in = compact.shape[-1]
        n_out = n_in * gh
        src = jnp.arange(n_in)
        dst = jnp.arange(n_out)
        same = ((src[:, None] // n_inner == dst[None, :] // (gh * n_inner))
                & (src[:, None] % n_inner == dst[None, :] % n_inner))
        keep = row_group[:, None] == (dst[None, :] // n_inner) % gh
        wide = jnp.einsum('ldrk,kc->ldrc', compact, same.astype(BF16))
        return jnp.where(keep, wide, jnp.zeros((), BF16))

    in_rows = (jnp.arange(T * gh * Cn) // Cn) % gh
    state_rows = (jnp.arange(4 * gh * P) // P) % gh
    v_all = halves(jnp.stack([vfr, vfi, vbr, vbi]).astype(BF16), 3)
    v_compact = jnp.transpose(v_all, (2, 3, 1, 4, 5, 0, 6)).reshape(depth, 2, T * gh * Cn, 4 * P)
    mb = spread(v_compact, P, in_rows)

    efr, efi = read_out(0, fwd_out)
    ebr, ebi = read_out(1, bwd_out)
    e_all = halves(jnp.stack([efr, -efi, ebr, -ebi]).astype(BF16), 3)
    e_compact = jnp.transpose(e_all, (2, 3, 0, 4, 6, 1, 5)).reshape(depth, 2, 4 * gh * P, T * Cn)
    mc = spread(e_compact, Cn, state_rows)

    lag = t_idx[None, :] - t_idx[:, None]
    sel = lambda cond: cond[:, :, None, None, None, None]
    d_diag = D.astype(F32).reshape(depth, G, Cn)[..., None] * jnp.eye(Cn, dtype=F32)
    toe = (jnp.where(sel(lag >= 0), lag_kernel(0)[jnp.clip(lag, 0, T - 1)], 0.0)
           + jnp.where(sel(lag <= 0), lag_kernel(1)[jnp.clip(-lag, 0, T - 1)], 0.0)
           + jnp.where(sel(lag == 0), d_diag[None, None], 0.0))
    t_compact = jnp.transpose(halves(toe.astype(BF16), 3), (2, 3, 0, 4, 6, 1, 5)).reshape(
        depth, 2, T * gh * Cn, T * Cn)
    tp = spread(t_compact, Cn, in_rows)

    a_rows = jnp.stack([pw_re[T, :, 0], pw_im[T, :, 0], pw_re[T, :, 1], pw_im[T, :, 1]], axis=1)
    a8 = jnp.swapaxes(a_rows.reshape(depth, 4, 2, gh * P), 1, 2)
    a8 = jnp.concatenate([a8, a8], axis=2)
    return mb, tp, mc, a8


def _s5_kernel(u_ref, mb_ref, tp_ref, mc_ref, a8_ref, y_ref, u8_ref, w_ref, *, sub):
    rows = w_ref.shape[0]
    ns = a8_ref.shape[1]
    for r in range(0, rows, sub):
        steps = [u_ref[pl.ds(r * S5_CHUNK + s, sub, stride=S5_CHUNK), :].astype(BF16)
                 for s in range(S5_CHUNK)]
        u8 = jnp.concatenate(steps, axis=-1)
        u8_ref[r:r + sub, :] = u8
        w_ref[r:r + sub, :] = _dot(u8, mb_ref[...])

    def axpy(a, x, w):
        return a[0] * x[0] - a[1] * x[1] + w[0], a[0] * x[1] + a[1] * x[0] + w[1]

    zero = (jnp.zeros((1, ns), F32),) * 2
    a_f = (a8_ref[0:1, :], a8_ref[1:2, :])
    a_b = (a8_ref[2:3, :], a8_ref[3:4, :])
    a2_f = axpy(a_f, a_f, zero)
    a2_b = axpy(a_b, a_b, zero)

    def sweep(x, a, a2, r0, r1, col):
        re, im = slice(col, col + ns), slice(col + ns, col + 2 * ns)
        w0 = (w_ref[pl.ds(r0, 1), re], w_ref[pl.ds(r0, 1), im])
        w1 = (w_ref[pl.ds(r1, 1), re], w_ref[pl.ds(r1, 1), im])
        x1 = axpy(a, x, w0)
        w_ref[pl.ds(r0, 1), re] = x[0]
        w_ref[pl.ds(r0, 1), im] = x[1]
        w_ref[pl.ds(r1, 1), re] = x1[0]
        w_ref[pl.ds(r1, 1), im] = x1[1]
        return axpy(a2, x, axpy(a, w0, w1))

    def scan_step(i, carry):
        xf, xb = carry
        j = 2 * i
        jb = rows - 1 - j
        return sweep(xf, a_f, a2_f, j, j + 1, 0), sweep(xb, a_b, a2_b, jb, jb - 1, 2 * ns)

    lax.fori_loop(0, rows // 2, scan_step, (zero, zero))

    for r in range(0, rows, sub):
        y8 = (_dot(u8_ref[r:r + sub, :], tp_ref[...])
              + _dot(w_ref[r:r + sub, :].astype(BF16), mc_ref[...]))
        for t in range(S5_CHUNK):
            y_ref[pl.ds(r * S5_CHUNK + t, sub, stride=S5_CHUNK), :] = y8[:, t * S5_HALF:(t + 1) * S5_HALF]


def s5_mixer(u, mats, layer, bsz, seq):
    mb, tp, mc, a8 = mats
    n = bsz * seq
    rows = seq // S5_CHUNK
    width = S5_CHUNK * S5_HALF
    ns = (S5_HALF // S5_GROUP) * S5_STATE
    kern = functools.partial(_s5_kernel, sub=min(256, rows))
    wspec = lambda a: pl.BlockSpec((None, None) + a.shape[2:], lambda h, b: (layer, h, 0, 0),
                                   pipeline_mode=pl.Buffered(1))
    tokens = pl.BlockSpec((None, seq, S5_HALF), lambda h, b: (h, b, 0))
    return pl.pallas_call(
        kern,
        grid=(2, bsz),
        in_specs=[tokens, wspec(mb), wspec(tp), wspec(mc), wspec(a8)],
        out_specs=tokens,
        out_shape=jax.ShapeDtypeStruct((2, n, S5_HALF), F32),
        scratch_shapes=[pltpu.VMEM((rows, width), BF16), pltpu.VMEM((rows, 4 * ns), F32)],
        compiler_params=_params(("arbitrary", "arbitrary")),
        name="s5_mixer",
    )(u, mb, tp, mc, a8)


def _out_proj_kernel(x_ref, da_ref, ret_ref, y5_ref, gluw_ref, glub_ref, wout_ref,
                     g_ref, b_ref, o_ref, *, alpha):
    c1 = DA_WIDTH
    c2 = DA_WIDTH + RET_WIDTH
    for r in range(0, x_ref.shape[0], EPILOGUE_ROWS):
        rs = slice(r, r + EPILOGUE_ROWS)
        y = jnp.concatenate([y5_ref[0, rs, :], y5_ref[1, rs, :]], axis=-1)
        ya = _gelu_tanh(y)
        gate = _sigmoid(_dot(ya.astype(BF16), gluw_ref[...]) + glub_ref[...])
        ys5 = (ya * gate).astype(BF16)
        mix = (_dot(da_ref[rs, :], wout_ref[0:c1, :]) + _dot(ret_ref[rs, :], wout_ref[c1:c2, :])
               + _dot(ys5, wout_ref[c2:, :]))
        o_ref[rs, :] = _layer_norm(alpha * x_ref[rs, :] + mix, g_ref[...], b_ref[...])


def out_proj(x, y_da, y_ret, y5, layer, glu_w, glu_b, w_out, ln_g, ln_b, alpha, tm=ROW_TILE):
    n = x.shape[0]
    tm = min(tm, n)
    row = lambda i: (i, 0)
    return pl.pallas_call(
        functools.partial(_out_proj_kernel, alpha=alpha),
        grid=(n // tm,),
        in_specs=[pl.BlockSpec((tm, D_MODEL), row), pl.BlockSpec((tm, DA_WIDTH), row),
                  pl.BlockSpec((tm, RET_WIDTH), row),
                  pl.BlockSpec((2, tm, S5_HALF), lambda i: (0, i, 0)),
                  _layer_spec((S5_WIDTH, S5_WIDTH), layer), _const_spec((1, S5_WIDTH)),
                  _layer_spec((D_MODEL, D_MODEL), layer),
                  _const_spec((1, D_MODEL)), _const_spec((1, D_MODEL))],
        out_specs=pl.BlockSpec((tm, D_MODEL), row),
        out_shape=jax.ShapeDtypeStruct((n, D_MODEL), F32),
        compiler_params=_params(("parallel",)),
        name="out_proj",
    )(x, y_da, y_ret, y5, glu_w, glu_b, w_out, ln_g, ln_b)


def _ffn_kernel(x_ref, xp_ref, xn_ref, p_ref, wup_ref, cw_ref, cb_ref, wdn_ref,
                plew_ref, gatew_ref, g_ref, b_ref, o_ref, act_ref, *, alpha, tiles_per_seq):
    tm = x_ref.shape[0]
    i = pl.program_id(0)
    has_prev = ((i % tiles_per_seq) != 0).astype(F32)
    has_next = ((i % tiles_per_seq) != tiles_per_seq - 1).astype(F32)
    x = x_ref[...]
    xb = x.astype(BF16)
    xpb = xp_ref[...].astype(BF16)
    xnb = xn_ref[...].astype(BF16)
    halo = xp_ref.shape[0]
    row = lax.broadcasted_iota(jnp.int32, (halo, FF_CHUNK), 0)

    for c in range(0, D_FF, FF_CHUNK):
        wg = wup_ref[:, c:c + FF_CHUNK]
        gate = _dot(xb, wg)
        val = _dot(xb, wup_ref[:, D_FF + c:D_FF + c + FF_CHUNK])
        before = _dot(xpb, wg)[halo - 1:halo, :] * has_prev
        after = _dot(xnb, wg)[0:1, :] * has_next
        left = pltpu.roll(gate, 1, 0)
        left = jnp.concatenate([jnp.where(row == 0, before, left[:halo]), left[halo:]], axis=0)
        right = pltpu.roll(gate, tm - 1, 0)
        right = jnp.concatenate(
            [right[:tm - halo], jnp.where(row == halo - 1, after, right[tm - halo:])], axis=0)
        conv = (cw_ref[0:1, c:c + FF_CHUNK] * left + cw_ref[1:2, c:c + FF_CHUNK] * gate
                + cw_ref[2:3, c:c + FF_CHUNK] * right + cb_ref[:, c:c + FF_CHUNK])
        act_ref[:, c:c + FF_CHUNK] = (_gelu_tanh(conv) * val).astype(BF16)

    for r in range(0, tm, EPILOGUE_ROWS):
        rs = slice(r, r + EPILOGUE_ROWS)
        f = _dot(act_ref[rs, :], wdn_ref[...])
        ple = (_dot(p_ref[rs, :].astype(BF16), plew_ref[...])
               * _sigmoid(_dot(xb[rs], gatew_ref[...])))
        o_ref[rs, :] = _layer_norm(alpha * x[rs] + f + ple, g_ref[...], b_ref[...])


def conv_ffn_ple(x, p, layer, w_up, conv_w, conv_b, w_down, ple_w, gate_w, ln_g, ln_b, alpha, seq, tm=ROW_TILE):
    n = x.shape[0]
    tm = min(tm, seq)
    p_base = layer * (n // tm)
    halo = 8
    tiles_per_seq = seq // tm
    per = tm // halo
    last = n // halo - 1
    row = lambda i: (i, 0)
    kern = functools.partial(_ffn_kernel, alpha=alpha, tiles_per_seq=tiles_per_seq)
    return pl.pallas_call(
        kern,
        grid=(n // tm,),
        in_specs=[pl.BlockSpec((tm, D_MODEL), row),
                  pl.BlockSpec((halo, D_MODEL), lambda i: (jnp.maximum(i * per - 1, 0), 0)),
                  pl.BlockSpec((halo, D_MODEL), lambda i: (jnp.minimum((i + 1) * per, last), 0)),
                  pl.BlockSpec((tm, PLE_DIM), lambda i: (p_base + i, 0)),
                  _layer_spec((D_MODEL, 2 * D_FF), layer), _const_spec((3, D_FF)),
                  _const_spec((1, D_FF)),
                  _layer_spec((D_FF, D_MODEL), layer), _layer_spec((PLE_DIM, D_MODEL), layer),
                  _layer_spec((D_MODEL, D_MODEL), layer),
                  _const_spec((1, D_MODEL)), _const_spec((1, D_MODEL))],
        out_specs=pl.BlockSpec((tm, D_MODEL), row),
        out_shape=jax.ShapeDtypeStruct((n, D_MODEL), F32),
        scratch_shapes=[pltpu.VMEM((tm, D_FF), BF16)],
        compiler_params=_params(("parallel",)),
        name="conv_ffn_ple",
    )(x, x, x, p, w_up, conv_w, conv_b, w_down, ple_w, gate_w, ln_g, ln_b)


def kernel(x, p, positions, w_in, da_lambda_q1, da_lambda_k1, da_lambda_q2, da_lambda_k2,
           da_subln_g, ret_gn_g, ret_gn_b, s5_A_re, s5_A_im, s5_log_dt, s5_B_re, s5_B_im,
           s5_C_re, s5_C_im, s5_D, s5_glu_w, s5_glu_b, w_out, ln1_g, ln1_b,
           ffn_w_up, ffn_conv_w, ffn_conv_b, ffn_w_down, ple_w, ple_gate_w, ln2_g, ln2_b):
    bsz, seq, _ = x.shape
    depth = w_in.shape[0]
    n = bsz * seq
    alpha = (2 * depth) ** 0.25
    cos, sin = rope_tables(positions)
    xf = x.reshape(n, D_MODEL)
    p_all = p.reshape(depth * n, PLE_DIM)
    mats = s5_matrices(s5_A_re, s5_A_im, s5_log_dt, s5_B_re, s5_B_im, s5_C_re, s5_C_im, s5_D)
    row = lambda v: v.reshape(1, -1).astype(F32)
    w_in_b, glu_w_b, w_out_b = w_in.astype(BF16), s5_glu_w.astype(BF16), w_out.astype(BF16)
    w_up_b, w_down_b = ffn_w_up.astype(BF16), ffn_w_down.astype(BF16)
    ple_w_b, gate_w_b = ple_w.astype(BF16), ple_gate_w.astype(BF16)
    for i in range(depth):
        lambda_init = 0.8 - 0.6 * math.exp(-0.3 * i)
        lam = (jnp.exp(jnp.sum(da_lambda_q1[i].astype(F32) * da_lambda_k1[i].astype(F32)))
               - jnp.exp(jnp.sum(da_lambda_q2[i].astype(F32) * da_lambda_k2[i].astype(F32)))
               + lambda_init)
        da, ret, g, u = in_proj(xf, w_in_b, i, cos, sin)
        y_da = diff_attention(da, lam.reshape(1, 1), da_subln_g[i].astype(F32).reshape(-1, 1),
                              1.0 - lambda_init, bsz, seq)
        y_ret = retention(ret, g, ret_gn_g[i].astype(F32), ret_gn_b[i].astype(F32), bsz, seq)
        y5 = s5_mixer(u, mats, i, bsz, seq)
        x1 = out_proj(xf, y_da, y_ret, y5, i, glu_w_b, row(s5_glu_b[i]), w_out_b,
                      row(ln1_g[i]), row(ln1_b[i]), alpha)
        xf = conv_ffn_ple(x1, p_all, i, w_up_b, ffn_conv_w[i].astype(F32), row(ffn_conv_b[i]),
                          w_down_b, ple_w_b, gate_w_b, row(ln2_g[i]), row(ln2_b[i]), alpha, seq)
    return xf.reshape(bsz, seq, D_MODEL)
```

```python
import functools
import math

import jax
import jax.numpy as jnp
from jax import lax
from jax.experimental import pallas as pl
from jax.experimental.pallas import tpu as pltpu

F32 = jnp.float32
BF16 = jnp.bfloat16

D_MODEL = 1024
PLE_DIM = 256
DA_HEADS = 4
DA_QK_DIM = 64
DA_V_DIM = 128
DA_WIDTH = DA_HEADS * DA_V_DIM
RET_HEADS = 4
RET_HEAD_DIM = 64
RET_WIDTH = RET_HEADS * RET_HEAD_DIM
S5_WIDTH = 256
S5_GROUP = 16
S5_GROUPS = S5_WIDTH // S5_GROUP
S5_STATE = 64
D_FF = 2816
ROPE_THETA = 10000.0
LN_EPS = 1e-5
RMS_EPS = 1e-6

COL_DA_Q = 0
COL_DA_K = COL_DA_Q + DA_HEADS * 2 * DA_QK_DIM
COL_DA_V = COL_DA_K + DA_HEADS * 2 * DA_QK_DIM
COL_RET_Q = COL_DA_V + DA_WIDTH
COL_RET_K = COL_RET_Q + RET_WIDTH
COL_RET_V = COL_RET_K + RET_WIDTH
COL_RET_G = COL_RET_V + RET_WIDTH
COL_S5_U = COL_RET_G + RET_WIDTH
IN_COLS = COL_S5_U + S5_WIDTH

LANES = 128
MXU_WIDTH = 256
S5_CHUNK = 8
S5_HALF = 128
RET_CHUNK = 256
RET_CHUNKS_PER_STEP = 4
FF_CHUNK = 256
ROW_TILE = 1024
EPILOGUE_ROWS = 256
LOG2E = 1.4426950408889634
NEG_BIG = -1e30
VMEM_LIMIT = 56 * 1024 * 1024


def _params(sem, vmem=VMEM_LIMIT):
    return pltpu.CompilerParams(dimension_semantics=sem, vmem_limit_bytes=vmem)


def _const_spec(shape):
    nd = len(shape)
    return pl.BlockSpec(shape, lambda *_: (0,) * nd, pipeline_mode=pl.Buffered(1))


def _layer_spec(shape, layer):
    nd = len(shape)
    return pl.BlockSpec((None,) + tuple(shape), lambda *_: (layer,) + (0,) * nd,
                        pipeline_mode=pl.Buffered(1))


def _layer_norm(x, g, b):
    mu = jnp.mean(x, axis=-1, keepdims=True)
    d = x - mu
    var = jnp.mean(d * d, axis=-1, keepdims=True)
    return d * lax.rsqrt(var + LN_EPS) * g + b


def _gelu_tanh(x):
    return 0.5 * x * (1.0 + jnp.tanh(math.sqrt(2.0 / math.pi) * (x + 0.044715 * (x * x * x))))


def _sigmoid(x):
    return 1.0 / (1.0 + jnp.exp(-x))


def _dot(a, b):
    return jnp.dot(a, b, preferred_element_type=F32)


def _dot_nt(a, b):
    return lax.dot_general(a, b, (((1,), (1,)), ((), ())), preferred_element_type=F32)


def _dot_tn(a, b):
    return lax.dot_general(a, b, (((0,), (0,)), ((), ())), preferred_element_type=F32)


def _dot_split(x, w):
    hi = x.astype(BF16)
    lo = (x - hi.astype(F32)).astype(BF16)
    return _dot(hi, w) + _dot(lo, w)


def _rope_table_kernel(pos_ref, freq_ref, sign_ref, cos_ref, sin_ref):
    ang = pos_ref[...].astype(F32) * freq_ref[...]
    cos_ref[...] = jnp.cos(ang)
    sin_ref[...] = jnp.sin(ang) * sign_ref[...]


def rope_tables(positions, tm=1024):
    n = positions.size
    half = DA_QK_DIM // 2
    inv_freq = ROPE_THETA ** (-jnp.arange(0, DA_QK_DIM, 2, dtype=F32) / DA_QK_DIM)
    freq_row = jnp.tile(inv_freq, LANES // half).reshape(1, LANES)
    lane = jnp.arange(LANES)
    sign_row = jnp.where(lane % DA_QK_DIM < half, -1.0, 1.0).astype(F32).reshape(1, LANES)
    pos = positions.reshape(n, 1)
    tm = min(tm, n)
    return pl.pallas_call(
        _rope_table_kernel,
        grid=(n // tm,),
        in_specs=[pl.BlockSpec((tm, 1), lambda i: (i, 0)),
                  _const_spec((1, LANES)), _const_spec((1, LANES))],
        out_specs=[pl.BlockSpec((tm, LANES), lambda i: (i, 0)),
                   pl.BlockSpec((tm, LANES), lambda i: (i, 0))],
        out_shape=[jax.ShapeDtypeStruct((n, LANES), F32)] * 2,
        compiler_params=_params(("parallel",)),
        name="rope_tables",
    )(pos, freq_row, sign_row)


def _rope(x, cos, sin, first_half):
    swapped = jnp.where(first_half, pltpu.roll(x, LANES - DA_QK_DIM // 2, 1),
                        pltpu.roll(x, DA_QK_DIM // 2, 1))
    return x * cos + swapped * sin


def _in_proj_kernel(x_ref, w_ref, cos_ref, sin_ref, da_ref, ret_ref, g_ref, u_ref):
    xb = x_ref[...].astype(BF16)
    cos = cos_ref[...]
    sin = sin_ref[...]
    lane = lax.broadcasted_iota(jnp.int32, cos.shape, 1)
    first_half = (lane % DA_QK_DIM) < (DA_QK_DIM // 2)
    q_scale = DA_QK_DIM ** -0.5 * LOG2E
    k_scale = RET_HEAD_DIM ** -0.5

    def proj(col):
        return _dot(xb, w_ref[:, col:col + MXU_WIDTH])

    def roped(z, scale):
        parts = [_rope(z[:, a:a + LANES], cos, sin, first_half) for a in (0, LANES)]
        out = jnp.concatenate(parts, axis=-1)
        return out if scale is None else out * scale

    for c in range(0, COL_DA_K, MXU_WIDTH):
        da_ref[:, c:c + MXU_WIDTH] = roped(proj(c), q_scale).astype(BF16)
    for c in range(COL_DA_K, COL_DA_V, MXU_WIDTH):
        da_ref[:, c:c + MXU_WIDTH] = roped(proj(c), None).astype(BF16)
    for c in range(COL_DA_V, COL_RET_Q, MXU_WIDTH):
        da_ref[:, c:c + MXU_WIDTH] = proj(c).astype(BF16)
    ret_ref[:, 0:RET_WIDTH] = roped(proj(COL_RET_Q), None).astype(BF16)
    ret_ref[:, RET_WIDTH:2 * RET_WIDTH] = roped(proj(COL_RET_K), k_scale).astype(BF16)
    ret_ref[:, 2 * RET_WIDTH:3 * RET_WIDTH] = proj(COL_RET_V).astype(BF16)
    g_ref[...] = proj(COL_RET_G)
    u = proj(COL_S5_U)
    u_ref[0] = u[:, :S5_HALF]
    u_ref[1] = u[:, S5_HALF:]


def in_proj(x, w_bf16, layer, cos, sin, tm=ROW_TILE):
    n = x.shape[0]
    tm = min(tm, n)
    row = lambda i: (i, 0)
    return pl.pallas_call(
        _in_proj_kernel,
        grid=(n // tm,),
        in_specs=[pl.BlockSpec((tm, D_MODEL), row), _layer_spec((D_MODEL, IN_COLS), layer),
                  pl.BlockSpec((tm, LANES), row), pl.BlockSpec((tm, LANES), row)],
        out_specs=[pl.BlockSpec((tm, COL_RET_Q), row), pl.BlockSpec((tm, 3 * RET_WIDTH), row),
                   pl.BlockSpec((tm, RET_WIDTH), row),
                   pl.BlockSpec((2, tm, S5_HALF), lambda i: (0, i, 0))],
        out_shape=[jax.ShapeDtypeStruct((n, COL_RET_Q), BF16),
                   jax.ShapeDtypeStruct((n, 3 * RET_WIDTH), BF16),
                   jax.ShapeDtypeStruct((n, RET_WIDTH), F32),
                   jax.ShapeDtypeStruct((2, n, S5_HALF), F32)],
        compiler_params=_params(("parallel",)),
        name="in_proj",
    )(x, w_bf16, cos, sin)


ONES_ROWS = 16


def _diff_attn_kernel(q_ref, k_ref, v_ref, lam_ref, g_ref, o_ref,
                      km_ref, vt_ref, m_ref, acc_ref, s_ref, *, tq, tk, out_scale):
    seq = k_ref.shape[0]
    n_kv = seq // tk
    n_q = seq // tq
    k_all = k_ref[...]
    lane = lax.broadcasted_iota(jnp.int32, k_all.shape, 1)
    zero = jnp.zeros_like(k_all)
    km_ref[0] = jnp.where(lane < DA_QK_DIM, k_all, zero)
    km_ref[1] = jnp.where(lane >= DA_QK_DIM, k_all, zero)
    for i in range(n_kv):
        vt_ref[i, :DA_V_DIM, :] = v_ref[i * tk:(i + 1) * tk, :].astype(F32).T.astype(BF16)
        vt_ref[i, DA_V_DIM:, :] = jnp.ones((ONES_ROWS, tk), BF16)

    def scores(qi, ki, slot):
        q = q_ref[pl.ds(pl.multiple_of(qi * tq, tq), tq), :]
        rows = pl.ds(pl.multiple_of(ki * tk, tk), tk)
        for mi in range(2):
            s_ref[slot, mi] = _dot_nt(km_ref[mi, rows, :], q)

    def consume(ki, slot):
        vt = vt_ref[ki]
        for mi in range(2):
            s = s_ref[slot, mi]
            m_old = m_ref[mi]
            m_new = jnp.maximum(m_old, jnp.max(s, axis=0, keepdims=True))
            alpha = jnp.exp2(m_old - m_new)
            p = jnp.exp2(s - m_new).astype(BF16)
            acc_ref[mi] = acc_ref[mi] * alpha + _dot(vt, p)
            m_ref[mi] = m_new

    scores(0, 0, 0)

    def q_tile(qi, carry):
        m_ref[...] = jnp.full(m_ref.shape, NEG_BIG, F32)
        acc_ref[...] = jnp.zeros(acc_ref.shape, F32)

        def kv_pair(j, c):
            for u in range(2):
                ki = 2 * j + u
                wrap = ki + 1 >= n_kv
                scores(jnp.where(wrap, jnp.minimum(qi + 1, n_q - 1), qi),
                       jnp.where(wrap, 0, ki + 1), 1 - u)
                consume(ki, u)
            return c

        lax.fori_loop(0, n_kv // 2, kv_pair, 0)
        a0 = acc_ref[0]
        a1 = acc_ref[1]
        o = (a0[:DA_V_DIM] / a0[DA_V_DIM:DA_V_DIM + 1]
             - lam_ref[...] * (a1[:DA_V_DIM] / a1[DA_V_DIM:DA_V_DIM + 1]))
        ms = jnp.mean(o * o, axis=0, keepdims=True)
        o = o * lax.rsqrt(ms + RMS_EPS) * g_ref[...] * out_scale
        rows = pl.ds(pl.multiple_of(qi * tq, tq), tq)
        o_ref[rows, :] = o.T.astype(o_ref.dtype)
        return carry

    lax.fori_loop(0, n_q, q_tile, 0)


def diff_attention(da, lam, subln_col, out_scale, bsz, seq, tq=256, tk=4096):
    tq = min(tq, seq)
    tk = min(tk, seq // 2)
    assert seq % (2 * tk) == 0 and seq % tq == 0
    kern = functools.partial(_diff_attn_kernel, tq=tq, tk=tk, out_scale=out_scale)
    head_block = lambda part: pl.BlockSpec((seq, LANES), lambda b, h: (b, part * DA_HEADS + h))
    vrows = DA_V_DIM + ONES_ROWS
    return pl.pallas_call(
        kern,
        grid=(bsz, DA_HEADS),
        in_specs=[head_block(0), head_block(1), head_block(2),
                  _const_spec((1, 1)), _const_spec((DA_V_DIM, 1))],
        out_specs=pl.BlockSpec((seq, DA_V_DIM), lambda b, h: (b, h)),
        out_shape=jax.ShapeDtypeStruct((bsz * seq, DA_WIDTH), BF16),
        scratch_shapes=[pltpu.VMEM((2, seq, LANES), BF16),
                        pltpu.VMEM((seq // tk, vrows, tk), BF16),
                        pltpu.VMEM((2, 1, tq), F32),
                        pltpu.VMEM((2, vrows, tq), F32),
                        pltpu.VMEM((2, 2, tk, tq), F32)],
        compiler_params=_params(("parallel", "parallel")),
        name="diff_attention",
    )(da, da, da, lam, subln_col)


def _ret_state_kernel(kf_ref, vf_ref, kb_ref, vb_ref, dkf_ref, dkb_ref, gc_ref,
                      sf_out, sb_out, sf_ref, sb_ref):
    @pl.when(pl.program_id(1) == 0)
    def _():
        sf_ref[...] = jnp.zeros(sf_ref.shape, F32)
        sb_ref[...] = jnp.zeros(sb_ref.shape, F32)

    c = dkf_ref.shape[0]
    per_step = sf_out.shape[0]

    def update(s_ref, k, v, dk_ref):
        kd = (k.astype(F32) * dk_ref[...]).astype(BF16)
        for pr in range(RET_WIDTH // LANES):
            sl = slice(pr * LANES, (pr + 1) * LANES)
            kv = _dot_tn(kd[:, sl], v[:, sl])
            keep = gc_ref[sl, :]
            s_ref[sl, :] = keep * s_ref[sl, :] + jnp.where(keep > 0.0, kv, 0.0)

    for u in range(per_step):
        lo = slice(u * c, (u + 1) * c)
        hi = slice((per_step - 1 - u) * c, (per_step - u) * c)
        sf_out[u] = sf_ref[...]
        sb_out[per_step - 1 - u] = sb_ref[...]
        update(sf_ref, kf_ref[lo, :], vf_ref[lo, :], dkf_ref)
        update(sb_ref, kb_ref[hi, :], vb_ref[hi, :], dkb_ref)


def _ret_out_kernel(q_ref, k_ref, v_ref, g_ref, sf_ref, sb_ref, dec_ref, dqf_ref, dqb_ref,
                    avg_ref, gng_ref, gnb_ref, o_ref):
    c = dqf_ref.shape[0]
    chunks = range(sf_ref.shape[0])
    pairs = range(RET_WIDTH // LANES)
    halves = range(LANES // RET_HEAD_DIM)
    lane = lax.broadcasted_iota(jnp.int32, (c, LANES), 1)
    zero = jnp.zeros((c, LANES), BF16)
    mine = [(lane >= hh * RET_HEAD_DIM) & (lane < (hh + 1) * RET_HEAD_DIM) for hh in halves]
    avg = avg_ref[...]
    rows = [slice(u * c, (u + 1) * c) for u in chunks]
    cols = [slice(pr * LANES, (pr + 1) * LANES) for pr in pairs]
    scores, carried = {}, {}
    for u in chunks:
        q = q_ref[rows[u], :]
        k = k_ref[rows[u], :]
        qf = q.astype(F32)
        for pr in pairs:
            sl = cols[pr]
            carried[u, pr] = (
                _dot((qf[:, sl] * dqf_ref[:, sl]).astype(BF16), sf_ref[u, sl, :].astype(BF16))
                + _dot((qf[:, sl] * dqb_ref[:, sl]).astype(BF16), sb_ref[u, sl, :].astype(BF16)))
            for hh in halves:
                scores[u, pr, hh] = _dot_nt(jnp.where(mine[hh], q[:, sl], zero), k[:, sl])
    outs = {}
    for u in chunks:
        v = v_ref[rows[u], :]
        parts = []
        for pr in pairs:
            acc = carried[u, pr]
            for hh in halves:
                s = (scores[u, pr, hh] * dec_ref[pr * 2 + hh]).astype(BF16)
                acc += _dot(s, jnp.where(mine[hh], v[:, cols[pr]], zero))
            parts.append(acc)
        outs[u] = jnp.concatenate(parts, axis=-1)
    mean = {u: _dot_split(outs[u], avg) for u in chunks}
    dev = {u: outs[u] - mean[u] for u in chunks}
    var = {u: _dot_split(dev[u] * dev[u], avg) for u in chunks}
    for u in chunks:
        y = dev[u] * lax.rsqrt(var[u] + LN_EPS) * gng_ref[...] + gnb_ref[...]
        g = g_ref[rows[u], :]
        o_ref[rows[u], :] = (g * _sigmoid(g) * y).astype(o_ref.dtype)


def retention(ret, g, gn_g, gn_b, bsz, seq):
    c = min(RET_CHUNK, seq)
    nc = seq // c
    heads = jnp.arange(RET_HEADS, dtype=F32)
    log_gamma = jnp.log(1.0 - 2.0 ** (-5.0 - heads))
    lg_cols = jnp.repeat(log_gamma, RET_HEAD_DIM)[None, :]
    idx = jnp.arange(c, dtype=F32)[:, None]
    dk_f = jnp.exp(lg_cols * (c - 1 - idx))
    dk_b = jnp.exp(lg_cols * idx)
    dq_f = jnp.exp(lg_cols * (idx + 1))
    dq_b = jnp.exp(lg_cols * (c - idx))
    dist = jnp.abs(idx - idx.T)
    decay = jnp.exp(log_gamma[:, None, None] * dist[None])
    row_head = jnp.arange(RET_WIDTH)[:, None] // RET_HEAD_DIM
    col_head = (jnp.arange(LANES)[None, :] // RET_HEAD_DIM) + 2 * (jnp.arange(RET_WIDTH)[:, None] // LANES)
    gc = jnp.where(row_head == col_head, jnp.exp(lg_cols.T * c), 0.0).astype(F32)
    seg = jnp.arange(RET_WIDTH) // RET_HEAD_DIM
    avg = jnp.where(seg[:, None] == seg[None, :], 1.0 / RET_HEAD_DIM, 0.0).astype(BF16)
    gng = jnp.tile(gn_g, RET_HEADS)[None, :]
    gnb = jnp.tile(gn_b, RET_HEADS)[None, :]

    per_step = RET_CHUNKS_PER_STEP if nc % RET_CHUNKS_PER_STEP == 0 else 1
    ns = nc // per_step
    rows = per_step * c
    blk = lambda col: pl.BlockSpec((rows, RET_WIDTH), lambda b, j: (b * ns + j, col))
    blk_rev = lambda col: pl.BlockSpec((rows, RET_WIDTH), lambda b, j: (b * ns + ns - 1 - j, col))
    st_shape = jax.ShapeDtypeStruct((bsz, nc, RET_WIDTH, LANES), F32)
    st_blk = (None, per_step, RET_WIDTH, LANES)
    sf, sb = pl.pallas_call(
        _ret_state_kernel,
        grid=(bsz, ns),
        in_specs=[blk(1), blk(2), blk_rev(1), blk_rev(2),
                  _const_spec((c, RET_WIDTH)), _const_spec((c, RET_WIDTH)),
                  _const_spec((RET_WIDTH, LANES))],
        out_specs=[pl.BlockSpec(st_blk, lambda b, j: (b, j, 0, 0)),
                   pl.BlockSpec(st_blk, lambda b, j: (b, ns - 1 - j, 0, 0))],
        out_shape=[st_shape, st_shape],
        scratch_shapes=[pltpu.VMEM((RET_WIDTH, LANES), F32), pltpu.VMEM((RET_WIDTH, LANES), F32)],
        compiler_params=_params(("parallel", "arbitrary")),
        name="retention_state",
    )(ret, ret, ret, ret, dk_f, dk_b, gc)

    st_spec = pl.BlockSpec(st_blk, lambda b, j: (b, j, 0, 0))
    return pl.pallas_call(
        _ret_out_kernel,
        grid=(bsz, ns),
        in_specs=[blk(0), blk(1), blk(2), blk(0), st_spec, st_spec,
                  _const_spec((RET_HEADS, c, c)), _const_spec((c, RET_WIDTH)),
                  _const_spec((c, RET_WIDTH)), _const_spec((RET_WIDTH, RET_WIDTH)),
                  _const_spec((1, RET_WIDTH)), _const_spec((1, RET_WIDTH))],
        out_specs=blk(0),
        out_shape=jax.ShapeDtypeStruct((bsz * seq, RET_WIDTH), BF16),
        compiler_params=_params(("parallel", "parallel")),
        name="retention_out",
    )(ret, ret, ret, g, sf, sb, decay, dq_f, dq_b, avg, gng, gnb)


def _cmul(ar, ai, br, bi):
    return ar * br - ai * bi, ar * bi + ai * br


def s5_matrices(A_re, A_im, log_dt, B_re, B_im, C_re, C_im, D):
    T, G, P, Cn = S5_CHUNK, S5_GROUPS, S5_STATE, S5_GROUP
    gh = S5_HALF // Cn
    depth = A_re.shape[0]
    step = jnp.exp(log_dt.astype(F32))[..., None]
    a_re = A_re.astype(F32)
    a_im = A_im.astype(F32)
    d = jnp.arange(T + 1, dtype=F32).reshape(T + 1, 1, 1, 1, 1)
    mag = jnp.exp(d * (step * a_re))
    pw_re = mag * jnp.cos(d * (step * a_im))
    pw_im = mag * jnp.sin(d * (step * a_im))
    den = a_re * a_re + a_im * a_im
    nr = pw_re[1] - 1.0
    ni = pw_im[1]
    coef_re = ((nr * a_re + ni * a_im) / den)[..., None]
    coef_im = ((ni * a_re - nr * a_im) / den)[..., None]
    b_re = B_re.astype(F32)
    b_im = B_im.astype(F32)
    bb_re = coef_re * b_re - coef_im * b_im
    bb_im = coef_re * b_im + coef_im * b_re
    c_re = C_re.astype(F32)
    c_im = C_im.astype(F32)
    t_idx = jnp.arange(T)
    fwd_in, bwd_in = T - 1 - t_idx, t_idx
    fwd_out, bwd_out = t_idx + 1, T - t_idx

    def state_in(direction, order):
        return _cmul(pw_re[order, :, direction, :, None, :], pw_im[order, :, direction, :, None, :],
                     jnp.swapaxes(bb_re[:, direction], -1, -2)[None],
                     jnp.swapaxes(bb_im[:, direction], -1, -2)[None])

    def read_out(direction, order):
        return _cmul(c_re[None, :, direction], c_im[None, :, direction],
                     pw_re[order, :, direction, :, None, :], pw_im[order, :, direction, :, None, :])

    def lag_kernel(direction):
        wr, wi = _cmul(pw_re[:T, :, direction, :, :, None], pw_im[:T, :, direction, :, :, None],
                       bb_re[None, :, direction], bb_im[None, :, direction])
        return (jnp.einsum('lgop,dlgpi->dlgoi', c_re[:, direction], wr)
                - jnp.einsum('lgop,dlgpi->dlgoi', c_im[:, direction], wi))

    def halves(v, lead):
        return v.reshape(v.shape[:lead] + (2, gh) + v.shape[lead + 1:])

    vfr, vfi = state_in(0, fwd_in)
    vbr, vbi = state_in(1, bwd_in)
    def spread(compact, n_inner, row_group):
        n_in = compact.shape[-1]
        n_out = n_in * gh
        src = jnp.arange(n_in)
        dst = jnp.arange(n_out)
        same = ((src[:, None] // n_inner == dst[None, :] // (gh * n_inner))
                & (src[:, None] % n_inner == dst[None, :] % n_inner))
        keep = row_group[:, None] == (dst[None, :] // n_inner) % gh
        wide = jnp.einsum('ldrk,kc->ldrc', compact, same.astype(BF16))
        return jnp.where(keep, wide, jnp.zeros((), BF16))

    in_rows = (jnp.arange(T * gh * Cn) // Cn) % gh
    state_rows = (jnp.arange(4 * gh * P) // P) % gh
    v_all = halves(jnp.stack([vfr, vfi, vbr, vbi]).astype(BF16), 3)
    v_compact = jnp.transpose(v_all, (2, 3, 1, 4, 5, 0, 6)).reshape(depth, 2, T * gh * Cn, 4 * P)
    mb = spread(v_compact, P, in_rows)

    efr, efi = read_out(0, fwd_out)
    ebr, ebi = read_out(1, bwd_out)
    e_all = halves(jnp.stack([efr, -efi, ebr, -ebi]).astype(BF16), 3)
    e_compact = jnp.transpose(e_all, (2, 3, 0, 4, 6, 1, 5)).reshape(depth, 2, 4 * gh * P, T * Cn)
    mc = spread(e_compact, Cn, state_rows)

    lag = t_idx[None, :] - t_idx[:, None]
    sel = lambda cond: cond[:, :, None, None, None, None]
    d_diag = D.astype(F32).reshape(depth, G, Cn)[..., None] * jnp.eye(Cn, dtype=F32)
    toe = (jnp.where(sel(lag >= 0), lag_kernel(0)[jnp.clip(lag, 0, T - 1)], 0.0)
           + jnp.where(sel(lag <= 0), lag_kernel(1)[jnp.clip(-lag, 0, T - 1)], 0.0)
           + jnp.where(sel(lag == 0), d_diag[None, None], 0.0))
    t_compact = jnp.transpose(halves(toe.astype(BF16), 3), (2, 3, 0, 4, 6, 1, 5)).reshape(
        depth, 2, T * gh * Cn, T * Cn)
    tp = spread(t_compact, Cn, in_rows)

    a_rows = jnp.stack([pw_re[T, :, 0], pw_im[T, :, 0], pw_re[T, :, 1], pw_im[T, :, 1]], axis=1)
    a8 = jnp.swapaxes(a_rows.reshape(depth, 4, 2, gh * P), 1, 2)
    a8 = jnp.concatenate([a8, a8], axis=2)
    return mb, tp, mc, a8


def _s5_kernel(u_ref, mb_ref, tp_ref, mc_ref, a8_ref, y_ref, u8_ref, w_ref, *, sub):
    rows = w_ref.shape[0]
    ns = a8_ref.shape[1]
    for r in range(0, rows, sub):
        steps = [u_ref[pl.ds(r * S5_CHUNK + s, sub, stride=S5_CHUNK), :].astype(BF16)
                 for s in range(S5_CHUNK)]
        u8 = jnp.concatenate(steps, axis=-1)
        u8_ref[r:r + sub, :] = u8
        w_ref[r:r + sub, :] = _dot(u8, mb_ref[...])

    def axpy(a, x, w):
        return a[0] * x[0] - a[1] * x[1] + w[0], a[0] * x[1] + a[1] * x[0] + w[1]

    zero = (jnp.zeros((1, ns), F32),) * 2
    a_f = (a8_ref[0:1, :], a8_ref[1:2, :])
    a_b = (a8_ref[2:3, :], a8_ref[3:4, :])
    a2_f = axpy(a_f, a_f, zero)
    a2_b = axpy(a_b, a_b, zero)

    def sweep(x, a, a2, r0, r1, col):
        re, im = slice(col, col + ns), slice(col + ns, col + 2 * ns)
        w0 = (w_ref[pl.ds(r0, 1), re], w_ref[pl.ds(r0, 1), im])
        w1 = (w_ref[pl.ds(r1, 1), re], w_ref[pl.ds(r1, 1), im])
        x1 = axpy(a, x, w0)
        w_ref[pl.ds(r0, 1), re] = x[0]
        w_ref[pl.ds(r0, 1), im] = x[1]
        w_ref[pl.ds(r1, 1), re] = x1[0]
        w_ref[pl.ds(r1, 1), im] = x1[1]
        return axpy(a2, x, axpy(a, w0, w1))

    def scan_step(i, carry):
        xf, xb = carry
        j = 2 * i
        jb = rows - 1 - j
        return sweep(xf, a_f, a2_f, j, j + 1, 0), sweep(xb, a_b, a2_b, jb, jb - 1, 2 * ns)

    lax.fori_loop(0, rows // 2, scan_step, (zero, zero))

    for r in range(0, rows, sub):
        y8 = (_dot(u8_ref[r:r + sub, :], tp_ref[...])
              + _dot(w_ref[r:r + sub, :].astype(BF16), mc_ref[...]))
        for t in range(S5_CHUNK):
            y_ref[pl.ds(r * S5_CHUNK + t, sub, stride=S5_CHUNK), :] = y8[:, t * S5_HALF:(t + 1) * S5_HALF]


def s5_mixer(u, mats, layer, bsz, seq):
    mb, tp, mc, a8 = mats
    n = bsz * seq
    rows = seq // S5_CHUNK
    width = S5_CHUNK * S5_HALF
    ns = (S5_HALF // S5_GROUP) * S5_STATE
    kern = functools.partial(_s5_kernel, sub=min(256, rows))
    wspec = lambda a: pl.BlockSpec((None, None) + a.shape[2:], lambda h, b: (layer, h, 0, 0),
                                   pipeline_mode=pl.Buffered(1))
    tokens = pl.BlockSpec((None, seq, S5_HALF), lambda h, b: (h, b, 0))
    return pl.pallas_call(
        kern,
        grid=(2, bsz),
        in_specs=[tokens, wspec(mb), wspec(tp), wspec(mc), wspec(a8)],
        out_specs=tokens,
        out_shape=jax.ShapeDtypeStruct((2, n, S5_HALF), F32),
        scratch_shapes=[pltpu.VMEM((rows, width), BF16), pltpu.VMEM((rows, 4 * ns), F32)],
        compiler_params=_params(("arbitrary", "arbitrary")),
        name="s5_mixer",
    )(u, mb, tp, mc, a8)


def _out_proj_kernel(x_ref, da_ref, ret_ref, y5_ref, gluw_ref, glub_ref, wout_ref,
                     g_ref, b_ref, o_ref, *, alpha):
    c1 = DA_WIDTH
    c2 = DA_WIDTH + RET_WIDTH
    for r in range(0, x_ref.shape[0], EPILOGUE_ROWS):
        rs = slice(r, r + EPILOGUE_ROWS)
        y = jnp.concatenate([y5_ref[0, rs, :], y5_ref[1, rs, :]], axis=-1)
        ya = _gelu_tanh(y)
        gate = _sigmoid(_dot(ya.astype(BF16), gluw_ref[...]) + glub_ref[...])
        ys5 = (ya * gate).astype(BF16)
        mix = (_dot(da_ref[rs, :], wout_ref[0:c1, :]) + _dot(ret_ref[rs, :], wout_ref[c1:c2, :])
               + _dot(ys5, wout_ref[c2:, :]))
        o_ref[rs, :] = _layer_norm(alpha * x_ref[rs, :] + mix, g_ref[...], b_ref[...])


def out_proj(x, y_da, y_ret, y5, layer, glu_w, glu_b, w_out, ln_g, ln_b, alpha, tm=ROW_TILE):
    n = x.shape[0]
    tm = min(tm, n)
    row = lambda i: (i, 0)
    return pl.pallas_call(
        functools.partial(_out_proj_kernel, alpha=alpha),
        grid=(n // tm,),
        in_specs=[pl.BlockSpec((tm, D_MODEL), row), pl.BlockSpec((tm, DA_WIDTH), row),
                  pl.BlockSpec((tm, RET_WIDTH), row),
                  pl.BlockSpec((2, tm, S5_HALF), lambda i: (0, i, 0)),
                  _layer_spec((S5_WIDTH, S5_WIDTH), layer), _const_spec((1, S5_WIDTH)),
                  _layer_spec((D_MODEL, D_MODEL), layer),
                  _const_spec((1, D_MODEL)), _const_spec((1, D_MODEL))],
        out_specs=pl.BlockSpec((tm, D_MODEL), row),
        out_shape=jax.ShapeDtypeStruct((n, D_MODEL), F32),
        compiler_params=_params(("parallel",)),
        name="out_proj",
    )(x, y_da, y_ret, y5, glu_w, glu_b, w_out, ln_g, ln_b)


def _ffn_kernel(x_ref, xp_ref, xn_ref, p_ref, wup_ref, cw_ref, cb_ref, wdn_ref,
                plew_ref, gatew_ref, g_ref, b_ref, o_ref, act_ref, *, alpha, tiles_per_seq):
    tm = x_ref.shape[0]
    i = pl.program_id(0)
    has_prev = ((i % tiles_per_seq) != 0).astype(F32)
    has_next = ((i % tiles_per_seq) != tiles_per_seq - 1).astype(F32)
    x = x_ref[...]
    xb = x.astype(BF16)
    xpb = xp_ref[...].astype(BF16)
    xnb = xn_ref[...].astype(BF16)
    halo = xp_ref.shape[0]
    row = lax.broadcasted_iota(jnp.int32, (halo, FF_CHUNK), 0)

    for c in range(0, D_FF, FF_CHUNK):
        wg = wup_ref[:, c:c + FF_CHUNK]
        gate = _dot(xb, wg)
        val = _dot(xb, wup_ref[:, D_FF + c:D_FF + c + FF_CHUNK])
        before = _dot(xpb, wg)[halo - 1:halo, :] * has_prev
        after = _dot(xnb, wg)[0:1, :] * has_next
        left = pltpu.roll(gate, 1, 0)
        left = jnp.concatenate([jnp.where(row == 0, before, left[:halo]), left[halo:]], axis=0)
        right = pltpu.roll(gate, tm - 1, 0)
        right = jnp.concatenate(
            [right[:tm - halo], jnp.where(row == halo - 1, after, right[tm - halo:])], axis=0)
        conv = (cw_ref[0:1, c:c + FF_CHUNK] * left + cw_ref[1:2, c:c + FF_CHUNK] * gate
                + cw_ref[2:3, c:c + FF_CHUNK] * right + cb_ref[:, c:c + FF_CHUNK])
        act_ref[:, c:c + FF_CHUNK] = (_gelu_tanh(conv) * val).astype(BF16)

    for r in range(0, tm, EPILOGUE_ROWS):
        rs = slice(r, r + EPILOGUE_ROWS)
        ple = (_dot(p_ref[rs, :].astype(BF16), plew_ref[...])
               * _sigmoid(_dot(xb[rs], gatew_ref[...])))
        f = _dot(act_ref[rs, :], wdn_ref[...])
        o_ref[rs, :] = _layer_norm(alpha * x[rs] + f + ple, g_ref[...], b_ref[...])


def conv_ffn_ple(x, p, layer, w_up, conv_w, conv_b, w_down, ple_w, gate_w, ln_g, ln_b, alpha, seq, tm=ROW_TILE):
    n = x.shape[0]
    tm = min(tm, seq)
    p_base = layer * (n // tm)
    halo = 8
    tiles_per_seq = seq // tm
    per = tm // halo
    last = n // halo - 1
    row = lambda i: (i, 0)
    kern = functools.partial(_ffn_kernel, alpha=alpha, tiles_per_seq=tiles_per_seq)
    return pl.pallas_call(
        kern,
        grid=(n // tm,),
        in_specs=[pl.BlockSpec((tm, D_MODEL), row),
                  pl.BlockSpec((halo, D_MODEL), lambda i: (jnp.maximum(i * per - 1, 0), 0)),
                  pl.BlockSpec((halo, D_MODEL), lambda i: (jnp.minimum((i + 1) * per, last), 0)),
                  pl.BlockSpec((tm, PLE_DIM), lambda i: (p_base + i, 0)),
                  _layer_spec((D_MODEL, 2 * D_FF), layer), _const_spec((3, D_FF)),
                  _const_spec((1, D_FF)),
                  _layer_spec((D_FF, D_MODEL), layer), _layer_spec((PLE_DIM, D_MODEL), layer),
                  _layer_spec((D_MODEL, D_MODEL), layer),
                  _const_spec((1, D_MODEL)), _const_spec((1, D_MODEL))],
        out_specs=pl.BlockSpec((tm, D_MODEL), row),
        out_shape=jax.ShapeDtypeStruct((n, D_MODEL), F32),
        scratch_shapes=[pltpu.VMEM((tm, D_FF), BF16)],
        compiler_params=_params(("parallel",)),
        name="conv_ffn_ple",
    )(x, x, x, p, w_up, conv_w, conv_b, w_down, ple_w, gate_w, ln_g, ln_b)


def kernel(x, p, positions, w_in, da_lambda_q1, da_lambda_k1, da_lambda_q2, da_lambda_k2,
           da_subln_g, ret_gn_g, ret_gn_b, s5_A_re, s5_A_im, s5_log_dt, s5_B_re, s5_B_im,
           s5_C_re, s5_C_im, s5_D, s5_glu_w, s5_glu_b, w_out, ln1_g, ln1_b,
           ffn_w_up, ffn_conv_w, ffn_conv_b, ffn_w_down, ple_w, ple_gate_w, ln2_g, ln2_b):
    bsz, seq, _ = x.shape
    depth = w_in.shape[0]
    n = bsz * seq
    alpha = (2 * depth) ** 0.25
    cos, sin = rope_tables(positions)
    xf = x.reshape(n, D_MODEL)
    p_all = p.reshape(depth * n, PLE_DIM)
    mats = s5_matrices(s5_A_re, s5_A_im, s5_log_dt, s5_B_re, s5_B_im, s5_C_re, s5_C_im, s5_D)
    row = lambda v: v.reshape(1, -1).astype(F32)
    w_in_b, glu_w_b, w_out_b = w_in.astype(BF16), s5_glu_w.astype(BF16), w_out.astype(BF16)
    w_up_b, w_down_b = ffn_w_up.astype(BF16), ffn_w_down.astype(BF16)
    ple_w_b, gate_w_b = ple_w.astype(BF16), ple_gate_w.astype(BF16)
    for i in range(depth):
        lambda_init = 0.8 - 0.6 * math.exp(-0.3 * i)
        lam = (jnp.exp(jnp.sum(da_lambda_q1[i].astype(F32) * da_lambda_k1[i].astype(F32)))
               - jnp.exp(jnp.sum(da_lambda_q2[i].astype(F32) * da_lambda_k2[i].astype(F32)))
               + lambda_init)
        da, ret, g, u = in_proj(xf, w_in_b, i, cos, sin)
        y_da = diff_attention(da, lam.reshape(1, 1), da_subln_g[i].astype(F32).reshape(-1, 1),
                              1.0 - lambda_init, bsz, seq)
        y_ret = retention(ret, g, ret_gn_g[i].astype(F32), ret_gn_b[i].astype(F32), bsz, seq)
        y5 = s5_mixer(u, mats, i, bsz, seq)
        x1 = out_proj(xf, y_da, y_ret, y5, i, glu_w_b, row(s5_glu_b[i]), w_out_b,
                      row(ln1_g[i]), row(ln1_b[i]), alpha)
        xf = conv_ffn_ple(x1, p_all, i, w_up_b, ffn_conv_w[i].astype(F32), row(ffn_conv_b[i]),
                          w_down_b, ple_w_b, gate_w_b, row(ln2_g[i]), row(ln2_b[i]), alpha, seq)
    return xf.reshape(bsz, seq, D_MODEL)
```

```python
import functools
import math

import jax
import jax.numpy as jnp
from jax import lax
from jax.experimental import pallas as pl
from jax.experimental.pallas import tpu as pltpu

F32 = jnp.float32
BF16 = jnp.bfloat16

D_MODEL = 1024
PLE_DIM = 256
DA_HEADS = 4
DA_QK_DIM = 64
DA_V_DIM = 128
DA_WIDTH = DA_HEADS * DA_V_DIM
RET_HEADS = 4
RET_HEAD_DIM = 64
RET_WIDTH = RET_HEADS * RET_HEAD_DIM
S5_WIDTH = 256
S5_GROUP = 16
S5_GROUPS = S5_WIDTH // S5_GROUP
S5_STATE = 64
D_FF = 2816
ROPE_THETA = 10000.0
LN_EPS = 1e-5
RMS_EPS = 1e-6

COL_DA_Q = 0
COL_DA_K = COL_DA_Q + DA_HEADS * 2 * DA_QK_DIM
COL_DA_V = COL_DA_K + DA_HEADS * 2 * DA_QK_DIM
COL_RET_Q = COL_DA_V + DA_WIDTH
COL_RET_K = COL_RET_Q + RET_WIDTH
COL_RET_V = COL_RET_K + RET_WIDTH
COL_RET_G = COL_RET_V + RET_WIDTH
COL_S5_U = COL_RET_G + RET_WIDTH
IN_COLS = COL_S5_U + S5_WIDTH

LANES = 128
MXU_WIDTH = 256
S5_CHUNK = 8
S5_HALF = 128
RET_CHUNK = 256
RET_CHUNKS_PER_STEP = 4
FF_CHUNK = 256
ROW_TILE = 1024
EPILOGUE_ROWS = 256
LOG2E = 1.4426950408889634
NEG_BIG = -1e30
VMEM_LIMIT = 56 * 1024 * 1024


def _params(sem, vmem=VMEM_LIMIT):
    return pltpu.CompilerParams(dimension_semantics=sem, vmem_limit_bytes=vmem)


def _const_spec(shape):
    nd = len(shape)
    return pl.BlockSpec(shape, lambda *_: (0,) * nd, pipeline_mode=pl.Buffered(1))


def _layer_spec(shape, layer):
    nd = len(shape)
    return pl.BlockSpec((None,) + tuple(shape), lambda *_: (layer,) + (0,) * nd,
                        pipeline_mode=pl.Buffered(1))


def _layer_norm(x, g, b):
    mu = jnp.mean(x, axis=-1, keepdims=True)
    d = x - mu
    var = jnp.mean(d * d, axis=-1, keepdims=True)
    return d * lax.rsqrt(var + LN_EPS) * g + b


def _gelu_tanh(x):
    return 0.5 * x * (1.0 + jnp.tanh(math.sqrt(2.0 / math.pi) * (x + 0.044715 * (x * x * x))))


def _sigmoid(x):
    return 1.0 / (1.0 + jnp.exp(-x))


def _dot(a, b):
    return jnp.dot(a, b, preferred_element_type=F32)


def _dot_nt(a, b):
    return lax.dot_general(a, b, (((1,), (1,)), ((), ())), preferred_element_type=F32)


def _dot_tn(a, b):
    return lax.dot_general(a, b, (((0,), (0,)), ((), ())), preferred_element_type=F32)


def _dot_split(x, w):
    hi = x.astype(BF16)
    lo = (x - hi.astype(F32)).astype(BF16)
    return _dot(hi, w) + _dot(lo, w)


def _rope_table_kernel(pos_ref, freq_ref, sign_ref, cos_ref, sin_ref):
    ang = pos_ref[...].astype(F32) * freq_ref[...]
    cos_ref[...] = jnp.cos(ang)
    sin_ref[...] = jnp.sin(ang) * sign_ref[...]


def rope_tables(positions, tm=1024):
    n = positions.size
    half = DA_QK_DIM // 2
    inv_freq = ROPE_THETA ** (-jnp.arange(0, DA_QK_DIM, 2, dtype=F32) / DA_QK_DIM)
    freq_row = jnp.tile(inv_freq, LANES // half).reshape(1, LANES)
    lane = jnp.arange(LANES)
    sign_row = jnp.where(lane % DA_QK_DIM < half, -1.0, 1.0).astype(F32).reshape(1, LANES)
    pos = positions.reshape(n, 1)
    tm = min(tm, n)
    return pl.pallas_call(
        _rope_table_kernel,
        grid=(n // tm,),
        in_specs=[pl.BlockSpec((tm, 1), lambda i: (i, 0)),
                  _const_spec((1, LANES)), _const_spec((1, LANES))],
        out_specs=[pl.BlockSpec((tm, LANES), lambda i: (i, 0)),
                   pl.BlockSpec((tm, LANES), lambda i: (i, 0))],
        out_shape=[jax.ShapeDtypeStruct((n, LANES), F32)] * 2,
        compiler_params=_params(("parallel",)),
        name="rope_tables",
    )(pos, freq_row, sign_row)


def _rope(x, cos, sin, first_half):
    swapped = jnp.where(first_half, pltpu.roll(x, LANES - DA_QK_DIM // 2, 1),
                        pltpu.roll(x, DA_QK_DIM // 2, 1))
    return x * cos + swapped * sin


def _in_proj_kernel(x_ref, w_ref, cos_ref, sin_ref, da_ref, ret_ref, g_ref, u_ref):
    xb = x_ref[...].astype(BF16)
    cos = cos_ref[...]
    sin = sin_ref[...]
    lane = lax.broadcasted_iota(jnp.int32, cos.shape, 1)
    first_half = (lane % DA_QK_DIM) < (DA_QK_DIM // 2)
    q_scale = DA_QK_DIM ** -0.5 * LOG2E
    k_scale = RET_HEAD_DIM ** -0.5

    def proj(col):
        return _dot(xb, w_ref[:, col:col + MXU_WIDTH])

    def roped(z, scale):
        parts = [_rope(z[:, a:a + LANES], cos, sin, first_half) for a in (0, LANES)]
        out = jnp.concatenate(parts, axis=-1)
        return out if scale is None else out * scale

    for c in range(0, COL_DA_K, MXU_WIDTH):
        da_ref[:, c:c + MXU_WIDTH] = roped(proj(c), q_scale).astype(BF16)
    for c in range(COL_DA_K, COL_DA_V, MXU_WIDTH):
        da_ref[:, c:c + MXU_WIDTH] = roped(proj(c), None).astype(BF16)
    for c in range(COL_DA_V, COL_RET_Q, MXU_WIDTH):
        da_ref[:, c:c + MXU_WIDTH] = proj(c).astype(BF16)
    ret_ref[:, 0:RET_WIDTH] = roped(proj(COL_RET_Q), None).astype(BF16)
    ret_ref[:, RET_WIDTH:2 * RET_WIDTH] = roped(proj(COL_RET_K), k_scale).astype(BF16)
    ret_ref[:, 2 * RET_WIDTH:3 * RET_WIDTH] = proj(COL_RET_V).astype(BF16)
    g_ref[...] = proj(COL_RET_G)
    u = proj(COL_S5_U)
    u_ref[0] = u[:, :S5_HALF]
    u_ref[1] = u[:, S5_HALF:]


def in_proj(x, w_bf16, layer, cos, sin, tm=ROW_TILE):
    n = x.shape[0]
    tm = min(tm, n)
    row = lambda i: (i, 0)
    return pl.pallas_call(
        _in_proj_kernel,
        grid=(n // tm,),
        in_specs=[pl.BlockSpec((tm, D_MODEL), row), _layer_spec((D_MODEL, IN_COLS), layer),
                  pl.BlockSpec((tm, LANES), row), pl.BlockSpec((tm, LANES), row)],
        out_specs=[pl.BlockSpec((tm, COL_RET_Q), row), pl.BlockSpec((tm, 3 * RET_WIDTH), row),
                   pl.BlockSpec((tm, RET_WIDTH), row),
                   pl.BlockSpec((2, tm, S5_HALF), lambda i: (0, i, 0))],
        out_shape=[jax.ShapeDtypeStruct((n, COL_RET_Q), BF16),
                   jax.ShapeDtypeStruct((n, 3 * RET_WIDTH), BF16),
                   jax.ShapeDtypeStruct((n, RET_WIDTH), F32),
                   jax.ShapeDtypeStruct((2, n, S5_HALF), F32)],
        compiler_params=_params(("parallel",)),
        name="in_proj",
    )(x, w_bf16, cos, sin)


ONES_ROWS = 16


def _diff_attn_kernel(q_ref, k_ref, v_ref, lam_ref, g_ref, o_ref,
                      km_ref, vt_ref, m_ref, acc_ref, s_ref, *, tq, tk, out_scale):
    seq = k_ref.shape[0]
    n_kv = seq // tk
    n_q = seq // tq
    k_all = k_ref[...]
    lane = lax.broadcasted_iota(jnp.int32, k_all.shape, 1)
    zero = jnp.zeros_like(k_all)
    km_ref[0] = jnp.where(lane < DA_QK_DIM, k_all, zero)
    km_ref[1] = jnp.where(lane >= DA_QK_DIM, k_all, zero)
    for i in range(n_kv):
        vt_ref[i, :DA_V_DIM, :] = v_ref[i * tk:(i + 1) * tk, :].astype(F32).T.astype(BF16)
        vt_ref[i, DA_V_DIM:, :] = jnp.ones((ONES_ROWS, tk), BF16)

    def scores(qi, ki, slot):
        q = q_ref[pl.ds(pl.multiple_of(qi * tq, tq), tq), :]
        rows = pl.ds(pl.multiple_of(ki * tk, tk), tk)
        for mi in range(2):
            s_ref[slot, mi] = _dot_nt(km_ref[mi, rows, :], q)

    def consume(ki, slot):
        vt = vt_ref[ki]
        for mi in range(2):
            s = s_ref[slot, mi]
            m_old = m_ref[mi]
            m_new = jnp.maximum(m_old, jnp.max(s, axis=0, keepdims=True))
            alpha = jnp.exp2(m_old - m_new)
            p = jnp.exp2(s - m_new).astype(BF16)
            acc_ref[mi] = acc_ref[mi] * alpha + _dot(vt, p)
            m_ref[mi] = m_new

    scores(0, 0, 0)

    def q_tile(qi, carry):
        m_ref[...] = jnp.full(m_ref.shape, NEG_BIG, F32)
        acc_ref[...] = jnp.zeros(acc_ref.shape, F32)

        def kv_pair(j, c):
            for u in range(2):
                ki = 2 * j + u
                wrap = ki + 1 >= n_kv
                scores(jnp.where(wrap, jnp.minimum(qi + 1, n_q - 1), qi),
                       jnp.where(wrap, 0, ki + 1), 1 - u)
                consume(ki, u)
            return c

        lax.fori_loop(0, n_kv // 2, kv_pair, 0)
        a0 = acc_ref[0]
        a1 = acc_ref[1]
        o = (a0[:DA_V_DIM] / a0[DA_V_DIM:DA_V_DIM + 1]
             - lam_ref[...] * (a1[:DA_V_DIM] / a1[DA_V_DIM:DA_V_DIM + 1]))
        ms = jnp.mean(o * o, axis=0, keepdims=True)
        o = o * lax.rsqrt(ms + RMS_EPS) * g_ref[...] * out_scale
        rows = pl.ds(pl.multiple_of(qi * tq, tq), tq)
        o_ref[rows, :] = o.T.astype(o_ref.dtype)
        return carry

    lax.fori_loop(0, n_q, q_tile, 0)


def diff_attention(da, lam, subln_col, out_scale, bsz, seq, tq=256, tk=4096):
    tq = min(tq, seq)
    tk = min(tk, seq // 2)
    assert seq % (2 * tk) == 0 and seq % tq == 0
    kern = functools.partial(_diff_attn_kernel, tq=tq, tk=tk, out_scale=out_scale)
    head_block = lambda part: pl.BlockSpec((seq, LANES), lambda b, h: (b, part * DA_HEADS + h))
    vrows = DA_V_DIM + ONES_ROWS
    return pl.pallas_call(
        kern,
        grid=(bsz, DA_HEADS),
        in_specs=[head_block(0), head_block(1), head_block(2),
                  _const_spec((1, 1)), _const_spec((DA_V_DIM, 1))],
        out_specs=pl.BlockSpec((seq, DA_V_DIM), lambda b, h: (b, h)),
        out_shape=jax.ShapeDtypeStruct((bsz * seq, DA_WIDTH), BF16),
        scratch_shapes=[pltpu.VMEM((2, seq, LANES), BF16),
                        pltpu.VMEM((seq // tk, vrows, tk), BF16),
                        pltpu.VMEM((2, 1, tq), F32),
                        pltpu.VMEM((2, vrows, tq), F32),
                        pltpu.VMEM((2, 2, tk, tq), F32)],
        compiler_params=_params(("parallel", "parallel")),
        name="diff_attention",
    )(da, da, da, lam, subln_col)


def _ret_state_kernel(k_ref, v_ref, dkf_ref, dkb_ref, gc_ref, sf_out, sb_out, sf_ref, sb_ref):
    kf_ref = kb_ref = k_ref
    vf_ref = vb_ref = v_ref
    sf_ref[...] = jnp.zeros(sf_ref.shape, F32)
    sb_ref[...] = jnp.zeros(sb_ref.shape, F32)
    c = dkf_ref.shape[0]
    per_step = sf_out.shape[0]

    def update(s_ref, k, v, dk_ref):
        kd = (k.astype(F32) * dk_ref[...]).astype(BF16)
        for pr in range(RET_WIDTH // LANES):
            sl = slice(pr * LANES, (pr + 1) * LANES)
            kv = _dot_tn(kd[:, sl], v[:, sl])
            keep = gc_ref[sl, :]
            s_ref[sl, :] = keep * s_ref[sl, :] + jnp.where(keep > 0.0, kv, 0.0)

    for u in range(per_step):
        lo = slice(u * c, (u + 1) * c)
        hi = slice((per_step - 1 - u) * c, (per_step - u) * c)
        sf_out[u] = sf_ref[...]
        sb_out[per_step - 1 - u] = sb_ref[...]
        update(sf_ref, kf_ref[lo, :], vf_ref[lo, :], dkf_ref)
        update(sb_ref, kb_ref[hi, :], vb_ref[hi, :], dkb_ref)


def _ret_out_kernel(q_ref, k_ref, v_ref, g_ref, sf_ref, sb_ref, dec_ref, dqf_ref, dqb_ref,
                    avg_ref, gng_ref, gnb_ref, o_ref):
    c = dqf_ref.shape[0]
    chunks = range(sf_ref.shape[0])
    pairs = range(RET_WIDTH // LANES)
    halves = range(LANES // RET_HEAD_DIM)
    lane = lax.broadcasted_iota(jnp.int32, (c, LANES), 1)
    zero = jnp.zeros((c, LANES), BF16)
    mine = [(lane >= hh * RET_HEAD_DIM) & (lane < (hh + 1) * RET_HEAD_DIM) for hh in halves]
    avg = avg_ref[...]
    rows = [slice(u * c, (u + 1) * c) for u in chunks]
    cols = [slice(pr * LANES, (pr + 1) * LANES) for pr in pairs]
    scores, carried = {}, {}
    for u in chunks:
        q = q_ref[rows[u], :]
        k = k_ref[rows[u], :]
        qf = q.astype(F32)
        for pr in pairs:
            sl = cols[pr]
            carried[u, pr] = (
                _dot((qf[:, sl] * dqf_ref[:, sl]).astype(BF16), sf_ref[u, sl, :].astype(BF16))
                + _dot((qf[:, sl] * dqb_ref[:, sl]).astype(BF16), sb_ref[u, sl, :].astype(BF16)))
            for hh in halves:
                scores[u, pr, hh] = _dot_nt(jnp.where(mine[hh], q[:, sl], zero), k[:, sl])
    outs = {}
    for u in chunks:
        v = v_ref[rows[u], :]
        parts = []
        for pr in pairs:
            acc = carried[u, pr]
            for hh in halves:
                s = (scores[u, pr, hh] * dec_ref[pr * 2 + hh]).astype(BF16)
                acc += _dot(s, jnp.where(mine[hh], v[:, cols[pr]], zero))
            parts.append(acc)
        outs[u] = jnp.concatenate(parts, axis=-1)
    mean = {u: _dot_split(outs[u], avg) for u in chunks}
    dev = {u: outs[u] - mean[u] for u in chunks}
    var = {u: _dot_split(dev[u] * dev[u], avg) for u in chunks}
    for u in chunks:
        y = dev[u] * lax.rsqrt(var[u] + LN_EPS) * gng_ref[...] + gnb_ref[...]
        g = g_ref[rows[u], :]
        o_ref[rows[u], :] = (g * _sigmoid(g) * y).astype(o_ref.dtype)


def retention(ret, g, gn_g, gn_b, bsz, seq):
    c = min(RET_CHUNK, seq)
    nc = seq // c
    heads = jnp.arange(RET_HEADS, dtype=F32)
    log_gamma = jnp.log(1.0 - 2.0 ** (-5.0 - heads))
    lg_cols = jnp.repeat(log_gamma, RET_HEAD_DIM)[None, :]
    idx = jnp.arange(c, dtype=F32)[:, None]
    dk_f = jnp.exp(lg_cols * (c - 1 - idx))
    dk_b = jnp.exp(lg_cols * idx)
    dq_f = jnp.exp(lg_cols * (idx + 1))
    dq_b = jnp.exp(lg_cols * (c - idx))
    dist = jnp.abs(idx - idx.T)
    decay = jnp.exp(log_gamma[:, None, None] * dist[None])
    row_head = jnp.arange(RET_WIDTH)[:, None] // RET_HEAD_DIM
    col_head = (jnp.arange(LANES)[None, :] // RET_HEAD_DIM) + 2 * (jnp.arange(RET_WIDTH)[:, None] // LANES)
    gc = jnp.where(row_head == col_head, jnp.exp(lg_cols.T * c), 0.0).astype(F32)
    seg = jnp.arange(RET_WIDTH) // RET_HEAD_DIM
    avg = jnp.where(seg[:, None] == seg[None, :], 1.0 / RET_HEAD_DIM, 0.0).astype(BF16)
    gng = jnp.tile(gn_g, RET_HEADS)[None, :]
    gnb = jnp.tile(gn_b, RET_HEADS)[None, :]

    per_step = RET_CHUNKS_PER_STEP if nc % RET_CHUNKS_PER_STEP == 0 else 1
    ns = nc // per_step
    rows = per_step * c
    blk = lambda col: pl.BlockSpec((rows, RET_WIDTH), lambda b, j: (b * ns + j, col))
    blk_rev = lambda col: pl.BlockSpec((rows, RET_WIDTH), lambda b, j: (b * ns + ns - 1 - j, col))
    st_shape = jax.ShapeDtypeStruct((bsz, nc, RET_WIDTH, LANES), F32)
    st_blk = (None, per_step, RET_WIDTH, LANES)
    seq_blk = lambda col: pl.BlockSpec((seq, RET_WIDTH), lambda b: (b, col))
    all_states = pl.BlockSpec((None, nc, RET_WIDTH, LANES), lambda b: (b, 0, 0, 0))
    sf, sb = pl.pallas_call(
        _ret_state_kernel,
        grid=(bsz,),
        in_specs=[seq_blk(1), seq_blk(2),
                  _const_spec((c, RET_WIDTH)), _const_spec((c, RET_WIDTH)),
                  _const_spec((RET_WIDTH, LANES))],
        out_specs=[all_states, all_states],
        out_shape=[st_shape, st_shape],
        scratch_shapes=[pltpu.VMEM((RET_WIDTH, LANES), F32), pltpu.VMEM((RET_WIDTH, LANES), F32)],
        compiler_params=_params(("parallel",)),
        name="retention_state",
    )(ret, ret, dk_f, dk_b, gc)

    st_spec = pl.BlockSpec(st_blk, lambda b, j: (b, j, 0, 0))
    return pl.pallas_call(
        _ret_out_kernel,
        grid=(bsz, ns),
        in_specs=[blk(0), blk(1), blk(2), blk(0), st_spec, st_spec,
                  _const_spec((RET_HEADS, c, c)), _const_spec((c, RET_WIDTH)),
                  _const_spec((c, RET_WIDTH)), _const_spec((RET_WIDTH, RET_WIDTH)),
                  _const_spec((1, RET_WIDTH)), _const_spec((1, RET_WIDTH))],
        out_specs=blk(0),
        out_shape=jax.ShapeDtypeStruct((bsz * seq, RET_WIDTH), BF16),
        compiler_params=_params(("parallel", "parallel")),
        name="retention_out",
    )(ret, ret, ret, g, sf, sb, decay, dq_f, dq_b, avg, gng, gnb)


def _cmul(ar, ai, br, bi):
    return ar * br - ai * bi, ar * bi + ai * br


def s5_matrices(A_re, A_im, log_dt, B_re, B_im, C_re, C_im, D):
    T, G, P, Cn = S5_CHUNK, S5_GROUPS, S5_STATE, S5_GROUP
    gh = S5_HALF // Cn
    depth = A_re.shape[0]
    step = jnp.exp(log_dt.astype(F32))[..., None]
    a_re = A_re.astype(F32)
    a_im = A_im.astype(F32)
    d = jnp.arange(T + 1, dtype=F32).reshape(T + 1, 1, 1, 1, 1)
    mag = jnp.exp(d * (step * a_re))
    pw_re = mag * jnp.cos(d * (step * a_im))
    pw_im = mag * jnp.sin(d * (step * a_im))
    den = a_re * a_re + a_im * a_im
    nr = pw_re[1] - 1.0
    ni = pw_im[1]
    coef_re = ((nr * a_re + ni * a_im) / den)[..., None]
    coef_im = ((ni * a_re - nr * a_im) / den)[..., None]
    b_re = B_re.astype(F32)
    b_im = B_im.astype(F32)
    bb_re = coef_re * b_re - coef_im * b_im
    bb_im = coef_re * b_im + coef_im * b_re
    c_re = C_re.astype(F32)
    c_im = C_im.astype(F32)
    t_idx = jnp.arange(T)
    fwd_in, bwd_in = T - 1 - t_idx, t_idx
    fwd_out, bwd_out = t_idx + 1, T - t_idx

    def state_in(direction, order):
        return _cmul(pw_re[order, :, direction, :, None, :], pw_im[order, :, direction, :, None, :],
                     jnp.swapaxes(bb_re[:, direction], -1, -2)[None],
                     jnp.swapaxes(bb_im[:, direction], -1, -2)[None])

    def read_out(direction, order):
        return _cmul(c_re[None, :, direction], c_im[None, :, direction],
                     pw_re[order, :, direction, :, None, :], pw_im[order, :, direction, :, None, :])

    def lag_kernel(direction):
        wr, wi = _cmul(pw_re[:T, :, direction, :, :, None], pw_im[:T, :, direction, :, :, None],
                       bb_re[None, :, direction], bb_im[None, :, direction])
        return (jnp.einsum('lgop,dlgpi->dlgoi', c_re[:, direction], wr)
                - jnp.einsum('lgop,dlgpi->dlgoi', c_im[:, direction], wi))

    def halves(v, lead):
        return v.reshape(v.shape[:lead] + (2, gh) + v.shape[lead + 1:])

    vfr, vfi = state_in(0, fwd_in)
    vbr, vbi = state_in(1, bwd_in)
    def spread(compact, n_inner, row_group):
        n_in = compact.shape[-1]
        n_out = n_in * gh
        src = jnp.arange(n_in)
        dst = jnp.arange(n_out)
        same = ((src[:, None] // n_inner == dst[None, :] // (gh * n_inner))
                & (src[:, None] % n_inner == dst[None, :] % n_inner))
        keep = row_group[:, None] == (dst[None, :] // n_inner) % gh
        wide = jnp.einsum('ldrk,kc->ldrc', compact, same.astype(BF16))
        return jnp.where(keep, wide, jnp.zeros((), BF16))

    in_rows = (jnp.arange(T * gh * Cn) // Cn) % gh
    state_rows = (jnp.arange(4 * gh * P) // P) % gh
    v_all = halves(jnp.stack([vfr, vfi, vbr, vbi]).astype(BF16), 3)
    v_compact = jnp.transpose(v_all, (2, 3, 1, 4, 5, 0, 6)).reshape(depth, 2, T * gh * Cn, 4 * P)
    mb = spread(v_compact, P, in_rows)

    efr, efi = read_out(0, fwd_out)
    ebr, ebi = read_out(1, bwd_out)
    e_all = halves(jnp.stack([efr, -efi, ebr, -ebi]).astype(BF16), 3)
    e_compact = jnp.transpose(e_all, (2, 3, 0, 4, 6, 1, 5)).reshape(depth, 2, 4 * gh * P, T * Cn)
    mc = spread(e_compact, Cn, state_rows)

    lag = t_idx[None, :] - t_idx[:, None]
    sel = lambda cond: cond[:, :, None, None, None, None]
    d_diag = D.astype(F32).reshape(depth, G, Cn)[..., None] * jnp.eye(Cn, dtype=F32)
    toe = (jnp.where(sel(lag >= 0), lag_kernel(0)[jnp.clip(lag, 0, T - 1)], 0.0)
           + jnp.where(sel(lag <= 0), lag_kernel(1)[jnp.clip(-lag, 0, T - 1)], 0.0)
           + jnp.where(sel(lag == 0), d_diag[None, None], 0.0))
    t_compact = jnp.transpose(halves(toe.astype(BF16), 3), (2, 3, 0, 4, 6, 1, 5)).reshape(
        depth, 2, T * gh * Cn, T * Cn)
    tp = spread(t_compact, Cn, in_rows)

    a_rows = jnp.stack([pw_re[T, :, 0], pw_im[T, :, 0], pw_re[T, :, 1], pw_im[T, :, 1]], axis=1)
    a8 = jnp.swapaxes(a_rows.reshape(depth, 4, 2, gh * P), 1, 2)
    a8 = jnp.concatenate([a8, a8], axis=2)
    return mb, tp, mc, a8


def _s5_kernel(u_ref, mb_ref, tp_ref, mc_ref, a8_ref, y_ref, u8_ref, w_ref, *, sub):
    rows = w_ref.shape[0]
    ns = a8_ref.shape[1]
    for r in range(0, rows, sub):
        steps = [u_ref[pl.ds(r * S5_CHUNK + s, sub, stride=S5_CHUNK), :].astype(BF16)
                 for s in range(S5_CHUNK)]
        u8 = jnp.concatenate(steps, axis=-1)
        u8_ref[r:r + sub, :] = u8
        w_ref[r:r + sub, :] = _dot(u8, mb_ref[...])

    def axpy(a, x, w):
        return a[0] * x[0] - a[1] * x[1] + w[0], a[0] * x[1] + a[1] * x[0] + w[1]

    zero = (jnp.zeros((1, ns), F32),) * 2
    a_f = (a8_ref[0:1, :], a8_ref[1:2, :])
    a_b = (a8_ref[2:3, :], a8_ref[3:4, :])
    a2_f = axpy(a_f, a_f, zero)
    a2_b = axpy(a_b, a_b, zero)

    def sweep(x, a, a2, r0, r1, col):
        re, im = slice(col, col + ns), slice(col + ns, col + 2 * ns)
        w0 = (w_ref[pl.ds(r0, 1), re], w_ref[pl.ds(r0, 1), im])
        w1 = (w_ref[pl.ds(r1, 1), re], w_ref[pl.ds(r1, 1), im])
        x1 = axpy(a, x, w0)
        w_ref[pl.ds(r0, 1), re] = x[0]
        w_ref[pl.ds(r0, 1), im] = x[1]
        w_ref[pl.ds(r1, 1), re] = x1[0]
        w_ref[pl.ds(r1, 1), im] = x1[1]
        return axpy(a2, x, axpy(a, w0, w1))

    def scan_step(i, carry):
        xf, xb = carry
        j = 2 * i
        jb = rows - 1 - j
        return sweep(xf, a_f, a2_f, j, j + 1, 0), sweep(xb, a_b, a2_b, jb, jb - 1, 2 * ns)

    lax.fori_loop(0, rows // 2, scan_step, (zero, zero))

    for r in range(0, rows, sub):
        y8 = (_dot(u8_ref[r:r + sub, :], tp_ref[...])
              + _dot(w_ref[r:r + sub, :].astype(BF16), mc_ref[...]))
        for t in range(S5_CHUNK):
            y_ref[pl.ds(r * S5_CHUNK + t, sub, stride=S5_CHUNK), :] = y8[:, t * S5_HALF:(t + 1) * S5_HALF]


def s5_mixer(u, mats, layer, bsz, seq):
    mb, tp, mc, a8 = mats
    n = bsz * seq
    rows = seq // S5_CHUNK
    width = S5_CHUNK * S5_HALF
    ns = (S5_HALF // S5_GROUP) * S5_STATE
    kern = functools.partial(_s5_kernel, sub=min(256, rows))
    wspec = lambda a: pl.BlockSpec((None, None) + a.shape[2:], lambda h, b: (layer, h, 0, 0),
                                   pipeline_mode=pl.Buffered(1))
    tokens = pl.BlockSpec((None, seq, S5_HALF), lambda h, b: (h, b, 0))
    return pl.pallas_call(
        kern,
        grid=(2, bsz),
        in_specs=[tokens, wspec(mb), wspec(tp), wspec(mc), wspec(a8)],
        out_specs=tokens,
        out_shape=jax.ShapeDtypeStruct((2, n, S5_HALF), F32),
        scratch_shapes=[pltpu.VMEM((rows, width), BF16), pltpu.VMEM((rows, 4 * ns), F32)],
        compiler_params=_params(("arbitrary", "arbitrary")),
        name="s5_mixer",
    )(u, mb, tp, mc, a8)


def _out_proj_kernel(x_ref, da_ref, ret_ref, y5_ref, gluw_ref, glub_ref, wout_ref,
                     g_ref, b_ref, o_ref, *, alpha):
    c1 = DA_WIDTH
    c2 = DA_WIDTH + RET_WIDTH
    for r in range(0, x_ref.shape[0], EPILOGUE_ROWS):
        rs = slice(r, r + EPILOGUE_ROWS)
        y = jnp.concatenate([y5_ref[0, rs, :], y5_ref[1, rs, :]], axis=-1)
        ya = _gelu_tanh(y)
        gate = _sigmoid(_dot(ya.astype(BF16), gluw_ref[...]) + glub_ref[...])
        ys5 = (ya * gate).astype(BF16)
        mix = (_dot(da_ref[rs, :], wout_ref[0:c1, :]) + _dot(ret_ref[rs, :], wout_ref[c1:c2, :])
               + _dot(ys5, wout_ref[c2:, :]))
        o_ref[rs, :] = _layer_norm(alpha * x_ref[rs, :] + mix, g_ref[...], b_ref[...])


def out_proj(x, y_da, y_ret, y5, layer, glu_w, glu_b, w_out, ln_g, ln_b, alpha, tm=ROW_TILE):
    n = x.shape[0]
    tm = min(tm, n)
    row = lambda i: (i, 0)
    return pl.pallas_call(
        functools.partial(_out_proj_kernel, alpha=alpha),
        grid=(n // tm,),
        in_specs=[pl.BlockSpec((tm, D_MODEL), row), pl.BlockSpec((tm, DA_WIDTH), row),
                  pl.BlockSpec((tm, RET_WIDTH), row),
                  pl.BlockSpec((2, tm, S5_HALF), lambda i: (0, i, 0)),
                  _layer_spec((S5_WIDTH, S5_WIDTH), layer), _const_spec((1, S5_WIDTH)),
                  _layer_spec((D_MODEL, D_MODEL), layer),
                  _const_spec((1, D_MODEL)), _const_spec((1, D_MODEL))],
        out_specs=pl.BlockSpec((tm, D_MODEL), row),
        out_shape=jax.ShapeDtypeStruct((n, D_MODEL), F32),
        compiler_params=_params(("parallel",)),
        name="out_proj",
    )(x, y_da, y_ret, y5, glu_w, glu_b, w_out, ln_g, ln_b)


def _ffn_kernel(x_ref, xp_ref, xn_ref, p_ref, wup_ref, cw_ref, cb_ref, wdn_ref,
                plew_ref, gatew_ref, g_ref, b_ref, o_ref, act_ref, *, alpha, tiles_per_seq):
    tm = x_ref.shape[0]
    i = pl.program_id(0)
    has_prev = ((i % tiles_per_seq) != 0).astype(F32)
    has_next = ((i % tiles_per_seq) != tiles_per_seq - 1).astype(F32)
    x = x_ref[...]
    xb = x.astype(BF16)
    xpb = xp_ref[...].astype(BF16)
    xnb = xn_ref[...].astype(BF16)
    halo = xp_ref.shape[0]
    row = lax.broadcasted_iota(jnp.int32, (halo, FF_CHUNK), 0)

    for c in range(0, D_FF, FF_CHUNK):
        wg = wup_ref[:, c:c + FF_CHUNK]
        gate = _dot(xb, wg)
        val = _dot(xb, wup_ref[:, D_FF + c:D_FF + c + FF_CHUNK])
        before = _dot(xpb, wg)[halo - 1:halo, :] * has_prev
        after = _dot(xnb, wg)[0:1, :] * has_next
        left = pltpu.roll(gate, 1, 0)
        left = jnp.concatenate([jnp.where(row == 0, before, left[:halo]), left[halo:]], axis=0)
        right = pltpu.roll(gate, tm - 1, 0)
        right = jnp.concatenate(
            [right[:tm - halo], jnp.where(row == halo - 1, after, right[tm - halo:])], axis=0)
        conv = (cw_ref[0:1, c:c + FF_CHUNK] * left + cw_ref[1:2, c:c + FF_CHUNK] * gate
                + cw_ref[2:3, c:c + FF_CHUNK] * right + cb_ref[:, c:c + FF_CHUNK])
        act_ref[:, c:c + FF_CHUNK] = (_gelu_tanh(conv) * val).astype(BF16)

    for r in range(0, tm, EPILOGUE_ROWS):
        rs = slice(r, r + EPILOGUE_ROWS)
        ple = (_dot(p_ref[rs, :].astype(BF16), plew_ref[...])
               * _sigmoid(_dot(xb[rs], gatew_ref[...])))
        f = _dot(act_ref[rs, :], wdn_ref[...])
        o_ref[rs, :] = _layer_norm(alpha * x[rs] + f + ple, g_ref[...], b_ref[...])


def conv_ffn_ple(x, p, layer, w_up, conv_w, conv_b, w_down, ple_w, gate_w, ln_g, ln_b, alpha, seq, tm=ROW_TILE):
    n = x.shape[0]
    tm = min(tm, seq)
    p_base = layer * (n // tm)
    halo = 8
    tiles_per_seq = seq // tm
    per = tm // halo
    last = n // halo - 1
    row = lambda i: (i, 0)
    kern = functools.partial(_ffn_kernel, alpha=alpha, tiles_per_seq=tiles_per_seq)
    return pl.pallas_call(
        kern,
        grid=(n // tm,),
        in_specs=[pl.BlockSpec((tm, D_MODEL), row),
                  pl.BlockSpec((halo, D_MODEL), lambda i: (jnp.maximum(i * per - 1, 0), 0)),
                  pl.BlockSpec((halo, D_MODEL), lambda i: (jnp.minimum((i + 1) * per, last), 0)),
                  pl.BlockSpec((tm, PLE_DIM), lambda i: (p_base + i, 0)),
                  _layer_spec((D_MODEL, 2 * D_FF), layer), _const_spec((3, D_FF)),
                  _const_spec((1, D_FF)),
                  _layer_spec((D_FF, D_MODEL), layer), _layer_spec((PLE_DIM, D_MODEL), layer),
                  _layer_spec((D_MODEL, D_MODEL), layer),
                  _const_spec((1, D_MODEL)), _const_spec((1, D_MODEL))],
        out_specs=pl.BlockSpec((tm, D_MODEL), row),
        out_shape=jax.ShapeDtypeStruct((n, D_MODEL), F32),
        scratch_shapes=[pltpu.VMEM((tm, D_FF), BF16)],
        compiler_params=_params(("parallel",)),
        name="conv_ffn_ple",
    )(x, x, x, p, w_up, conv_w, conv_b, w_down, ple_w, gate_w, ln_g, ln_b)


def kernel(x, p, positions, w_in, da_lambda_q1, da_lambda_k1, da_lambda_q2, da_lambda_k2,
           da_subln_g, ret_gn_g, ret_gn_b, s5_A_re, s5_A_im, s5_log_dt, s5_B_re, s5_B_im,
           s5_C_re, s5_C_im, s5_D, s5_glu_w, s5_glu_b, w_out, ln1_g, ln1_b,
           ffn_w_up, ffn_conv_w, ffn_conv_b, ffn_w_down, ple_w, ple_gate_w, ln2_g, ln2_b):
    bsz, seq, _ = x.shape
    depth = w_in.shape[0]
    n = bsz * seq
    alpha = (2 * depth) ** 0.25
    cos, sin = rope_tables(positions)
    xf = x.reshape(n, D_MODEL)
    p_all = p.reshape(depth * n, PLE_DIM)
    mats = s5_matrices(s5_A_re, s5_A_im, s5_log_dt, s5_B_re, s5_B_im, s5_C_re, s5_C_im, s5_D)
    row = lambda v: v.reshape(1, -1).astype(F32)
    w_in_b, glu_w_b, w_out_b = w_in.astype(BF16), s5_glu_w.astype(BF16), w_out.astype(BF16)
    w_up_b, w_down_b = ffn_w_up.astype(BF16), ffn_w_down.astype(BF16)
    ple_w_b, gate_w_b = ple_w.astype(BF16), ple_gate_w.astype(BF16)
    for i in range(depth):
        lambda_init = 0.8 - 0.6 * math.exp(-0.3 * i)
        lam = (jnp.exp(jnp.sum(da_lambda_q1[i].astype(F32) * da_lambda_k1[i].astype(F32)))
               - jnp.exp(jnp.sum(da_lambda_q2[i].astype(F32) * da_lambda_k2[i].astype(F32)))
               + lambda_init)
        da, ret, g, u = in_proj(xf, w_in_b, i, cos, sin)
        y_da = diff_attention(da, lam.reshape(1, 1), da_subln_g[i].astype(F32).reshape(-1, 1),
                              1.0 - lambda_init, bsz, seq)
        y_ret = retention(ret, g, ret_gn_g[i].astype(F32), ret_gn_b[i].astype(F32), bsz, seq)
        y5 = s5_mixer(u, mats, i, bsz, seq)
        x1 = out_proj(xf, y_da, y_ret, y5, i, glu_w_b, row(s5_glu_b[i]), w_out_b,
                      row(ln1_g[i]), row(ln1_b[i]), alpha)
        xf = conv_ffn_ple(x1, p_all, i, w_up_b, ffn_conv_w[i].astype(F32), row(ffn_conv_b[i]),
                          w_down_b, ple_w_b, gate_w_b, row(ln2_g[i]), row(ln2_b[i]), alpha, seq)
    return xf.reshape(bsz, seq, D_MODEL)
```
